```python
import jax
import jax.numpy as jnp
from jax import lax
import numpy as np

D_MODEL = 1024
BATCH = 2
SEQ = 8192
DEPTH = 4

GRID_W = 64
CTX_LEN = 256
HEAD_DIM = D_MODEL // 16
FNET_GROUPS = 4
FNET_GROUP_DIM = D_MODEL // 8
FNET_WIDTH = FNET_GROUPS * FNET_GROUP_DIM
RET_HEADS = 4
RET_QK_DIM = HEAD_DIM
RET_V_DIM = 2 * HEAD_DIM
RET_CHUNK = 128
EVEN_SPLITS = (FNET_WIDTH,
               FNET_WIDTH + RET_HEADS * RET_QK_DIM,
               FNET_WIDTH + 2 * RET_HEADS * RET_QK_DIM,
               FNET_WIDTH + 2 * RET_HEADS * RET_QK_DIM + RET_HEADS * RET_V_DIM)
EVEN_IN_WIDTH = FNET_WIDTH + 2 * RET_HEADS * RET_QK_DIM + 2 * RET_HEADS * RET_V_DIM
EVEN_OUT_WIDTH = FNET_WIDTH + RET_HEADS * RET_V_DIM
DIFF_HEADS = 8
DIFF_V_DIM = 2 * HEAD_DIM
ODD_IN_WIDTH = 3 * DIFF_HEADS * 2 * HEAD_DIM
ODD_OUT_WIDTH = DIFF_HEADS * DIFF_V_DIM
Q_BLOCK = 128
ROPE_BASE = 10000.0
FFN_DIM = 256 * ((8 * D_MODEL // 3 + 255) // 256)
N_EXPERTS = 8
TOP_K = 2
EXPERT_DIM = 7 * D_MODEL // 2
N_MOD = 6
EPS = 1e-6

kernel_name = 'hybrid_fnet_retention_diffattn_moe_dit'


def rms_norm(x, gain=None):
    xf = x.astype(jnp.float32)
    y = xf * lax.rsqrt(jnp.mean(xf * xf, axis=-1, keepdims=True) + EPS)
    if gain is not None:
        y = y * gain.astype(jnp.float32)
    return y.astype(x.dtype)


def modulate(x, shift, scale):
    return rms_norm(x) * (1.0 + scale) + shift


def adaln_terms(cond, w_mod, b_mod):
    return jnp.split(jax.nn.silu(cond) @ w_mod + b_mod, N_MOD, axis=-1)


def axial_rope_tables(n_tokens, dim):
    rows = n_tokens // GRID_W
    row = jnp.repeat(jnp.arange(rows, dtype=jnp.float32), GRID_W)
    col = jnp.tile(jnp.arange(GRID_W, dtype=jnp.float32), rows)
    quarter = dim // 4
    inv_freq = ROPE_BASE ** (-jnp.arange(quarter, dtype=jnp.float32) / quarter)
    ang = jnp.concatenate([row[:, None] * inv_freq, col[:, None] * inv_freq], axis=-1)
    return jnp.cos(ang), jnp.sin(ang)


def apply_rope(x, cos, sin):
    half = x.shape[-1] // 2
    xf = x.astype(jnp.float32)
    x1, x2 = xf[..., :half], xf[..., half:]
    return jnp.concatenate([x1 * cos - x2 * sin, x2 * cos + x1 * sin], axis=-1).astype(x.dtype)


def swiglu(h, w1, w3, w2):
    return (jax.nn.silu(h @ w1) * (h @ w3)) @ w2


def moe_swiglu(h, w_router, w1, w3, w2):
    logits = jnp.einsum('bnd,de->bne', h, w_router).astype(jnp.float32)
    top_vals, top_idx = lax.top_k(logits, TOP_K)
    gates = jax.nn.softmax(top_vals, axis=-1)
    combine = jnp.einsum('bnk,bnke->bne', gates, jax.nn.one_hot(top_idx, N_EXPERTS, dtype=jnp.float32))
    out = jnp.zeros_like(h)
    for e in range(N_EXPERTS):
        out = out + combine[..., e:e + 1].astype(h.dtype) * swiglu(h, w1[e], w3[e], w2[e])
    return out


def fourier_mix(f):
    b, n, _ = f.shape
    u = f.reshape(b, n, FNET_GROUPS, FNET_GROUP_DIM).astype(jnp.float32)
    y = jnp.fft.fft2(u, axes=(1, 3), norm='ortho').real
    return y.reshape(b, n, FNET_WIDTH).astype(f.dtype)


def retention_chunkwise(q, k, v, log_gamma, state0, include_diag):
    b, h, n, _ = q.shape
    dv = v.shape[-1]
    nc = n // RET_CHUNK
    pos = jnp.arange(RET_CHUNK, dtype=jnp.float32)
    rel = pos[:, None] - pos[None, :]
    mask = rel >= 0 if include_diag else rel > 0
    lg = log_gamma[:, None, None]
    decay_intra = jnp.where(mask, jnp.exp(lg * jnp.where(mask, rel, 0.0)), 0.0)
    decay_q = jnp.exp(log_gamma[:, None] * (pos + 1.0))[:, :, None]
    decay_k = jnp.exp(log_gamma[:, None] * (RET_CHUNK - 1.0 - pos))[:, :, None]
    decay_chunk = jnp.exp(log_gamma * RET_CHUNK)[:, None, None]

    def to_chunks(t):
        return t.reshape(b, h, nc, RET_CHUNK, t.shape[-1]).transpose(2, 0, 1, 3, 4)

    def step(state, qkv):
        qc, kc, vc = qkv
        scores = jnp.einsum('bhid,bhjd->bhij', qc, kc) * decay_intra
        out = (jnp.einsum('bhij,bhje->bhie', scores, vc)
               + jnp.einsum('bhid,bhde->bhie', qc * decay_q, state))
        state = decay_chunk * state + jnp.einsum('bhjd,bhje->bhde', kc * decay_k, vc)
        return state, out

    state, out = lax.scan(step, state0, (to_chunks(q), to_chunks(k), to_chunks(v)))
    return out.transpose(1, 2, 0, 3, 4).reshape(b, h, n, dv), state


def bidir_retention(qc, kc, vc, ql, kl, vl, lg_f, lg_b):
    zero = jnp.zeros((qc.shape[0], RET_HEADS, RET_QK_DIM, RET_V_DIM), jnp.float32)
    flip = lambda t: jnp.flip(t, axis=2)
    oc_f, s_f = retention_chunkwise(qc, kc, vc, lg_f, zero, True)
    ol_f, _ = retention_chunkwise(ql, kl, vl, lg_f, s_f, True)
    oc_b, s_b = retention_chunkwise(flip(qc), flip(kc), flip(vc), lg_b, zero, False)
    ol_b, _ = retention_chunkwise(flip(ql), flip(kl), flip(vl), lg_b, s_b, False)
    return oc_f + flip(oc_b), ol_f + flip(ol_b)


def even_mixer(h_ctx, h_lat, w_in, w_out, dec_f, dec_b, cos, sin):
    lg_f = jax.nn.log_sigmoid(dec_f.astype(jnp.float32))
    lg_b = jax.nn.log_sigmoid(dec_b.astype(jnp.float32))

    def project(h, rope):
        b, n, _ = h.shape
        f, q, k, v, g = jnp.split(h @ w_in, EVEN_SPLITS, axis=-1)
        heads = lambda t, d: t.reshape(b, n, RET_HEADS, d).transpose(0, 2, 1, 3).astype(jnp.float32)
        q, k, v = heads(q, RET_QK_DIM), heads(k, RET_QK_DIM), heads(v, RET_V_DIM)
        if rope:
            q, k = apply_rope(q, cos, sin), apply_rope(k, cos, sin)
        return f, q, k * RET_QK_DIM ** -0.5, v, g

    fc, qc, kc, vc, gc = project(h_ctx, False)
    fl, ql, kl, vl, gl = project(h_lat, True)
    oc, ol = bidir_retention(qc, kc, vc, ql, kl, vl, lg_f, lg_b)

    def merge(f, o, g):
        b, n, _ = f.shape
        o = rms_norm(o).transpose(0, 2, 1, 3).reshape(b, n, RET_HEADS * RET_V_DIM).astype(g.dtype)
        return jnp.concatenate([fourier_mix(f), o * jax.nn.silu(g)], axis=-1) @ w_out

    return merge(fc, oc, gc), merge(fl, ol, gl)


def diff_attend(q, k, v, lam):
    s = jnp.einsum('bhmqd,bhmkd->bhmqk', q, k).astype(jnp.float32) * HEAD_DIM ** -0.5
    p = jax.nn.softmax(s, axis=-1)
    a = p[:, :, 0] - lam * p[:, :, 1]
    return jnp.einsum('bhqk,bhkv->bhqv', a.astype(v.dtype), v)


def odd_mixer(h_ctx, h_lat, w_in, w_out, q_gain, k_gain, lq1, lk1, lq2, lk2, sub_gain,
              lam_init, cos, sin, with_ctx_out):
    f32 = jnp.float32
    lam = (jnp.exp(jnp.sum(lq1.astype(f32) * lk1.astype(f32)))
           - jnp.exp(jnp.sum(lq2.astype(f32) * lk2.astype(f32))) + lam_init)

    def project(h, rope):
        b, n, _ = h.shape
        q, k, v = jnp.split(h @ w_in, 3, axis=-1)
        q = rms_norm(q.reshape(b, n, DIFF_HEADS, 2, HEAD_DIM).transpose(0, 2, 3, 1, 4), q_gain)
        k = rms_norm(k.reshape(b, n, DIFF_HEADS, 2, HEAD_DIM).transpose(0, 2, 3, 1, 4), k_gain)
        if rope:
            q, k = apply_rope(q, cos, sin), apply_rope(k, cos, sin)
        v = v.reshape(b, n, DIFF_HEADS, DIFF_V_DIM).transpose(0, 2, 1, 3)
        return q, k, v

    qc, kc, vc = project(h_ctx, False)
    ql, kl, vl = project(h_lat, True)
    k_all = jnp.concatenate([kc, kl], axis=3)
    v_all = jnp.concatenate([vc, vl], axis=2)

    def finish(o):
        b, _, n, _ = o.shape
        o = rms_norm(o, sub_gain) * (1.0 - lam_init)
        return o.transpose(0, 2, 1, 3).reshape(b, n, ODD_OUT_WIDTH) @ w_out

    b, _, _, n, _ = ql.shape
    nb = n // Q_BLOCK
    q_blocks = ql.reshape(b, DIFF_HEADS, 2, nb, Q_BLOCK, HEAD_DIM).transpose(3, 0, 1, 2, 4, 5)
    o_lat = lax.map(lambda qb: diff_attend(qb, k_all, v_all, lam), q_blocks)
    o_lat = o_lat.transpose(1, 2, 0, 3, 4).reshape(b, DIFF_HEADS, n, DIFF_V_DIM)
    out_ctx = finish(diff_attend(qc, kc, vc, lam)) if with_ctx_out else None
    return out_ctx, finish(o_lat)


def setup_inputs(seed: int = 0) -> dict:
    key = jax.random.key(seed)
    ks = jax.random.split(key, 26)
    n_even = (DEPTH + 1) // 2
    n_odd = DEPTH // 2
    f32 = jnp.float32
    d = D_MODEL

    def nrm(k, shape, scale):
        return scale * jax.random.normal(k, shape, f32)

    decay_logit = jnp.log(2.0 ** (5.0 + jnp.arange(RET_HEADS, dtype=f32)) - 1.0)
    return {
        'x': nrm(ks[0], (BATCH, SEQ, d), 1.0),
        'c': nrm(ks[1], (BATCH, d), 1.0),
        'ctx': nrm(ks[2], (BATCH, CTX_LEN, d), 1.0),
        'c_ctx': nrm(ks[3], (d,), 1.0),
        'w_mod': nrm(ks[4], (DEPTH, d, N_MOD * d), 0.5 * d ** -0.5),
        'b_mod': nrm(ks[5], (DEPTH, N_MOD * d), 0.02),
        'w_in_even': nrm(ks[6], (n_even, d, EVEN_IN_WIDTH), d ** -0.5),
        'w_out_even': nrm(ks[7], (n_even, EVEN_OUT_WIDTH, d), EVEN_OUT_WIDTH ** -0.5),
        'ret_decay_fwd': decay_logit + nrm(ks[8], (n_even, RET_HEADS), 0.1),
        'ret_decay_bwd': decay_logit + nrm(ks[9], (n_even, RET_HEADS), 0.1),
        'ffn_w1': nrm(ks[10], (n_even, d, FFN_DIM), d ** -0.5),
        'ffn_w3': nrm(ks[11], (n_even, d, FFN_DIM), d ** -0.5),
        'ffn_w2': nrm(ks[12], (n_even, FFN_DIM, d), FFN_DIM ** -0.5),
        'w_in_odd': nrm(ks[13], (n_odd, d, ODD_IN_WIDTH), d ** -0.5),
        'w_out_odd': nrm(ks[14], (n_odd, ODD_OUT_WIDTH, d), ODD_OUT_WIDTH ** -0.5),
        'q_norm_gain': 1.0 + nrm(ks[15], (n_odd, HEAD_DIM), 0.02),
        'k_norm_gain': 1.0 + nrm(ks[16], (n_odd, HEAD_DIM), 0.02),
        'lambda_q1': nrm(ks[17], (n_odd, HEAD_DIM), 0.1),
        'lambda_k1': nrm(ks[18], (n_odd, HEAD_DIM), 0.1),
        'lambda_q2': nrm(ks[19], (n_odd, HEAD_DIM), 0.1),
        'lambda_k2': nrm(ks[20], (n_odd, HEAD_DIM), 0.1),
        'subln_gain': 1.0 + nrm(ks[21], (n_odd, DIFF_V_DIM), 0.02),
        'w_router': nrm(ks[22], (n_odd, d, N_EXPERTS), d ** -0.5),
        'moe_w1': nrm(ks[23], (n_odd, N_EXPERTS, d, EXPERT_DIM), d ** -0.5),
        'moe_w3': nrm(ks[24], (n_odd, N_EXPERTS, d, EXPERT_DIM), d ** -0.5),
        'moe_w2': nrm(ks[25], (n_odd, N_EXPERTS, EXPERT_DIM, d), EXPERT_DIM ** -0.5),
    }


def reference(x, c, ctx, c_ctx, w_mod, b_mod, w_in_even, w_out_even, ret_decay_fwd, ret_decay_bwd,
              ffn_w1, ffn_w3, ffn_w2, w_in_odd, w_out_odd, q_norm_gain, k_norm_gain,
              lambda_q1, lambda_k1, lambda_q2, lambda_k2, subln_gain, w_router, moe_w1, moe_w3, moe_w2):
    cos, sin = axial_rope_tables(x.shape[1], HEAD_DIM)
    x_lat, x_ctx = x, ctx
    n_ctx = ctx.shape[1]
    for layer in range(DEPTH):
        i = layer // 2
        last = layer == DEPTH - 1
        mod_lat = [m[:, None, :] for m in adaln_terms(c, w_mod[layer], b_mod[layer])]
        mod_ctx = adaln_terms(c_ctx, w_mod[layer], b_mod[layer])
        h_lat = modulate(x_lat, mod_lat[0], mod_lat[1])
        h_ctx = modulate(x_ctx, mod_ctx[0], mod_ctx[1])
        if layer % 2 == 0:
            o_ctx, o_lat = even_mixer(h_ctx, h_lat, w_in_even[i], w_out_even[i],
                                      ret_decay_fwd[i], ret_decay_bwd[i], cos, sin)
        else:
            lam_init = 0.8 - 0.6 * float(np.exp(-0.3 * layer))
            o_ctx, o_lat = odd_mixer(h_ctx, h_lat, w_in_odd[i], w_out_odd[i], q_norm_gain[i], k_norm_gain[i],
                                     lambda_q1[i], lambda_k1[i], lambda_q2[i], lambda_k2[i], subln_gain[i],
                                     lam_init, cos, sin, not last)
        x_lat = x_lat + mod_lat[2] * o_lat
        h_lat = modulate(x_lat, mod_lat[3], mod_lat[4])
        if last:
            h_all = h_lat
        else:
            x_ctx = x_ctx + mod_ctx[2] * o_ctx
            h_all = jnp.concatenate([modulate(x_ctx, mod_ctx[3], mod_ctx[4]), h_lat], axis=1)
        if layer % 2 == 0:
            y = swiglu(h_all, ffn_w1[i], ffn_w3[i], ffn_w2[i])
        else:
            y = moe_swiglu(h_all, w_router[i], moe_w1[i], moe_w3[i], moe_w2[i])
        if last:
            x_lat = x_lat + mod_lat[5] * y
        else:
            x_ctx = x_ctx + mod_ctx[5] * y[:, :n_ctx]
            x_lat = x_lat + mod_lat[5] * y[:, n_ctx:]
    return x_lat
```

```python
import functools
import math

import jax
import jax.numpy as jnp
import numpy as np
from jax import lax
from jax.experimental import pallas as pl
from jax.experimental.pallas import tpu as pltpu

F32 = jnp.float32
BF16 = jnp.bfloat16

D_MODEL = 1024
GRID_W = 64
HEAD_DIM = 64
LANES = 128
FNET_GROUPS = 4
FNET_GROUP_DIM = 128
FNET_WIDTH = FNET_GROUPS * FNET_GROUP_DIM
RET_HEADS = 4
RET_CHUNK = 128
DIFF_HEADS = 8
N_EXPERTS = 8
N_MOD = 6
ROPE_BASE = 10000.0
EPS = 1e-6
VMEM_LIMIT_BYTES = 56 * 1024 * 1024
MOD_ROWS = 8


def _cparams(*sem):
    return pltpu.CompilerParams(dimension_semantics=sem, vmem_limit_bytes=VMEM_LIMIT_BYTES)


def _pick_tile(n, candidates):
    for c in candidates:
        if n % c == 0:
            return c
    raise ValueError(f"no tile in {candidates} divides {n}")


def _const_spec(shape):
    nd = len(shape)
    return pl.BlockSpec(shape, lambda *_: (0,) * nd)


def _resident_spec(shape):
    nd = len(shape)
    return pl.BlockSpec(shape, lambda *_: (0,) * nd, pipeline_mode=pl.Buffered(1))


def _sigmoid(x):
    return 1.0 / (1.0 + jnp.exp(-x))


def _modulated(x, modl_ref, modc_ref, k_shift, row0, n_lat):
    tm = x.shape[0]
    ms = jnp.mean(x * x, axis=-1, keepdims=True)
    xn = x * lax.rsqrt(ms + EPS)
    rows = row0 + lax.broadcasted_iota(jnp.int32, (tm, 1), 0)
    is_ctx = rows >= n_lat
    shift = jnp.where(is_ctx, modc_ref[0, k_shift:k_shift + 1, :], modl_ref[0, k_shift:k_shift + 1, :])
    scale = jnp.where(is_ctx, modc_ref[0, k_shift + 1:k_shift + 2, :], modl_ref[0, k_shift + 1:k_shift + 2, :])
    return xn * (1.0 + scale) + shift


def _gate_rows(modl_ref, modc_ref, k_gate, row0, tm, n_lat):
    rows = row0 + lax.broadcasted_iota(jnp.int32, (tm, 1), 0)
    return jnp.where(rows >= n_lat, modc_ref[0, k_gate:k_gate + 1, :], modl_ref[0, k_gate:k_gate + 1, :])


def _adaln_kernel(c_ref, w_ref, b_ref, o_ref):
    c = c_ref[...]
    s = c * _sigmoid(c)
    o_ref[0] = jnp.dot(s, w_ref[0], preferred_element_type=F32, precision=lax.Precision.HIGHEST) + b_ref[0]


def _adaln(cond_rows, w_mod, b_mod):
    depth, d, n = w_mod.shape
    tn = _pick_tile(n, (1536, 1024, 512))
    return pl.pallas_call(
        _adaln_kernel,
        grid=(depth, n // tn),
        in_specs=[
            _const_spec((MOD_ROWS, d)),
            pl.BlockSpec((1, d, tn), lambda l, j: (l, 0, j)),
            pl.BlockSpec((1, 1, tn), lambda l, j: (l, 0, j)),
        ],
        out_specs=pl.BlockSpec((1, MOD_ROWS, tn), lambda l, j: (l, 0, j)),
        out_shape=jax.ShapeDtypeStruct((depth, MOD_ROWS, n), F32),
        compiler_params=_cparams("parallel", "parallel"),
        name="adaln",
    )(cond_rows, w_mod, b_mod.reshape(depth, 1, n))


def _rope_dup(blk, tab):
    t = blk * tab
    return t + pltpu.roll(t, HEAD_DIM, axis=1)


def _inproj_even_kernel(x_ref, modl_ref, modc_ref, w_ref, tab_ref, o_ref, *, n_lat, tm):
    row0 = pl.program_id(1) * tm
    h = _modulated(x_ref[0], modl_ref, modc_ref, 0, row0, n_lat).astype(BF16)
    tab = tab_ref[...]
    lane = lax.broadcasted_iota(jnp.int32, (tm, LANES), 1)
    n_out = o_ref.shape[2]
    for c in range(n_out // 256):
        y = jnp.dot(h, w_ref[:, c * 256:(c + 1) * 256], preferred_element_type=F32)
        for half in range(2):
            blk = y[:, half * LANES:(half + 1) * LANES]
            col = c * 2 + half
            if 4 <= col < 8:
                blk = _rope_dup(blk, tab)
            elif 8 <= col < 12:
                blk = jnp.where(lane < HEAD_DIM, _rope_dup(blk, tab) * (HEAD_DIM ** -0.5), 0.0)
            o_ref[0, :, col * LANES:(col + 1) * LANES] = blk.astype(BF16)


def _inproj_odd_kernel(x_ref, modl_ref, modc_ref, w_ref, tab_ref, gq_ref, gk_ref, o_ref, *, n_lat, tm):
    row0 = pl.program_id(1) * tm
    h = _modulated(x_ref[0], modl_ref, modc_ref, 0, row0, n_lat).astype(BF16)
    tab = tab_ref[...]
    lane = lax.broadcasted_iota(jnp.int32, (tm, LANES), 1)
    n_qk = 2 * DIFF_HEADS
    for c in range(n_qk):
        y = jnp.dot(h, w_ref[:, c * 256:(c + 1) * 256], preferred_element_type=F32)
        gain = gq_ref[...] * (HEAD_DIM ** -0.5) if c < DIFF_HEADS else gk_ref[...]
        subs = []
        for half in range(2):
            blk = y[:, half * LANES:(half + 1) * LANES]
            ms = jnp.sum(blk * blk, axis=-1, keepdims=True) * (1.0 / LANES)
            subs.append(_rope_dup(blk * lax.rsqrt(ms + EPS) * gain, tab))
        o_ref[0, :, c * LANES:(c + 1) * LANES] = jnp.where(lane < HEAD_DIM, subs[0], subs[1]).astype(BF16)
    for c in range(DIFF_HEADS // 2):
        col0 = n_qk * 256 + c * 256
        y = jnp.dot(h, w_ref[:, col0:col0 + 256], preferred_element_type=F32)
        out0 = n_qk * LANES + c * 256
        o_ref[0, :, out0:out0 + 256] = y.astype(BF16)


def _inproj(x, mod, w, tab, gains, n_lat, n_out, even):
    b, t, d = x.shape
    tm = _pick_tile(t, (768, 256))
    kern = _inproj_even_kernel if even else _inproj_odd_kernel
    in_specs = [
        pl.BlockSpec((1, tm, d), lambda bi, i: (bi, i, 0)),
        pl.BlockSpec((1, N_MOD, d), lambda bi, i: (bi, 0, 0)),
        pl.BlockSpec((1, N_MOD, d), lambda bi, i: (b, 0, 0)),
        _resident_spec(w.shape),
        pl.BlockSpec((tm, LANES), lambda bi, i: (i, 0)),
    ] + [_const_spec((1, LANES))] * len(gains)
    return pl.pallas_call(
        functools.partial(kern, n_lat=n_lat, tm=tm),
        grid=(b, t // tm),
        in_specs=in_specs,
        out_specs=pl.BlockSpec((1, tm, n_out), lambda bi, i: (bi, i, 0)),
        out_shape=jax.ShapeDtypeStruct((b, t, n_out), BF16),
        compiler_params=_cparams("parallel", "parallel"),
        name="inproj_even" if even else "inproj_odd",
    )(x, mod, mod, w, tab, *gains)


def _dft_kernel(c_ref, s_ref, u_ref, cc_ref, sc_ref, o_ref, acc_c, acc_s, *, nb, scale):
    j = pl.program_id(1)

    @pl.when(j == 0)
    def _():
        acc_c[...] = jnp.zeros_like(acc_c)
        acc_s[...] = jnp.zeros_like(acc_s)

    cm = c_ref[...]
    sm = s_ref[...]
    for bi in range(nb):
        u = u_ref[bi]
        acc_c[bi] += jnp.dot(cm, u, preferred_element_type=F32)
        acc_s[bi] += jnp.dot(sm, u, preferred_element_type=F32)

    @pl.when(j == pl.num_programs(1) - 1)
    def _():
        cc = cc_ref[...]
        sc = sc_ref[...]
        for bi in range(nb):
            for g in range(FNET_GROUPS):
                sl = slice(g * FNET_GROUP_DIM, (g + 1) * FNET_GROUP_DIM)
                a = acc_c[bi, :, sl].astype(BF16)
                bm = acc_s[bi, :, sl].astype(BF16)
                y = jnp.dot(a, cc, preferred_element_type=F32) - jnp.dot(bm, sc, preferred_element_type=F32)
                o_ref[bi, :, sl] = (y * scale).astype(BF16)


def _dft_tables(n):
    idx = jnp.arange(n, dtype=jnp.int32)
    prod = (idx[:, None] * idx[None, :]) % n
    ang = prod.astype(F32) * (2.0 * math.pi / n)
    return jnp.cos(ang).astype(BF16), jnp.sin(ang).astype(BF16)


def _fourier_seq(hin, row0, n, chan_tabs):
    b = hin.shape[0]
    cm, sm = _dft_tables(n)
    cc, sc = chan_tabs
    tk = _pick_tile(n, (1024, 512, 256))
    tn = tk
    off = row0 // tn
    return pl.pallas_call(
        functools.partial(_dft_kernel, nb=b, scale=1.0 / math.sqrt(n * FNET_GROUP_DIM)),
        grid=(n // tk, n // tn),
        in_specs=[
            pl.BlockSpec((tk, tn), lambda i, j: (i, j)),
            pl.BlockSpec((tk, tn), lambda i, j: (i, j)),
            pl.BlockSpec((b, tn, FNET_WIDTH), lambda i, j: (0, off + j, 0)),
            _const_spec((FNET_GROUP_DIM, FNET_GROUP_DIM)),
            _const_spec((FNET_GROUP_DIM, FNET_GROUP_DIM)),
        ],
        out_specs=pl.BlockSpec((b, tk, FNET_WIDTH), lambda i, j: (0, i, 0)),
        out_shape=jax.ShapeDtypeStruct((b, n, FNET_WIDTH), BF16),
        scratch_shapes=[pltpu.VMEM((b, tk, FNET_WIDTH), F32), pltpu.VMEM((b, tk, FNET_WIDTH), F32)],
        compiler_params=_cparams("parallel", "arbitrary"),
        name=f"fourier_seq_{n}",
    )(cm, sm, hin, cc, sc)


def _log_sigmoid(x):
    return jnp.minimum(x, 0.0) - jnp.log(1.0 + jnp.exp(-jnp.abs(x)))


def _retention_kernel(q_ref, k_ref, v_ref, g_ref, decf_ref, decb_ref, o_ref,
                      of_scr, ob_scr, st_scr, dec_scr, *, n_lat, n_ctx):
    c = RET_CHUNK
    pos_i = lax.broadcasted_iota(jnp.int32, (c, c), 0).astype(F32)
    pos_j = lax.broadcasted_iota(jnp.int32, (c, c), 1).astype(F32)
    lg_f = _log_sigmoid(decf_ref[0])
    lg_b = _log_sigmoid(decb_ref[0])
    rel = pos_i - pos_j
    mask_f = rel >= 0.0
    mask_b = rel < 0.0
    dec_scr[0] = jnp.where(mask_f, jnp.exp(lg_f * jnp.where(mask_f, rel, 0.0)), 0.0)
    dec_scr[1] = jnp.where(mask_b, jnp.exp(lg_b * jnp.where(mask_b, -rel, 0.0)), 0.0)
    dec_scr[2] = jnp.exp(lg_f * (pos_i + 1.0))
    dec_scr[3] = jnp.exp(lg_b * (c - pos_i))
    dec_scr[4] = jnp.exp(lg_f * (c - 1.0 - pos_i))
    dec_scr[5] = jnp.exp(lg_b * pos_i)
    dec_scr[6] = jnp.exp(lg_f * (c + 0.0 * pos_i))
    dec_scr[7] = jnp.exp(lg_b * (c + 0.0 * pos_i))
    st_scr[...] = jnp.zeros_like(st_scr)

    def one_dir(d, start, out_scr):
        qc = q_ref[0, pl.ds(start, c), :]
        kc = k_ref[0, pl.ds(start, c), :]
        vc = v_ref[0, pl.ds(start, c), :]
        state = st_scr[d]
        scores = lax.dot_general(qc, kc, (((1,), (1,)), ((), ())), preferred_element_type=F32)
        intra = jnp.dot((scores * dec_scr[d]).astype(BF16), vc, preferred_element_type=F32)
        inter = jnp.dot(qc, state.astype(BF16), preferred_element_type=F32)
        out_scr[pl.ds(start, c), :] = intra + dec_scr[2 + d] * inter
        vd = (vc.astype(F32) * dec_scr[4 + d]).astype(BF16)
        kv = lax.dot_general(kc, vd, (((0,), (0,)), ((), ())), preferred_element_type=F32)
        st_scr[d] = dec_scr[6 + d] * state + kv

    n_ctx_chunks = n_ctx // c
    for ci in range(n_ctx_chunks):
        one_dir(0, n_lat + ci * c, of_scr)
        one_dir(1, n_lat + (n_ctx_chunks - 1 - ci) * c, ob_scr)

    n_lat_chunks = n_lat // c

    def body(i, carry):
        one_dir(0, pl.multiple_of(i * c, c), of_scr)
        one_dir(1, pl.multiple_of((n_lat_chunks - 1 - i) * c, c), ob_scr)
        return carry

    lax.fori_loop(0, n_lat_chunks, body, 0)

    def finish(i, carry):
        r0 = pl.multiple_of(i * c, c)
        o = of_scr[pl.ds(r0, c), :] + ob_scr[pl.ds(r0, c), :]
        o = o * lax.rsqrt(jnp.mean(o * o, axis=-1, keepdims=True) + EPS)
        g = g_ref[0, pl.ds(r0, c), :].astype(F32)
        o_ref[0, pl.ds(r0, c), :] = (o * (g * _sigmoid(g))).astype(BF16)
        return carry

    lax.fori_loop(0, (n_lat + n_ctx) // c, finish, 0)


def _retention(hin, dec_f, dec_b, n_lat):
    b, t, _ = hin.shape
    h = RET_HEADS
    seq = lambda col0: pl.BlockSpec((1, t, LANES), lambda bi, hi: (bi, 0, col0 + hi))
    dec = pl.BlockSpec((1, 1, LANES), lambda bi, hi: (hi, 0, 0))
    return pl.pallas_call(
        functools.partial(_retention_kernel, n_lat=n_lat, n_ctx=t - n_lat),
        grid=(b, h),
        in_specs=[seq(4), seq(8), seq(12), seq(16), dec, dec],
        out_specs=pl.BlockSpec((1, t, LANES), lambda bi, hi: (bi, 0, hi)),
        out_shape=jax.ShapeDtypeStruct((b, t, h * LANES), BF16),
        scratch_shapes=[
            pltpu.VMEM((t, LANES), F32),
            pltpu.VMEM((t, LANES), F32),
            pltpu.VMEM((2, LANES, LANES), F32),
            pltpu.VMEM((8, RET_CHUNK, LANES), F32),
        ],
        compiler_params=_cparams("parallel", "parallel"),
        name="retention",
    )(hin, hin, hin, hin, dec_f, dec_b)


def _merge_kernel(*refs, n_parts, n_lat, tm):
    x_ref, modl_ref, modc_ref = refs[:3]
    a_refs = refs[3:3 + n_parts]
    w_refs = refs[3 + n_parts:3 + 2 * n_parts]
    o_ref = refs[3 + 2 * n_parts]
    row0 = pl.program_id(1) * tm
    y = jnp.dot(a_refs[0][0], w_refs[0][...], preferred_element_type=F32)
    for p in range(1, n_parts):
        y = y + jnp.dot(a_refs[p][0], w_refs[p][...], preferred_element_type=F32)
    gate = _gate_rows(modl_ref, modc_ref, 2, row0, tm, n_lat)
    o_ref[0] = x_ref[0] + gate * y


def _merge(x, mod, parts, weights, n_lat):
    b, t, d = x.shape
    tm = _pick_tile(t, (768, 256))
    n_parts = len(parts)
    in_specs = [
        pl.BlockSpec((1, tm, d), lambda bi, i: (bi, i, 0)),
        pl.BlockSpec((1, N_MOD, d), lambda bi, i: (bi, 0, 0)),
        pl.BlockSpec((1, N_MOD, d), lambda bi, i: (b, 0, 0)),
    ]
    in_specs += [pl.BlockSpec((1, tm, p.shape[2]), lambda bi, i: (bi, i, 0)) for p in parts]
    in_specs += [_resident_spec(w.shape) for w in weights]
    return pl.pallas_call(
        functools.partial(_merge_kernel, n_parts=n_parts, n_lat=n_lat, tm=tm),
        grid=(b, t // tm),
        in_specs=in_specs,
        out_specs=pl.BlockSpec((1, tm, d), lambda bi, i: (bi, i, 0)),
        out_shape=jax.ShapeDtypeStruct((b, t, d), F32),
        input_output_aliases={0: 0},
        compiler_params=_cparams("parallel", "parallel"),
        name=f"merge_{n_parts}",
    )(x, mod, mod, *parts, *weights)


def _attn_kernel(q_ref, k_ref, v_ref, lam_ref, sg_ref, o_ref, m_scr, l_scr, acc_scr,
                 *, tk, lam_init):
    tq = q_ref.shape[1]
    n_keys = k_ref.shape[1]
    q = q_ref[0]
    lane = lax.broadcasted_iota(jnp.int32, (tq, LANES), 1)
    zero = jnp.zeros_like(q)
    q_sub = (jnp.where(lane < HEAD_DIM, q, zero), jnp.where(lane >= HEAD_DIM, q, zero))
    m_scr[...] = jnp.full_like(m_scr, -jnp.inf)
    l_scr[...] = jnp.zeros_like(l_scr)
    acc_scr[...] = jnp.zeros_like(acc_scr)

    def body(j, carry):
        k0 = pl.multiple_of(j * tk, tk)
        kc = k_ref[0, pl.ds(k0, tk), :]
        vc = v_ref[0, pl.ds(k0, tk), :]
        for sub in range(2):
            s = lax.dot_general(q_sub[sub], kc, (((1,), (1,)), ((), ())), preferred_element_type=F32)
            m_old = m_scr[sub]
            m_new = jnp.maximum(m_old, jnp.max(s, axis=-1, keepdims=True))
            alpha = jnp.exp(m_old - m_new)
            p = jnp.exp(s - m_new)
            l_scr[sub] = alpha * l_scr[sub] + jnp.sum(p, axis=-1, keepdims=True)
            acc_scr[sub] = alpha * acc_scr[sub] + jnp.dot(p.astype(BF16), vc, preferred_element_type=F32)
            m_scr[sub] = m_new
        return carry

    lax.fori_loop(0, n_keys // tk, body, 0)

    lv = lam_ref[...]
    s1 = jnp.sum(lv[0:1] * lv[1:2], axis=-1, keepdims=True)
    s2 = jnp.sum(lv[2:3] * lv[3:4], axis=-1, keepdims=True)
    lam = jnp.exp(s1) - jnp.exp(s2) + lam_init
    o = acc_scr[0] / l_scr[0] - lam * (acc_scr[1] / l_scr[1])
    o = o * lax.rsqrt(jnp.mean(o * o, axis=-1, keepdims=True) + EPS)
    o_ref[0] = (o * sg_ref[...] * (1.0 - lam_init)).astype(BF16)


def _attention(qkv, lam_vecs, sub_gain, lam_init, n_lat, prev=None):
    b, t, _ = qkv.shape
    h = DIFF_HEADS
    n_ctx = t - n_lat
    if prev is None:
        tq = _pick_tile(n_lat, (512, 256))
        n_q_tiles, q_off = n_lat // tq, 0
        keys, k_off = t, 0
        tk = _pick_tile(t, (768, 256))
    else:
        tq, n_q_tiles, q_off = n_ctx, 1, n_lat // n_ctx
        keys, k_off = n_ctx, n_lat // n_ctx
        tk = n_ctx
    kern = functools.partial(_attn_kernel, tk=tk, lam_init=lam_init)
    if prev is not None:
        body = kern
        kern = lambda prev_ref, *refs: body(*refs)
    in_specs = [
        pl.BlockSpec((1, tq, LANES), lambda bi, hi, i: (bi, q_off + i, hi)),
        pl.BlockSpec((1, keys, LANES), lambda bi, hi, i: (bi, k_off, h + hi)),
        pl.BlockSpec((1, keys, LANES), lambda bi, hi, i: (bi, k_off, 2 * h + hi)),
        _const_spec((4, LANES)),
        _const_spec((1, LANES)),
    ]
    args = [qkv, qkv, qkv, lam_vecs, sub_gain]
    aliases = {}
    if prev is not None:
        in_specs = [pl.BlockSpec(memory_space=pl.ANY)] + in_specs
        args = [prev] + args
        aliases = {0: 0}
    return pl.pallas_call(
        kern,
        grid=(b, h, n_q_tiles),
        in_specs=in_specs,
        out_specs=pl.BlockSpec((1, tq, LANES), lambda bi, hi, i: (bi, q_off + i, hi)),
        out_shape=jax.ShapeDtypeStruct((b, t, h * LANES), BF16),
        scratch_shapes=[
            pltpu.VMEM((2, tq, 1), F32),
            pltpu.VMEM((2, tq, 1), F32),
            pltpu.VMEM((2, tq, LANES), F32),
        ],
        input_output_aliases=aliases,
        compiler_params=_cparams("parallel", "parallel", "arbitrary"),
        name="diff_attn_lat" if prev is None else "diff_attn_ctx",
    )(*args)


def _ffn_kernel(x_ref, modl_ref, modc_ref, w1_ref, w3_ref, w2_ref, o_ref, *, n_lat, tm, tf):
    row0 = pl.program_id(1) * tm
    x = x_ref[0]
    h = _modulated(x, modl_ref, modc_ref, 3, row0, n_lat).astype(BF16)
    f_dim = w1_ref.shape[1]
    y = jnp.zeros((tm, x.shape[1]), F32)
    for c in range(f_dim // tf):
        sl = slice(c * tf, (c + 1) * tf)
        a = jnp.dot(h, w1_ref[:, sl], preferred_element_type=F32)
        g = jnp.dot(h, w3_ref[:, sl], preferred_element_type=F32)
        u = (a * _sigmoid(a) * g).astype(BF16)
        y = y + jnp.dot(u, w2_ref[sl, :], preferred_element_type=F32)
    gate = _gate_rows(modl_ref, modc_ref, 5, row0, tm, n_lat)
    o_ref[0] = x + gate * y


def _ffn(x, mod, w1, w3, w2, n_lat):
    b, t, d = x.shape
    tm = _pick_tile(t, (768, 256))
    tf = _pick_tile(w1.shape[1], (256, 128))
    return pl.pallas_call(
        functools.partial(_ffn_kernel, n_lat=n_lat, tm=tm, tf=tf),
        grid=(b, t // tm),
        in_specs=[
            pl.BlockSpec((1, tm, d), lambda bi, i: (bi, i, 0)),
            pl.BlockSpec((1, N_MOD, d), lambda bi, i: (bi, 0, 0)),
            pl.BlockSpec((1, N_MOD, d), lambda bi, i: (b, 0, 0)),
            _resident_spec(w1.shape),
            _resident_spec(w3.shape),
            _resident_spec(w2.shape),
        ],
        out_specs=pl.BlockSpec((1, tm, d), lambda bi, i: (bi, i, 0)),
        out_shape=jax.ShapeDtypeStruct((b, t, d), F32),
        input_output_aliases={0: 0},
        compiler_params=_cparams("parallel", "parallel"),
        name="ffn_swiglu",
    )(x, mod, mod, w1, w3, w2)


def _top2_combine(logits):
    tm = logits.shape[0]
    lane = lax.broadcasted_iota(jnp.int32, (tm, LANES), 1)
    neg = jnp.float32(-jnp.inf)
    lg = jnp.where(lane < N_EXPERTS, logits, neg)
    m1 = jnp.max(lg, axis=-1, keepdims=True)
    i1 = jnp.min(jnp.where(lg == m1, lane, LANES), axis=-1, keepdims=True)
    lg2 = jnp.where(lane == i1, neg, lg)
    m2 = jnp.max(lg2, axis=-1, keepdims=True)
    i2 = jnp.min(jnp.where(lg2 == m2, lane, LANES), axis=-1, keepdims=True)
    e = jnp.exp(m2 - m1)
    g1 = 1.0 / (1.0 + e)
    g2 = e / (1.0 + e)
    return jnp.where(lane == i1, g1, 0.0) + jnp.where(lane == i2, g2, 0.0)


def _moe_kernel(x_ref, modl_ref, modc_ref, wr_ref, w1_ref, w3_ref, w2_ref, o_ref,
                h_scr, comb_scr, acc_scr, *, n_lat, tm, tf):
    e = pl.program_id(2)
    f = pl.program_id(3)
    row0 = pl.program_id(1) * tm

    @pl.when((e == 0) & (f == 0))
    def _():
        hf = _modulated(x_ref[0], modl_ref, modc_ref, 3, row0, n_lat)
        logits = jnp.dot(hf, wr_ref[...], preferred_element_type=F32, precision=lax.Precision.HIGHEST)
        comb_scr[...] = _top2_combine(logits)
        h_scr[...] = hf.astype(BF16)
        acc_scr[...] = jnp.zeros_like(acc_scr)

    h = h_scr[...]
    lane = lax.broadcasted_iota(jnp.int32, (tm, LANES), 1)
    ce = jnp.sum(jnp.where(lane == e, comb_scr[...], 0.0), axis=-1, keepdims=True)
    f_blk = w1_ref.shape[2]
    y = jnp.zeros((tm, x_ref.shape[2]), F32)
    for c in range(f_blk // tf):
        sl = slice(c * tf, (c + 1) * tf)
        a = jnp.dot(h, w1_ref[0, :, sl], preferred_element_type=F32)
        g = jnp.dot(h, w3_ref[0, :, sl], preferred_element_type=F32)
        u = (a * _sigmoid(a) * g).astype(BF16)
        y = y + jnp.dot(u, w2_ref[0, sl, :], preferred_element_type=F32)
    acc_scr[...] += ce * y

    @pl.when((e == pl.num_programs(2) - 1) & (f == pl.num_programs(3) - 1))
    def _():
        gate = _gate_rows(modl_ref, modc_ref, 5, row0, tm, n_lat)
        o_ref[0] = x_ref[0] + gate * acc_scr[...]


def _moe(x, mod, w_router, w1, w3, w2, n_lat):
    b, t, d = x.shape
    n_exp, _, f_dim = w1.shape
    tm = _pick_tile(t, (768, 256))
    f_blk = _pick_tile(f_dim, (1792, 512, 256))
    tf = _pick_tile(f_blk, (256, 128))
    return pl.pallas_call(
        functools.partial(_moe_kernel, n_lat=n_lat, tm=tm, tf=tf),
        grid=(b, t // tm, n_exp, f_dim // f_blk),
        in_specs=[
            pl.BlockSpec((1, tm, d), lambda bi, i, e, f: (bi, i, 0)),
            pl.BlockSpec((1, N_MOD, d), lambda bi, i, e, f: (bi, 0, 0)),
            pl.BlockSpec((1, N_MOD, d), lambda bi, i, e, f: (b, 0, 0)),
            _const_spec((d, LANES)),
            pl.BlockSpec((1, d, f_blk), lambda bi, i, e, f: (e, 0, f)),
            pl.BlockSpec((1, d, f_blk), lambda bi, i, e, f: (e, 0, f)),
            pl.BlockSpec((1, f_blk, d), lambda bi, i, e, f: (e, f, 0)),
        ],
        out_specs=pl.BlockSpec((1, tm, d), lambda bi, i, e, f: (bi, i, 0)),
        out_shape=jax.ShapeDtypeStruct((b, t, d), F32),
        scratch_shapes=[
            pltpu.VMEM((tm, d), BF16),
            pltpu.VMEM((tm, LANES), F32),
            pltpu.VMEM((tm, d), F32),
        ],
        input_output_aliases={0: 0},
        compiler_params=_cparams("parallel", "parallel", "arbitrary", "arbitrary"),
        name="moe_dense",
    )(x, mod, mod, w_router, w1, w3, w2)


def _rot_cols(w):
    half = HEAD_DIM // 2
    return jnp.concatenate([-w[..., half:], w[..., :half]], axis=-1)


def _with_rot(w_heads):
    d, n, _ = w_heads.shape
    return jnp.concatenate([w_heads, _rot_cols(w_heads)], axis=-1).reshape(d, n * LANES)


def _gain_lanes(gain):
    half = HEAD_DIM // 2
    perm = jnp.concatenate([gain[half:], gain[:half]])
    return jnp.concatenate([gain, perm]).reshape(1, LANES).astype(F32)


def _rope_table(n_lat, n_ctx):
    rows = n_lat // GRID_W
    row = jnp.repeat(jnp.arange(rows, dtype=F32), GRID_W)
    col = jnp.tile(jnp.arange(GRID_W, dtype=F32), rows)
    quarter = HEAD_DIM // 4
    inv_freq = ROPE_BASE ** (-jnp.arange(quarter, dtype=F32) / quarter)
    ang = jnp.concatenate([row[:, None] * inv_freq, col[:, None] * inv_freq], axis=-1)
    cos, sin = jnp.cos(ang), jnp.sin(ang)
    lat = jnp.concatenate([cos, cos, sin, sin], axis=-1)
    ctx = jnp.concatenate([jnp.ones((n_ctx, HEAD_DIM), F32), jnp.zeros((n_ctx, HEAD_DIM), F32)], axis=-1)
    return jnp.concatenate([lat, ctx], axis=0)


def _chan_tables():
    idx = jnp.arange(FNET_GROUP_DIM, dtype=jnp.int32)
    ang = ((idx[:, None] * idx[None, :]) % FNET_GROUP_DIM).astype(F32) * (2.0 * math.pi / FNET_GROUP_DIM)
    return jnp.cos(ang).astype(BF16), jnp.sin(ang).astype(BF16)


def kernel(x, c, ctx, c_ctx, w_mod, b_mod, w_in_even, w_out_even, ret_decay_fwd, ret_decay_bwd,
           ffn_w1, ffn_w3, ffn_w2, w_in_odd, w_out_odd, q_norm_gain, k_norm_gain,
           lambda_q1, lambda_k1, lambda_q2, lambda_k2, subln_gain, w_router, moe_w1, moe_w3, moe_w2):
    b, n_lat, d = x.shape
    n_ctx = ctx.shape[1]
    depth = w_mod.shape[0]
    assert d == D_MODEL and b < MOD_ROWS and n_lat % n_ctx == 0 and n_ctx % RET_CHUNK == 0

    stream = jnp.concatenate([x, ctx], axis=1)
    cond = jnp.zeros((MOD_ROWS, d), F32).at[:b].set(c).at[b].set(c_ctx)
    mods = _adaln(cond, w_mod, b_mod).reshape(depth, MOD_ROWS, N_MOD, d)
    tab = _rope_table(n_lat, n_ctx)
    chan_tabs = _chan_tables()

    for layer in range(depth):
        i = layer // 2
        mod = mods[layer]
        if layer % 2 == 0:
            w = w_in_even[i]
            hq = RET_HEADS * HEAD_DIM
            wq = w[:, FNET_WIDTH:FNET_WIDTH + hq].reshape(d, RET_HEADS, HEAD_DIM)
            wk = w[:, FNET_WIDTH + hq:FNET_WIDTH + 2 * hq].reshape(d, RET_HEADS, HEAD_DIM)
            w_in = jnp.concatenate(
                [w[:, :FNET_WIDTH], _with_rot(wq), _with_rot(wk), w[:, FNET_WIDTH + 2 * hq:]], axis=1).astype(BF16)
            hin = _inproj(stream, mod, w_in, tab, (), n_lat, w_in.shape[1], True)
            four = jnp.concatenate(
                [_fourier_seq(hin, 0, n_lat, chan_tabs), _fourier_seq(hin, n_lat, n_ctx, chan_tabs)], axis=1)
            dec_f = jnp.broadcast_to(ret_decay_fwd[i].astype(F32)[:, None, None], (RET_HEADS, 1, LANES))
            dec_b = jnp.broadcast_to(ret_decay_bwd[i].astype(F32)[:, None, None], (RET_HEADS, 1, LANES))
            ret = _retention(hin, dec_f, dec_b, n_lat)
            w_out = w_out_even[i].astype(BF16)
            stream = _merge(stream, mod, [four, ret], [w_out[:FNET_WIDTH], w_out[FNET_WIDTH:]], n_lat)
            stream = _ffn(stream, mod, ffn_w1[i].astype(BF16), ffn_w3[i].astype(BF16),
                          ffn_w2[i].astype(BF16), n_lat)
        else:
            w = w_in_odd[i]
            n_sub = 2 * DIFF_HEADS
            wq = w[:, :n_sub * HEAD_DIM].reshape(d, n_sub, HEAD_DIM)
            wk = w[:, n_sub * HEAD_DIM:2 * n_sub * HEAD_DIM].reshape(d, n_sub, HEAD_DIM)
            w_in = jnp.concatenate([_with_rot(wq), _with_rot(wk), w[:, 2 * n_sub * HEAD_DIM:]], axis=1).astype(BF16)
            gains = (_gain_lanes(q_norm_gain[i]), _gain_lanes(k_norm_gain[i]))
            qkv = _inproj(stream, mod, w_in, tab, gains, n_lat, 3 * DIFF_HEADS * LANES, False)
            lam_init = 0.8 - 0.6 * float(np.exp(-0.3 * layer))
            lam_vecs = jnp.zeros((4, LANES), F32).at[:, :HEAD_DIM].set(
                jnp.stack([lambda_q1[i], lambda_k1[i], lambda_q2[i], lambda_k2[i]]).astype(F32))
            sub_gain = subln_gain[i].astype(F32).reshape(1, LANES)
            att = _attention(qkv, lam_vecs, sub_gain, lam_init, n_lat)
            att = _attention(qkv, lam_vecs, sub_gain, lam_init, n_lat, prev=att)
            stream = _merge(stream, mod, [att], [w_out_odd[i].astype(BF16)], n_lat)
            wr = jnp.zeros((d, LANES), F32).at[:, :N_EXPERTS].set(w_router[i].astype(F32))
            stream = _moe(stream, mod, wr, moe_w1[i].astype(BF16), moe_w3[i].astype(BF16),
                          moe_w2[i].astype(BF16), n_lat)
    return stream[:, :n_lat]
```

```python
import functools
import math

import jax
import jax.numpy as jnp
import numpy as np
from jax import lax
from jax.experimental import pallas as pl
from jax.experimental.pallas import tpu as pltpu

F32 = jnp.float32
BF16 = jnp.bfloat16

D_MODEL = 1024
GRID_W = 64
HEAD_DIM = 64
LANES = 128
FNET_GROUPS = 4
FNET_GROUP_DIM = 128
FNET_WIDTH = FNET_GROUPS * FNET_GROUP_DIM
RET_HEADS = 4
RET_CHUNK = 128
DIFF_HEADS = 8
N_EXPERTS = 8
N_MOD = 6
ROPE_BASE = 10000.0
EPS = 1e-6
LOG2_E = math.log2(math.e)
VMEM_LIMIT_BYTES = 56 * 1024 * 1024
MOD_ROWS = 8


def _cparams(*sem):
    return pltpu.CompilerParams(dimension_semantics=sem, vmem_limit_bytes=VMEM_LIMIT_BYTES)


def _pick_tile(n, candidates):
    for c in candidates:
        if n % c == 0:
            return c
    raise ValueError(f"no tile in {candidates} divides {n}")


def _const_spec(shape):
    nd = len(shape)
    return pl.BlockSpec(shape, lambda *_: (0,) * nd)


def _resident_spec(shape):
    nd = len(shape)
    return pl.BlockSpec(shape, lambda *_: (0,) * nd, pipeline_mode=pl.Buffered(1))


def _sigmoid(x):
    return 1.0 / (1.0 + jnp.exp(-x))


def _modulated(x, modl_ref, modc_ref, k_shift, row0, n_lat):
    tm = x.shape[0]
    ms = jnp.mean(x * x, axis=-1, keepdims=True)
    xn = x * lax.rsqrt(ms + EPS)
    rows = row0 + lax.broadcasted_iota(jnp.int32, (tm, 1), 0)
    is_ctx = rows >= n_lat
    shift = jnp.where(is_ctx, modc_ref[0, k_shift:k_shift + 1, :], modl_ref[0, k_shift:k_shift + 1, :])
    scale = jnp.where(is_ctx, modc_ref[0, k_shift + 1:k_shift + 2, :], modl_ref[0, k_shift + 1:k_shift + 2, :])
    return xn * (1.0 + scale) + shift


def _gate_rows(modl_ref, modc_ref, k_gate, row0, tm, n_lat):
    rows = row0 + lax.broadcasted_iota(jnp.int32, (tm, 1), 0)
    return jnp.where(rows >= n_lat, modc_ref[0, k_gate:k_gate + 1, :], modl_ref[0, k_gate:k_gate + 1, :])


def _adaln_kernel(c_ref, w_ref, b_ref, o_ref):
    c = c_ref[...]
    s = c * _sigmoid(c)
    o_ref[0] = jnp.dot(s, w_ref[0], preferred_element_type=F32, precision=lax.Precision.HIGHEST) + b_ref[0]


def _adaln(cond_rows, w_mod, b_mod):
    depth, d, n = w_mod.shape
    tn = _pick_tile(n, (1536, 1024, 512))
    return pl.pallas_call(
        _adaln_kernel,
        grid=(depth, n // tn),
        in_specs=[
            _const_spec((MOD_ROWS, d)),
            pl.BlockSpec((1, d, tn), lambda l, j: (l, 0, j)),
            pl.BlockSpec((1, 1, tn), lambda l, j: (l, 0, j)),
        ],
        out_specs=pl.BlockSpec((1, MOD_ROWS, tn), lambda l, j: (l, 0, j)),
        out_shape=jax.ShapeDtypeStruct((depth, MOD_ROWS, n), F32),
        compiler_params=_cparams("parallel", "parallel"),
        name="adaln",
    )(cond_rows, w_mod, b_mod.reshape(depth, 1, n))


def _rope_dup(blk, tab):
    t = blk * tab
    return t + pltpu.roll(t, HEAD_DIM, axis=1)


def _inproj_even_kernel(x_ref, modl_ref, modc_ref, w_ref, tab_ref, o_ref, *, n_lat, tm):
    row0 = pl.program_id(1) * tm
    h = _modulated(x_ref[0], modl_ref, modc_ref, 0, row0, n_lat).astype(BF16)
    tab = tab_ref[...]
    lane = lax.broadcasted_iota(jnp.int32, (tm, LANES), 1)
    n_out = o_ref.shape[2]
    for c in range(n_out // 256):
        y = jnp.dot(h, w_ref[:, c * 256:(c + 1) * 256], preferred_element_type=F32)
        for half in range(2):
            blk = y[:, half * LANES:(half + 1) * LANES]
            col = c * 2 + half
            if 4 <= col < 8:
                blk = _rope_dup(blk, tab)
            elif 8 <= col < 12:
                blk = jnp.where(lane < HEAD_DIM, _rope_dup(blk, tab) * (HEAD_DIM ** -0.5), 0.0)
            o_ref[0, :, col * LANES:(col + 1) * LANES] = blk.astype(BF16)


def _inproj_odd_kernel(x_ref, modl_ref, modc_ref, w_ref, tab_ref, gq_ref, gk_ref, o_ref, *, n_lat, tm):
    row0 = pl.program_id(1) * tm
    h = _modulated(x_ref[0], modl_ref, modc_ref, 0, row0, n_lat).astype(BF16)
    tab = tab_ref[...]
    lane = lax.broadcasted_iota(jnp.int32, (tm, LANES), 1)
    n_qk = 2 * DIFF_HEADS
    for c in range(n_qk):
        y = jnp.dot(h, w_ref[:, c * 256:(c + 1) * 256], preferred_element_type=F32)
        gain = gq_ref[...] * (HEAD_DIM ** -0.5 * LOG2_E) if c < DIFF_HEADS else gk_ref[...]
        subs = []
        for half in range(2):
            blk = y[:, half * LANES:(half + 1) * LANES]
            ms = jnp.sum(blk * blk, axis=-1, keepdims=True) * (1.0 / LANES)
            subs.append(_rope_dup(blk * lax.rsqrt(ms + EPS) * gain, tab))
        o_ref[0, :, c * LANES:(c + 1) * LANES] = jnp.where(lane < HEAD_DIM, subs[0], subs[1]).astype(BF16)
    for c in range(DIFF_HEADS // 2):
        col0 = n_qk * 256 + c * 256
        y = jnp.dot(h, w_ref[:, col0:col0 + 256], preferred_element_type=F32)
        out0 = n_qk * LANES + c * 256
        o_ref[0, :, out0:out0 + 256] = y.astype(BF16)


def _inproj(x, mod, w, tab, gains, n_lat, n_out, even):
    b, t, d = x.shape
    tm = _pick_tile(t, (768, 256))
    kern = _inproj_even_kernel if even else _inproj_odd_kernel
    in_specs = [
        pl.BlockSpec((1, tm, d), lambda bi, i: (bi, i, 0)),
        pl.BlockSpec((1, N_MOD, d), lambda bi, i: (bi, 0, 0)),
        pl.BlockSpec((1, N_MOD, d), lambda bi, i: (b, 0, 0)),
        _resident_spec(w.shape),
        pl.BlockSpec((tm, LANES), lambda bi, i: (i, 0)),
    ] + [_const_spec((1, LANES))] * len(gains)
    return pl.pallas_call(
        functools.partial(kern, n_lat=n_lat, tm=tm),
        grid=(b, t // tm),
        in_specs=in_specs,
        out_specs=pl.BlockSpec((1, tm, n_out), lambda bi, i: (bi, i, 0)),
        out_shape=jax.ShapeDtypeStruct((b, t, n_out), BF16),
        compiler_params=_cparams("parallel", "parallel"),
        name="inproj_even" if even else "inproj_odd",
    )(x, mod, mod, w, tab, *gains)


def _dft_kernel(c_ref, s_ref, u_ref, cc_ref, sc_ref, o_ref, acc_c, acc_s, *, nb, scale):
    j = pl.program_id(1)

    @pl.when(j == 0)
    def _():
        acc_c[...] = jnp.zeros_like(acc_c)
        acc_s[...] = jnp.zeros_like(acc_s)

    cm = c_ref[...]
    sm = s_ref[...]
    for bi in range(nb):
        u = u_ref[bi]
        acc_c[bi] += jnp.dot(cm, u, preferred_element_type=F32)
        acc_s[bi] += jnp.dot(sm, u, preferred_element_type=F32)

    @pl.when(j == pl.num_programs(1) - 1)
    def _():
        cc = cc_ref[...]
        sc = sc_ref[...]
        for bi in range(nb):
            for g in range(FNET_GROUPS):
                sl = slice(g * FNET_GROUP_DIM, (g + 1) * FNET_GROUP_DIM)
                a = acc_c[bi, :, sl].astype(BF16)
                bm = acc_s[bi, :, sl].astype(BF16)
                y = jnp.dot(a, cc, preferred_element_type=F32) - jnp.dot(bm, sc, preferred_element_type=F32)
                o_ref[bi, :, sl] = (y * scale).astype(BF16)


def _angle_tables(n_rows, n_cols, stride, period):
    k = jnp.arange(n_rows, dtype=jnp.int32)[:, None]
    n = jnp.arange(n_cols, dtype=jnp.int32)[None, :]
    ang = ((k * n * stride) % period).astype(F32) * (2.0 * math.pi / period)
    return jnp.cos(ang), jnp.sin(ang)


def _dft_tables(n):
    if n <= GRID_W:
        c, s = _angle_tables(n, n, 1, n)
        return c.astype(BF16), s.astype(BF16)
    ca, sa = _angle_tables(n, n // GRID_W, GRID_W, n)
    cb, sb = _angle_tables(n, GRID_W, 1, n)
    ca, sa, cb, sb = ca[:, :, None], sa[:, :, None], cb[:, None, :], sb[:, None, :]
    cm = (ca * cb - sa * sb).reshape(n, n)
    sm = (sa * cb + ca * sb).reshape(n, n)
    return cm.astype(BF16), sm.astype(BF16)


def _fourier_seq(hin, row0, n, chan_tabs):
    b = hin.shape[0]
    cm, sm = _dft_tables(n)
    cc, sc = chan_tabs
    tk = _pick_tile(n, (1024, 512, 256))
    tn = tk
    off = row0 // tn
    return pl.pallas_call(
        functools.partial(_dft_kernel, nb=b, scale=1.0 / math.sqrt(n * FNET_GROUP_DIM)),
        grid=(n // tk, n // tn),
        in_specs=[
            pl.BlockSpec((tk, tn), lambda i, j: (i, j)),
            pl.BlockSpec((tk, tn), lambda i, j: (i, j)),
            pl.BlockSpec((b, tn, FNET_WIDTH), lambda i, j: (0, off + j, 0)),
            _const_spec((FNET_GROUP_DIM, FNET_GROUP_DIM)),
            _const_spec((FNET_GROUP_DIM, FNET_GROUP_DIM)),
        ],
        out_specs=pl.BlockSpec((b, tk, FNET_WIDTH), lambda i, j: (0, i, 0)),
        out_shape=jax.ShapeDtypeStruct((b, n, FNET_WIDTH), BF16),
        scratch_shapes=[pltpu.VMEM((b, tk, FNET_WIDTH), F32), pltpu.VMEM((b, tk, FNET_WIDTH), F32)],
        compiler_params=_cparams("parallel", "arbitrary"),
        name=f"fourier_seq_{n}",
    )(cm, sm, hin, cc, sc)


def _log_sigmoid(x):
    return jnp.minimum(x, 0.0) - jnp.log(1.0 + jnp.exp(-jnp.abs(x)))


def _retention_kernel(q_ref, k_ref, v_ref, g_ref, decf_ref, decb_ref, o_ref,
                      of_scr, ob_scr, st_scr, dec_scr, *, n_lat, n_ctx):
    c = RET_CHUNK
    pos_i = lax.broadcasted_iota(jnp.int32, (c, c), 0).astype(F32)
    pos_j = lax.broadcasted_iota(jnp.int32, (c, c), 1).astype(F32)
    lg_f = _log_sigmoid(decf_ref[0])
    lg_b = _log_sigmoid(decb_ref[0])
    rel = pos_i - pos_j
    mask_f = rel >= 0.0
    mask_b = rel < 0.0
    dec_scr[0] = jnp.where(mask_f, jnp.exp(lg_f * jnp.where(mask_f, rel, 0.0)), 0.0)
    dec_scr[1] = jnp.where(mask_b, jnp.exp(lg_b * jnp.where(mask_b, -rel, 0.0)), 0.0)
    dec_scr[2] = jnp.exp(lg_f * (pos_i + 1.0))
    dec_scr[3] = jnp.exp(lg_b * (c - pos_i))
    dec_scr[4] = jnp.exp(lg_f * (c - 1.0 - pos_i))
    dec_scr[5] = jnp.exp(lg_b * pos_i)
    dec_scr[6] = jnp.exp(lg_f * (c + 0.0 * pos_i))
    dec_scr[7] = jnp.exp(lg_b * (c + 0.0 * pos_i))
    st_scr[...] = jnp.zeros_like(st_scr)

    def one_dir(d, start, out_scr):
        qc = q_ref[0, pl.ds(start, c), :]
        kc = k_ref[0, pl.ds(start, c), :]
        vc = v_ref[0, pl.ds(start, c), :]
        state = st_scr[d]
        scores = lax.dot_general(qc, kc, (((1,), (1,)), ((), ())), preferred_element_type=F32)
        intra = jnp.dot((scores * dec_scr[d]).astype(BF16), vc, preferred_element_type=F32)
        inter = jnp.dot(qc, state.astype(BF16), preferred_element_type=F32)
        out_scr[pl.ds(start, c), :] = intra + dec_scr[2 + d] * inter
        vd = (vc.astype(F32) * dec_scr[4 + d]).astype(BF16)
        kv = lax.dot_general(kc, vd, (((0,), (0,)), ((), ())), preferred_element_type=F32)
        st_scr[d] = dec_scr[6 + d] * state + kv

    n_ctx_chunks = n_ctx // c
    for ci in range(n_ctx_chunks):
        one_dir(0, n_lat + ci * c, of_scr)
        one_dir(1, n_lat + (n_ctx_chunks - 1 - ci) * c, ob_scr)

    n_lat_chunks = n_lat // c

    def body(i, carry):
        one_dir(0, pl.multiple_of(i * c, c), of_scr)
        one_dir(1, pl.multiple_of((n_lat_chunks - 1 - i) * c, c), ob_scr)
        return carry

    lax.fori_loop(0, n_lat_chunks, body, 0)

    def finish(i, carry):
        r0 = pl.multiple_of(i * c, c)
        o = of_scr[pl.ds(r0, c), :] + ob_scr[pl.ds(r0, c), :]
        o = o * lax.rsqrt(jnp.mean(o * o, axis=-1, keepdims=True) + EPS)
        g = g_ref[0, pl.ds(r0, c), :].astype(F32)
        o_ref[0, pl.ds(r0, c), :] = (o * (g * _sigmoid(g))).astype(BF16)
        return carry

    lax.fori_loop(0, (n_lat + n_ctx) // c, finish, 0)


def _retention(hin, dec_f, dec_b, n_lat):
    b, t, _ = hin.shape
    h = RET_HEADS
    seq = lambda col0: pl.BlockSpec((1, t, LANES), lambda bi, hi: (bi, 0, col0 + hi))
    dec = pl.BlockSpec((1, 1, LANES), lambda bi, hi: (hi, 0, 0))
    return pl.pallas_call(
        functools.partial(_retention_kernel, n_lat=n_lat, n_ctx=t - n_lat),
        grid=(b, h),
        in_specs=[seq(4), seq(8), seq(12), seq(16), dec, dec],
        out_specs=pl.BlockSpec((1, t, LANES), lambda bi, hi: (bi, 0, hi)),
        out_shape=jax.ShapeDtypeStruct((b, t, h * LANES), BF16),
        scratch_shapes=[
            pltpu.VMEM((t, LANES), F32),
            pltpu.VMEM((t, LANES), F32),
            pltpu.VMEM((2, LANES, LANES), F32),
            pltpu.VMEM((8, RET_CHUNK, LANES), F32),
        ],
        compiler_params=_cparams("parallel", "parallel"),
        name="retention",
    )(hin, hin, hin, hin, dec_f, dec_b)


def _merge_kernel(*refs, n_parts, n_lat, tm):
    x_ref, modl_ref, modc_ref = refs[:3]
    a_refs = refs[3:3 + n_parts]
    w_refs = refs[3 + n_parts:3 + 2 * n_parts]
    o_ref = refs[3 + 2 * n_parts]
    row0 = pl.program_id(1) * tm
    y = jnp.dot(a_refs[0][0], w_refs[0][...], preferred_element_type=F32)
    for p in range(1, n_parts):
        y = y + jnp.dot(a_refs[p][0], w_refs[p][...], preferred_element_type=F32)
    gate = _gate_rows(modl_ref, modc_ref, 2, row0, tm, n_lat)
    o_ref[0] = x_ref[0] + gate * y


def _merge(x, mod, parts, weights, n_lat):
    b, t, d = x.shape
    tm = _pick_tile(t, (768, 256))
    n_parts = len(parts)
    in_specs = [
        pl.BlockSpec((1, tm, d), lambda bi, i: (bi, i, 0)),
        pl.BlockSpec((1, N_MOD, d), lambda bi, i: (bi, 0, 0)),
        pl.BlockSpec((1, N_MOD, d), lambda bi, i: (b, 0, 0)),
    ]
    in_specs += [pl.BlockSpec((1, tm, p.shape[2]), lambda bi, i: (bi, i, 0)) for p in parts]
    in_specs += [_resident_spec(w.shape) for w in weights]
    return pl.pallas_call(
        functools.partial(_merge_kernel, n_parts=n_parts, n_lat=n_lat, tm=tm),
        grid=(b, t // tm),
        in_specs=in_specs,
        out_specs=pl.BlockSpec((1, tm, d), lambda bi, i: (bi, i, 0)),
        out_shape=jax.ShapeDtypeStruct((b, t, d), F32),
        input_output_aliases={0: 0},
        compiler_params=_cparams("parallel", "parallel"),
        name=f"merge_{n_parts}",
    )(x, mod, mod, *parts, *weights)


def _attn_kernel(prev_ref, q_ref, k_ref, v_ref, lam_ref, sg_ref, o_ref, *, tk, lam_init):
    del prev_ref
    tq = q_ref.shape[1]
    n_keys = k_ref.shape[1]
    q = q_ref[0]
    lane = lax.broadcasted_iota(jnp.int32, (tq, LANES), 1)
    zero = jnp.zeros_like(q)
    q_sub = (jnp.where(lane < HEAD_DIM, q, zero), jnp.where(lane >= HEAD_DIM, q, zero))
    ones = jnp.ones((tk, LANES), BF16)
    m = [None, None]
    acc = [None, None]
    for j in range(n_keys // tk):
        kc = k_ref[0, j * tk:(j + 1) * tk, :]
        v1 = jnp.concatenate([v_ref[0, j * tk:(j + 1) * tk, :], ones], axis=1)
        for sub in range(2):
            s = lax.dot_general(q_sub[sub], kc, (((1,), (1,)), ((), ())), preferred_element_type=F32)
            mx = jnp.max(s, axis=-1, keepdims=True)
            if j == 0:
                m[sub] = mx
                p = jnp.exp2(s - mx).astype(BF16)
                acc[sub] = jnp.dot(p, v1, preferred_element_type=F32)
            else:
                m_new = jnp.maximum(m[sub], mx)
                alpha = jnp.exp2(m[sub] - m_new)
                p = jnp.exp2(s - m_new).astype(BF16)
                acc[sub] = alpha * acc[sub] + jnp.dot(p, v1, preferred_element_type=F32)
                m[sub] = m_new

    lv = lam_ref[...]
    s1 = jnp.sum(lv[0:1] * lv[1:2], axis=-1, keepdims=True)
    s2 = jnp.sum(lv[2:3] * lv[3:4], axis=-1, keepdims=True)
    lam = jnp.exp(s1) - jnp.exp(s2) + lam_init
    o = acc[0][:, :LANES] / acc[0][:, LANES:] - lam * (acc[1][:, :LANES] / acc[1][:, LANES:])
    o = o * lax.rsqrt(jnp.mean(o * o, axis=-1, keepdims=True) + EPS)
    o_ref[0] = (o * sg_ref[...] * (1.0 - lam_init)).astype(BF16)


def _attention(out, qkv, lam_vecs, sub_gain, lam_init, n_lat, ctx_queries):
    b, t, _ = qkv.shape
    h = DIFF_HEADS
    n_ctx = t - n_lat
    if ctx_queries:
        tq, n_q_tiles, q_off = n_ctx, 1, n_lat // n_ctx
        keys, k_off = n_ctx, n_lat // n_ctx
        tk = n_ctx
    else:
        tq = _pick_tile(n_lat, (512, 256))
        n_q_tiles, q_off = n_lat // tq, 0
        keys, k_off = t, 0
        tk = _pick_tile(t, (768, 256))
    return pl.pallas_call(
        functools.partial(_attn_kernel, tk=tk, lam_init=lam_init),
        grid=(b, h, n_q_tiles),
        in_specs=[
            pl.BlockSpec(memory_space=pl.ANY),
            pl.BlockSpec((1, tq, LANES), lambda bi, hi, i: (bi, q_off + i, hi)),
            pl.BlockSpec((1, keys, LANES), lambda bi, hi, i: (bi, k_off, h + hi)),
            pl.BlockSpec((1, keys, LANES), lambda bi, hi, i: (bi, k_off, 2 * h + hi)),
            _const_spec((4, LANES)),
            _const_spec((1, LANES)),
        ],
        out_specs=pl.BlockSpec((1, tq, LANES), lambda bi, hi, i: (bi, q_off + i, hi)),
        out_shape=jax.ShapeDtypeStruct((b, t, h * LANES), BF16),
        input_output_aliases={0: 0},
        compiler_params=_cparams("parallel", "parallel", "arbitrary"),
        name="diff_attn_ctx" if ctx_queries else "diff_attn_lat",
    )(out, qkv, qkv, qkv, lam_vecs, sub_gain)


def _ffn_kernel(x_ref, modl_ref, modc_ref, w1_ref, w3_ref, w2_ref, o_ref, *, n_lat, tm, tf):
    row0 = pl.program_id(1) * tm
    x = x_ref[0]
    h = _modulated(x, modl_ref, modc_ref, 3, row0, n_lat).astype(BF16)
    f_dim = w1_ref.shape[1]
    y = jnp.zeros((tm, x.shape[1]), F32)
    for c in range(f_dim // tf):
        sl = slice(c * tf, (c + 1) * tf)
        a = jnp.dot(h, w1_ref[:, sl], preferred_element_type=F32)
        g = jnp.dot(h, w3_ref[:, sl], preferred_element_type=F32)
        u = (a * _sigmoid(a) * g).astype(BF16)
        y = y + jnp.dot(u, w2_ref[sl, :], preferred_element_type=F32)
    gate = _gate_rows(modl_ref, modc_ref, 5, row0, tm, n_lat)
    o_ref[0] = x + gate * y


def _ffn(x, mod, w1, w3, w2, n_lat):
    b, t, d = x.shape
    tm = _pick_tile(t, (768, 256))
    tf = _pick_tile(w1.shape[1], (256, 128))
    return pl.pallas_call(
        functools.partial(_ffn_kernel, n_lat=n_lat, tm=tm, tf=tf),
        grid=(b, t // tm),
        in_specs=[
            pl.BlockSpec((1, tm, d), lambda bi, i: (bi, i, 0)),
            pl.BlockSpec((1, N_MOD, d), lambda bi, i: (bi, 0, 0)),
            pl.BlockSpec((1, N_MOD, d), lambda bi, i: (b, 0, 0)),
            _resident_spec(w1.shape),
            _resident_spec(w3.shape),
            _resident_spec(w2.shape),
        ],
        out_specs=pl.BlockSpec((1, tm, d), lambda bi, i: (bi, i, 0)),
        out_shape=jax.ShapeDtypeStruct((b, t, d), F32),
        input_output_aliases={0: 0},
        compiler_params=_cparams("parallel", "parallel"),
        name="ffn_swiglu",
    )(x, mod, mod, w1, w3, w2)


ROUTE_E1, ROUTE_E2, ROUTE_G1, ROUTE_G2 = 0, 1, 2, 3
SLABS = D_MODEL // LANES


def _top2_route(logits):
    tm = logits.shape[0]
    lane = lax.broadcasted_iota(jnp.int32, (tm, LANES), 1)
    neg = jnp.float32(-jnp.inf)
    lg = jnp.where(lane < N_EXPERTS, logits, neg)
    m1 = jnp.max(lg, axis=-1, keepdims=True)
    i1 = jnp.min(jnp.where(lg == m1, lane, LANES), axis=-1, keepdims=True)
    lg2 = jnp.where(lane == i1, neg, lg)
    m2 = jnp.max(lg2, axis=-1, keepdims=True)
    i2 = jnp.min(jnp.where(lg2 == m2, lane, LANES), axis=-1, keepdims=True)
    e = jnp.exp(m2 - m1)
    g1 = 1.0 / (1.0 + e)
    g2 = e / (1.0 + e)
    rec = jnp.where(lane == ROUTE_E1, i1.astype(F32), 0.0) + jnp.where(lane == ROUTE_E2, i2.astype(F32), 0.0)
    return rec + jnp.where(lane == ROUTE_G1, g1, 0.0) + jnp.where(lane == ROUTE_G2, g2, 0.0)


def _slab_store(ref, val, n_rows):
    for s in range(SLABS):
        ref[pl.ds(s, n_rows, stride=SLABS), :] = val[:, s * LANES:(s + 1) * LANES]


def _slab_load(ref, n_rows):
    return jnp.concatenate([ref[pl.ds(s, n_rows, stride=SLABS), :] for s in range(SLABS)], axis=1)


def _route_kernel(x_ref, modl_ref, modc_ref, wr_ref, h_ref, rec_ref, *, n_lat, tm):
    row0 = pl.program_id(1) * tm
    hf = _modulated(x_ref[0], modl_ref, modc_ref, 3, row0, n_lat)
    logits = jnp.dot(hf, wr_ref[...], preferred_element_type=F32, precision=lax.Precision.HIGHEST)
    rec_ref[0] = _top2_route(logits)
    _slab_store(h_ref, hf, tm)


def _moe_route(x, mod, w_router, n_lat):
    b, t, d = x.shape
    tm = _pick_tile(t, (768, 256))
    tiles = t // tm
    return pl.pallas_call(
        functools.partial(_route_kernel, n_lat=n_lat, tm=tm),
        grid=(b, tiles),
        in_specs=[
            pl.BlockSpec((1, tm, d), lambda bi, i: (bi, i, 0)),
            pl.BlockSpec((1, N_MOD, d), lambda bi, i: (bi, 0, 0)),
            pl.BlockSpec((1, N_MOD, d), lambda bi, i: (b, 0, 0)),
            _const_spec((d, LANES)),
        ],
        out_specs=[
            pl.BlockSpec((tm * SLABS, LANES), lambda bi, i: (bi * tiles + i, 0)),
            pl.BlockSpec((1, tm, LANES), lambda bi, i: (bi, i, 0)),
        ],
        out_shape=[
            jax.ShapeDtypeStruct((b * t * SLABS, LANES), F32),
            jax.ShapeDtypeStruct((b, t, LANES), F32),
        ],
        compiler_params=_cparams("parallel", "parallel"),
        name="moe_route",
    )(x, mod, mod, w_router)


def _row_copy(src_ref, src_row, dst_ref, dst_row, sem):
    return pltpu.make_async_copy(
        src_ref.at[pl.ds(pl.multiple_of(src_row * SLABS, SLABS), SLABS), :],
        dst_ref.at[pl.ds(pl.multiple_of(dst_row * SLABS, SLABS), SLABS), :],
        sem)


def _dispatch_kernel(dest_ref, h_ref, init_ref, xs_ref, sem, *, tm):
    del init_ref
    tok0 = pl.program_id(0) * tm

    def start(r, carry):
        for slot in range(2):
            _row_copy(h_ref, tok0 + r, xs_ref, dest_ref[0, 0, 2 * r + slot], sem).start()
        return carry

    def wait(r, carry):
        for slot in range(2):
            _row_copy(h_ref, tok0 + r, xs_ref, dest_ref[0, 0, 2 * r + slot], sem).wait()
        return carry

    lax.fori_loop(0, tm, start, 0)
    lax.fori_loop(0, tm, wait, 0)


def _moe_dispatch(h_slabs, dest, n_sorted_rows):
    n_tok = h_slabs.shape[0] // SLABS
    tm = _pick_tile(n_tok, (512, 256))
    steps = n_tok // tm
    return pl.pallas_call(
        functools.partial(_dispatch_kernel, tm=tm),
        grid=(steps,),
        in_specs=[
            pl.BlockSpec((1, 1, 2 * tm), lambda i: (i, 0, 0), memory_space=pltpu.SMEM),
            pl.BlockSpec(memory_space=pl.ANY),
            pl.BlockSpec(memory_space=pl.ANY),
        ],
        out_specs=pl.BlockSpec(memory_space=pl.ANY),
        out_shape=jax.ShapeDtypeStruct((n_sorted_rows * SLABS, LANES), F32),
        scratch_shapes=[pltpu.SemaphoreType.DMA(())],
        input_output_aliases={2: 0},
        compiler_params=_cparams("arbitrary"),
        name="moe_dispatch",
    )(dest.reshape(steps, 1, 2 * tm), h_slabs, jnp.zeros((n_sorted_rows * SLABS, LANES), F32))


def _experts_kernel(te_ref, used_ref, x_ref, w1_ref, w3_ref, w2_ref, o_ref, h_scr, acc_scr, *, tm, tf):
    del te_ref
    i = pl.program_id(0)
    f = pl.program_id(1)
    valid = i < used_ref[0]

    @pl.when(f == 0)
    def _():
        acc_scr[...] = jnp.zeros_like(acc_scr)

    @pl.when(valid & (f == 0))
    def _():
        h_scr[...] = _slab_load(x_ref, tm).astype(BF16)

    @pl.when(valid)
    def _():
        h = h_scr[...]
        f_blk = w1_ref.shape[2]
        y = jnp.zeros(acc_scr.shape, F32)
        for c in range(f_blk // tf):
            sl = slice(c * tf, (c + 1) * tf)
            a = jnp.dot(h, w1_ref[0, :, sl], preferred_element_type=F32)
            g = jnp.dot(h, w3_ref[0, :, sl], preferred_element_type=F32)
            u = (a * _sigmoid(a) * g).astype(BF16)
            y = y + jnp.dot(u, w2_ref[0, sl, :], preferred_element_type=F32)
        acc_scr[...] += y

    @pl.when(f == pl.num_programs(1) - 1)
    def _():
        _slab_store(o_ref, acc_scr[...], tm)


def _moe_experts(xs, tile_expert, n_used, w1, w3, w2, tm):
    n_exp, d, f_dim = w1.shape
    n_tiles = xs.shape[0] // (tm * SLABS)
    f_blk = _pick_tile(f_dim, (1792, 512, 256))
    tf = _pick_tile(f_blk, (256, 128))
    n_f = f_dim // f_blk

    def f_eff(i, f, used):
        return jnp.where(i < used[0], f, n_f - 1)

    grid_spec = pltpu.PrefetchScalarGridSpec(
        num_scalar_prefetch=2,
        grid=(n_tiles, n_f),
        in_specs=[
            pl.BlockSpec((tm * SLABS, LANES), lambda i, f, te, used: (jnp.minimum(i, used[0] - 1), 0)),
            pl.BlockSpec((1, d, f_blk), lambda i, f, te, used: (te[i], 0, f_eff(i, f, used))),
            pl.BlockSpec((1, d, f_blk), lambda i, f, te, used: (te[i], 0, f_eff(i, f, used))),
            pl.BlockSpec((1, f_blk, d), lambda i, f, te, used: (te[i], f_eff(i, f, used), 0)),
        ],
        out_specs=pl.BlockSpec((tm * SLABS, LANES), lambda i, f, te, used: (i, 0)),
        scratch_shapes=[pltpu.VMEM((tm, d), BF16), pltpu.VMEM((tm, d), F32)],
    )
    return pl.pallas_call(
        functools.partial(_experts_kernel, tm=tm, tf=tf),
        grid_spec=grid_spec,
        out_shape=jax.ShapeDtypeStruct(xs.shape, F32),
        compiler_params=_cparams("arbitrary", "arbitrary"),
        name="moe_experts",
    )(tile_expert, n_used, xs, w1, w3, w2)


def _combine_kernel(dcur_ref, dnext_ref, x_ref, modl_ref, modc_ref, rec_ref, ys_ref, o_ref, buf, sems,
                    *, n_lat, tm, tiles_per_batch):
    i = pl.program_id(0)
    n = pl.num_programs(0)

    def gather(dest_ref, slot_buf, op):
        def body(r, carry):
            for slot in range(2):
                cp = _row_copy(ys_ref, dest_ref[0, 0, 2 * r + slot], buf.at[slot_buf, slot], r, sems.at[slot_buf])
                cp.start() if op == "start" else cp.wait()
            return carry
        lax.fori_loop(0, tm, body, 0)

    @pl.when(i == 0)
    def _():
        gather(dcur_ref, 0, "start")

    for parity in range(2):
        @pl.when((i + 1 < n) & ((i + 1) % 2 == parity))
        def _():
            gather(dnext_ref, parity, "start")

    for parity in range(2):
        @pl.when(i % 2 == parity)
        def _():
            gather(dcur_ref, parity, "wait")
            rec = rec_ref[0]
            y = (rec[:, ROUTE_G1:ROUTE_G1 + 1] * _slab_load(buf.at[parity, 0], tm)
                 + rec[:, ROUTE_G2:ROUTE_G2 + 1] * _slab_load(buf.at[parity, 1], tm))
            row0 = (i % tiles_per_batch) * tm
            gate = _gate_rows(modl_ref, modc_ref, 5, row0, tm, n_lat)
            o_ref[0] = x_ref[0] + gate * y


def _moe_combine(x, mod, rec, ys, dest, n_lat):
    b, t, d = x.shape
    tm = _pick_tile(t, (256,))
    tpb = t // tm
    n_tiles = b * tpb
    dest3 = dest.reshape(n_tiles, 1, 2 * tm)
    tok = lambda i: (i // tpb, i % tpb, 0)
    return pl.pallas_call(
        functools.partial(_combine_kernel, n_lat=n_lat, tm=tm, tiles_per_batch=tpb),
        grid=(n_tiles,),
        in_specs=[
            pl.BlockSpec((1, 1, 2 * tm), lambda i: (i, 0, 0), memory_space=pltpu.SMEM),
            pl.BlockSpec((1, 1, 2 * tm), lambda i: (jnp.minimum(i + 1, n_tiles - 1), 0, 0),
                         memory_space=pltpu.SMEM),
            pl.BlockSpec((1, tm, d), tok),
            pl.BlockSpec((1, N_MOD, d), lambda i: (i // tpb, 0, 0)),
            pl.BlockSpec((1, N_MOD, d), lambda i: (b, 0, 0)),
            pl.BlockSpec((1, tm, LANES), tok),
            pl.BlockSpec(memory_space=pl.ANY),
        ],
        out_specs=pl.BlockSpec((1, tm, d), tok),
        out_shape=jax.ShapeDtypeStruct((b, t, d), F32),
        scratch_shapes=[pltpu.VMEM((2, 2, tm * SLABS, LANES), F32), pltpu.SemaphoreType.DMA((2,))],
        input_output_aliases={2: 0},
        compiler_params=_cparams("arbitrary"),
        name="moe_combine",
    )(dest3, dest3, x, mod, mod, rec, ys)


def _dispatch_plan(rec, tm, n_tiles):
    experts = rec[:, ROUTE_E1:ROUTE_E2 + 1].astype(jnp.int32).reshape(-1)
    onehot = (experts[:, None] == jnp.arange(N_EXPERTS, dtype=jnp.int32)[None, :]).astype(jnp.int32)
    csum = jnp.cumsum(onehot, axis=0)
    counts = csum[-1]
    padded = (counts + tm - 1) // tm * tm
    ends = jnp.cumsum(padded)
    dest = jnp.sum(onehot * (csum - 1 + (ends - padded)[None, :]), axis=1)
    n_used = (ends[-1] // tm).astype(jnp.int32)
    tile_start = jnp.arange(n_tiles, dtype=jnp.int32) * tm
    tile_expert = jnp.sum((tile_start[:, None] >= ends[None, :]).astype(jnp.int32), axis=1)
    tile_expert = jnp.minimum(tile_expert, tile_expert[n_used - 1])
    return dest.astype(jnp.int32), tile_expert.astype(jnp.int32), n_used.reshape(1)


def _moe(x, mod, w_router, w1, w3, w2, n_lat):
    b, t, d = x.shape
    n_tok = b * t
    tm = 512
    n_tiles = (2 * n_tok + N_EXPERTS * (tm - 1)) // tm
    h_slabs, rec = _moe_route(x, mod, w_router, n_lat)
    dest, tile_expert, n_used = _dispatch_plan(rec.reshape(n_tok, LANES), tm, n_tiles)
    xs = _moe_dispatch(h_slabs, dest, n_tiles * tm)
    ys = _moe_experts(xs, tile_expert, n_used, w1, w3, w2, tm)
    return _moe_combine(x, mod, rec, ys, dest, n_lat)


def _rot_cols(w):
    half = HEAD_DIM // 2
    return jnp.concatenate([-w[..., half:], w[..., :half]], axis=-1)


def _with_rot(w_heads):
    d, n, _ = w_heads.shape
    return jnp.concatenate([w_heads, _rot_cols(w_heads)], axis=-1).reshape(d, n * LANES)


def _gain_lanes(gain):
    half = HEAD_DIM // 2
    perm = jnp.concatenate([gain[half:], gain[:half]])
    return jnp.concatenate([gain, perm]).reshape(1, LANES).astype(F32)


def _rope_table(n_lat, n_ctx):
    rows = n_lat // GRID_W
    row = jnp.repeat(jnp.arange(rows, dtype=F32), GRID_W)
    col = jnp.tile(jnp.arange(GRID_W, dtype=F32), rows)
    quarter = HEAD_DIM // 4
    inv_freq = ROPE_BASE ** (-jnp.arange(quarter, dtype=F32) / quarter)
    ang = jnp.concatenate([row[:, None] * inv_freq, col[:, None] * inv_freq], axis=-1)
    cos, sin = jnp.cos(ang), jnp.sin(ang)
    lat = jnp.concatenate([cos, cos, sin, sin], axis=-1)
    ctx = jnp.concatenate([jnp.ones((n_ctx, HEAD_DIM), F32), jnp.zeros((n_ctx, HEAD_DIM), F32)], axis=-1)
    return jnp.concatenate([lat, ctx], axis=0)


def _chan_tables():
    idx = jnp.arange(FNET_GROUP_DIM, dtype=jnp.int32)
    ang = ((idx[:, None] * idx[None, :]) % FNET_GROUP_DIM).astype(F32) * (2.0 * math.pi / FNET_GROUP_DIM)
    return jnp.cos(ang).astype(BF16), jnp.sin(ang).astype(BF16)


def kernel(x, c, ctx, c_ctx, w_mod, b_mod, w_in_even, w_out_even, ret_decay_fwd, ret_decay_bwd,
           ffn_w1, ffn_w3, ffn_w2, w_in_odd, w_out_odd, q_norm_gain, k_norm_gain,
           lambda_q1, lambda_k1, lambda_q2, lambda_k2, subln_gain, w_router, moe_w1, moe_w3, moe_w2):
    b, n_lat, d = x.shape
    n_ctx = ctx.shape[1]
    depth = w_mod.shape[0]
    assert d == D_MODEL and b < MOD_ROWS and n_lat % n_ctx == 0 and n_ctx % RET_CHUNK == 0

    stream = jnp.concatenate([x, ctx], axis=1)
    cond = jnp.zeros((MOD_ROWS, d), F32).at[:b].set(c).at[b].set(c_ctx)
    mods = _adaln(cond, w_mod, b_mod).reshape(depth, MOD_ROWS, N_MOD, d)
    tab = _rope_table(n_lat, n_ctx)
    chan_tabs = _chan_tables()

    for layer in range(depth):
        i = layer // 2
        mod = mods[layer]
        if layer % 2 == 0:
            w = w_in_even[i]
            hq = RET_HEADS * HEAD_DIM
            wq = w[:, FNET_WIDTH:FNET_WIDTH + hq].reshape(d, RET_HEADS, HEAD_DIM)
            wk = w[:, FNET_WIDTH + hq:FNET_WIDTH + 2 * hq].reshape(d, RET_HEADS, HEAD_DIM)
            w_in = jnp.concatenate(
                [w[:, :FNET_WIDTH], _with_rot(wq), _with_rot(wk), w[:, FNET_WIDTH + 2 * hq:]], axis=1).astype(BF16)
            hin = _inproj(stream, mod, w_in, tab, (), n_lat, w_in.shape[1], True)
            four = jnp.concatenate(
                [_fourier_seq(hin, 0, n_lat, chan_tabs), _fourier_seq(hin, n_lat, n_ctx, chan_tabs)], axis=1)
            dec_f = jnp.broadcast_to(ret_decay_fwd[i].astype(F32)[:, None, None], (RET_HEADS, 1, LANES))
            dec_b = jnp.broadcast_to(ret_decay_bwd[i].astype(F32)[:, None, None], (RET_HEADS, 1, LANES))
            ret = _retention(hin, dec_f, dec_b, n_lat)
            w_out = w_out_even[i].astype(BF16)
            stream = _merge(stream, mod, [four, ret], [w_out[:FNET_WIDTH], w_out[FNET_WIDTH:]], n_lat)
            stream = _ffn(stream, mod, ffn_w1[i].astype(BF16), ffn_w3[i].astype(BF16),
                          ffn_w2[i].astype(BF16), n_lat)
        else:
            w = w_in_odd[i]
            n_sub = 2 * DIFF_HEADS
            wq = w[:, :n_sub * HEAD_DIM].reshape(d, n_sub, HEAD_DIM)
            wk = w[:, n_sub * HEAD_DIM:2 * n_sub * HEAD_DIM].reshape(d, n_sub, HEAD_DIM)
            w_in = jnp.concatenate([_with_rot(wq), _with_rot(wk), w[:, 2 * n_sub * HEAD_DIM:]], axis=1).astype(BF16)
            gains = (_gain_lanes(q_norm_gain[i]), _gain_lanes(k_norm_gain[i]))
            qkv = _inproj(stream, mod, w_in, tab, gains, n_lat, 3 * DIFF_HEADS * LANES, False)
            lam_init = 0.8 - 0.6 * float(np.exp(-0.3 * layer))
            lam_vecs = jnp.zeros((4, LANES), F32).at[:, :HEAD_DIM].set(
                jnp.stack([lambda_q1[i], lambda_k1[i], lambda_q2[i], lambda_k2[i]]).astype(F32))
            sub_gain = subln_gain[i].astype(F32).reshape(1, LANES)
            att = jnp.zeros((b, n_lat + n_ctx, DIFF_HEADS * LANES), BF16)
            att = _attention(att, qkv, lam_vecs, sub_gain, lam_init, n_lat, False)
            att = _attention(att, qkv, lam_vecs, sub_gain, lam_init, n_lat, True)
            stream = _merge(stream, mod, [att], [w_out_odd[i].astype(BF16)], n_lat)
            wr = jnp.zeros((d, LANES), F32).at[:, :N_EXPERTS].set(w_router[i].astype(F32))
            stream = _moe(stream, mod, wr, moe_w1[i].astype(BF16), moe_w3[i].astype(BF16),
                          moe_w2[i].astype(BF16), n_lat)
    return stream[:, :n_lat]
```

```python
import functools
import math

import jax
import jax.numpy as jnp
import numpy as np
from jax import lax
from jax.experimental import pallas as pl
from jax.experimental.pallas import tpu as pltpu

F32 = jnp.float32
BF16 = jnp.bfloat16

D_MODEL = 1024
GRID_W = 64
HEAD_DIM = 64
LANES = 128
FNET_GROUPS = 4
FNET_GROUP_DIM = 128
FNET_WIDTH = FNET_GROUPS * FNET_GROUP_DIM
RET_HEADS = 4
RET_CHUNK = 128
DIFF_HEADS = 8
N_EXPERTS = 8
N_MOD = 6
ROPE_BASE = 10000.0
EPS = 1e-6
LOG2_E = math.log2(math.e)
VMEM_LIMIT_BYTES = 56 * 1024 * 1024
MOD_ROWS = 8


def _cparams(*sem):
    return pltpu.CompilerParams(dimension_semantics=sem, vmem_limit_bytes=VMEM_LIMIT_BYTES)


def _pick_tile(n, candidates):
    for c in candidates:
        if n % c == 0:
            return c
    raise ValueError(f"no tile in {candidates} divides {n}")


def _const_spec(shape):
    nd = len(shape)
    return pl.BlockSpec(shape, lambda *_: (0,) * nd)


def _resident_spec(shape):
    nd = len(shape)
    return pl.BlockSpec(shape, lambda *_: (0,) * nd, pipeline_mode=pl.Buffered(1))


def _sigmoid(x):
    return 1.0 / (1.0 + jnp.exp(-x))


def _modulated(x, modl_ref, modc_ref, k_shift, row0, n_lat):
    tm = x.shape[0]
    ms = jnp.mean(x * x, axis=-1, keepdims=True)
    xn = x * lax.rsqrt(ms + EPS)
    rows = row0 + lax.broadcasted_iota(jnp.int32, (tm, 1), 0)
    is_ctx = rows >= n_lat
    shift = jnp.where(is_ctx, modc_ref[0, k_shift:k_shift + 1, :], modl_ref[0, k_shift:k_shift + 1, :])
    scale = jnp.where(is_ctx, modc_ref[0, k_shift + 1:k_shift + 2, :], modl_ref[0, k_shift + 1:k_shift + 2, :])
    return xn * (1.0 + scale) + shift


def _gate_rows(modl_ref, modc_ref, k_gate, row0, tm, n_lat):
    rows = row0 + lax.broadcasted_iota(jnp.int32, (tm, 1), 0)
    return jnp.where(rows >= n_lat, modc_ref[0, k_gate:k_gate + 1, :], modl_ref[0, k_gate:k_gate + 1, :])


def _adaln_kernel(c_ref, w_ref, b_ref, o_ref):
    c = c_ref[...]
    s = c * _sigmoid(c)
    o_ref[0] = jnp.dot(s, w_ref[0], preferred_element_type=F32, precision=lax.Precision.HIGHEST) + b_ref[0]


def _adaln(cond_rows, w_mod, b_mod):
    depth, d, n = w_mod.shape
    tn = _pick_tile(n, (1536, 1024, 512))
    return pl.pallas_call(
        _adaln_kernel,
        grid=(depth, n // tn),
        in_specs=[
            _const_spec((MOD_ROWS, d)),
            pl.BlockSpec((1, d, tn), lambda l, j: (l, 0, j)),
            pl.BlockSpec((1, 1, tn), lambda l, j: (l, 0, j)),
        ],
        out_specs=pl.BlockSpec((1, MOD_ROWS, tn), lambda l, j: (l, 0, j)),
        out_shape=jax.ShapeDtypeStruct((depth, MOD_ROWS, n), F32),
        compiler_params=_cparams("parallel", "parallel"),
        name="adaln",
    )(cond_rows, w_mod, b_mod.reshape(depth, 1, n))


def _rope_dup(blk, tab):
    t = blk * tab
    return t + pltpu.roll(t, HEAD_DIM, axis=1)


def _inproj_even_kernel(x_ref, modl_ref, modc_ref, w_ref, tab_ref, o_ref, *, n_lat, tm):
    row0 = pl.program_id(1) * tm
    h = _modulated(x_ref[0], modl_ref, modc_ref, 0, row0, n_lat).astype(BF16)
    tab = tab_ref[...]
    lane = lax.broadcasted_iota(jnp.int32, (tm, LANES), 1)
    n_out = o_ref.shape[2]
    for c in range(n_out // 256):
        y = jnp.dot(h, w_ref[:, c * 256:(c + 1) * 256], preferred_element_type=F32)
        for half in range(2):
            blk = y[:, half * LANES:(half + 1) * LANES]
            col = c * 2 + half
            if 4 <= col < 8:
                blk = _rope_dup(blk, tab)
            elif 8 <= col < 12:
                blk = jnp.where(lane < HEAD_DIM, _rope_dup(blk, tab) * (HEAD_DIM ** -0.5), 0.0)
            o_ref[0, :, col * LANES:(col + 1) * LANES] = blk.astype(BF16)


def _inproj_odd_kernel(x_ref, modl_ref, modc_ref, w_ref, tab_ref, gq_ref, gk_ref, o_ref, *, n_lat, tm):
    row0 = pl.program_id(1) * tm
    h = _modulated(x_ref[0], modl_ref, modc_ref, 0, row0, n_lat).astype(BF16)
    tab = tab_ref[...]
    lane = lax.broadcasted_iota(jnp.int32, (tm, LANES), 1)
    n_qk = 2 * DIFF_HEADS
    for c in range(n_qk):
        y = jnp.dot(h, w_ref[:, c * 256:(c + 1) * 256], preferred_element_type=F32)
        gain = gq_ref[...] * (HEAD_DIM ** -0.5 * LOG2_E) if c < DIFF_HEADS else gk_ref[...]
        subs = []
        for half in range(2):
            blk = y[:, half * LANES:(half + 1) * LANES]
            ms = jnp.sum(blk * blk, axis=-1, keepdims=True) * (1.0 / LANES)
            subs.append(_rope_dup(blk * lax.rsqrt(ms + EPS) * gain, tab))
        o_ref[0, :, c * LANES:(c + 1) * LANES] = jnp.where(lane < HEAD_DIM, subs[0], subs[1]).astype(BF16)
    for c in range(DIFF_HEADS // 2):
        col0 = n_qk * 256 + c * 256
        y = jnp.dot(h, w_ref[:, col0:col0 + 256], preferred_element_type=F32)
        out0 = n_qk * LANES + c * 256
        o_ref[0, :, out0:out0 + 256] = y.astype(BF16)


def _inproj(x, mod, w, tab, gains, n_lat, n_out, even):
    b, t, d = x.shape
    tm = _pick_tile(t, (768, 256))
    kern = _inproj_even_kernel if even else _inproj_odd_kernel
    in_specs = [
        pl.BlockSpec((1, tm, d), lambda bi, i: (bi, i, 0)),
        pl.BlockSpec((1, N_MOD, d), lambda bi, i: (bi, 0, 0)),
        pl.BlockSpec((1, N_MOD, d), lambda bi, i: (b, 0, 0)),
        _resident_spec(w.shape),
        pl.BlockSpec((tm, LANES), lambda bi, i: (i, 0)),
    ] + [_const_spec((1, LANES))] * len(gains)
    return pl.pallas_call(
        functools.partial(kern, n_lat=n_lat, tm=tm),
        grid=(b, t // tm),
        in_specs=in_specs,
        out_specs=pl.BlockSpec((1, tm, n_out), lambda bi, i: (bi, i, 0)),
        out_shape=jax.ShapeDtypeStruct((b, t, n_out), BF16),
        compiler_params=_cparams("parallel", "parallel"),
        name="inproj_even" if even else "inproj_odd",
    )(x, mod, mod, w, tab, *gains)


def _dft_kernel(c_ref, s_ref, u_ref, cc_ref, sc_ref, o_ref, acc_c, acc_s, *, nb, scale):
    j = pl.program_id(1)

    @pl.when(j == 0)
    def _():
        acc_c[...] = jnp.zeros_like(acc_c)
        acc_s[...] = jnp.zeros_like(acc_s)

    cm = c_ref[...]
    sm = s_ref[...]
    for bi in range(nb):
        u = u_ref[bi]
        acc_c[bi] += jnp.dot(cm, u, preferred_element_type=F32)
        acc_s[bi] += jnp.dot(sm, u, preferred_element_type=F32)

    @pl.when(j == pl.num_programs(1) - 1)
    def _():
        cc = cc_ref[...]
        sc = sc_ref[...]
        for bi in range(nb):
            for g in range(FNET_GROUPS):
                sl = slice(g * FNET_GROUP_DIM, (g + 1) * FNET_GROUP_DIM)
                a = acc_c[bi, :, sl].astype(BF16)
                bm = acc_s[bi, :, sl].astype(BF16)
                y = jnp.dot(a, cc, preferred_element_type=F32) - jnp.dot(bm, sc, preferred_element_type=F32)
                o_ref[bi, :, sl] = (y * scale).astype(BF16)


def _angle_tables(n_rows, n_cols, stride, period):
    k = jnp.arange(n_rows, dtype=jnp.int32)[:, None]
    n = jnp.arange(n_cols, dtype=jnp.int32)[None, :]
    ang = ((k * n * stride) % period).astype(F32) * (2.0 * math.pi / period)
    return jnp.cos(ang), jnp.sin(ang)


def _dft_tables(n):
    if n <= LANES:
        c, s = _angle_tables(n, n, 1, n)
        return c.astype(BF16), s.astype(BF16)
    ca, sa = _angle_tables(n, n // LANES, LANES, n)
    cb, sb = _angle_tables(n, LANES, 1, n)
    cols = range(n // LANES)
    cm = jnp.concatenate([(ca[:, m:m + 1] * cb - sa[:, m:m + 1] * sb).astype(BF16) for m in cols], axis=1)
    sm = jnp.concatenate([(sa[:, m:m + 1] * cb + ca[:, m:m + 1] * sb).astype(BF16) for m in cols], axis=1)
    return cm, sm


def _fourier_seq(hin, row0, n, chan_tabs):
    b = hin.shape[0]
    cm, sm = _dft_tables(n)
    cc, sc = chan_tabs
    tk = _pick_tile(n, (1024, 512, 256))
    tn = tk
    off = row0 // tn
    return pl.pallas_call(
        functools.partial(_dft_kernel, nb=b, scale=1.0 / math.sqrt(n * FNET_GROUP_DIM)),
        grid=(n // tk, n // tn),
        in_specs=[
            pl.BlockSpec((tk, tn), lambda i, j: (i, j)),
            pl.BlockSpec((tk, tn), lambda i, j: (i, j)),
            pl.BlockSpec((b, tn, FNET_WIDTH), lambda i, j: (0, off + j, 0)),
            _const_spec((FNET_GROUP_DIM, FNET_GROUP_DIM)),
            _const_spec((FNET_GROUP_DIM, FNET_GROUP_DIM)),
        ],
        out_specs=pl.BlockSpec((b, tk, FNET_WIDTH), lambda i, j: (0, i, 0)),
        out_shape=jax.ShapeDtypeStruct((b, n, FNET_WIDTH), BF16),
        scratch_shapes=[pltpu.VMEM((b, tk, FNET_WIDTH), F32), pltpu.VMEM((b, tk, FNET_WIDTH), F32)],
        compiler_params=_cparams("parallel", "arbitrary"),
        name=f"fourier_seq_{n}",
    )(cm, sm, hin, cc, sc)


def _log_sigmoid(x):
    return jnp.minimum(x, 0.0) - jnp.log(1.0 + jnp.exp(-jnp.abs(x)))


def _retention_kernel(q_ref, k_ref, v_ref, g_ref, decf_ref, decb_ref, o_ref,
                      of_scr, ob_scr, dec_scr, *, n_lat, n_ctx):
    c = RET_CHUNK
    pos_i = lax.broadcasted_iota(jnp.int32, (c, c), 0).astype(F32)
    pos_j = lax.broadcasted_iota(jnp.int32, (c, c), 1).astype(F32)
    lg_f = _log_sigmoid(decf_ref[0])
    lg_b = _log_sigmoid(decb_ref[0])
    rel = pos_i - pos_j
    mask_f = rel >= 0.0
    mask_b = rel < 0.0
    dec_scr[0] = jnp.where(mask_f, jnp.exp(lg_f * jnp.where(mask_f, rel, 0.0)), 0.0)
    dec_scr[1] = jnp.where(mask_b, jnp.exp(lg_b * jnp.where(mask_b, -rel, 0.0)), 0.0)
    dec_scr[2] = jnp.exp(lg_f * (pos_i + 1.0))
    dec_scr[3] = jnp.exp(lg_b * (c - pos_i))
    dec_scr[4] = jnp.exp(lg_f * (c - 1.0 - pos_i))
    dec_scr[5] = jnp.exp(lg_b * pos_i)
    dec_scr[6] = jnp.exp(lg_f * (c + 0.0 * pos_i))
    dec_scr[7] = jnp.exp(lg_b * (c + 0.0 * pos_i))
    def one_dir(d, start, out_scr, state):
        qc = q_ref[0, pl.ds(start, c), :]
        kc = k_ref[0, pl.ds(start, c), :]
        vc = v_ref[0, pl.ds(start, c), :]
        scores = lax.dot_general(qc, kc, (((1,), (1,)), ((), ())), preferred_element_type=F32)
        intra = jnp.dot((scores * dec_scr[d]).astype(BF16), vc, preferred_element_type=F32)
        inter = jnp.dot(qc, state.astype(BF16), preferred_element_type=F32)
        out_scr[pl.ds(start, c), :] = intra + dec_scr[2 + d] * inter
        vd = (vc.astype(F32) * dec_scr[4 + d]).astype(BF16)
        kv = lax.dot_general(kc, vd, (((0,), (0,)), ((), ())), preferred_element_type=F32)
        return dec_scr[6 + d] * state + kv

    s_f = jnp.zeros((LANES, LANES), F32)
    s_b = jnp.zeros((LANES, LANES), F32)
    n_ctx_chunks = n_ctx // c
    for ci in range(n_ctx_chunks):
        s_f = one_dir(0, n_lat + ci * c, of_scr, s_f)
        s_b = one_dir(1, n_lat + (n_ctx_chunks - 1 - ci) * c, ob_scr, s_b)

    n_lat_chunks = n_lat // c

    def body(i, states):
        s_f = one_dir(0, pl.multiple_of(i * c, c), of_scr, states[0])
        s_b = one_dir(1, pl.multiple_of((n_lat_chunks - 1 - i) * c, c), ob_scr, states[1])
        return s_f, s_b

    lax.fori_loop(0, n_lat_chunks, body, (s_f, s_b), unroll=_pick_tile(n_lat_chunks, (4, 2, 1)))

    def finish(i, carry):
        r0 = pl.multiple_of(i * c, c)
        o = of_scr[pl.ds(r0, c), :] + ob_scr[pl.ds(r0, c), :]
        o = o * lax.rsqrt(jnp.mean(o * o, axis=-1, keepdims=True) + EPS)
        g = g_ref[0, pl.ds(r0, c), :].astype(F32)
        o_ref[0, pl.ds(r0, c), :] = (o * (g * _sigmoid(g))).astype(BF16)
        return carry

    lax.fori_loop(0, (n_lat + n_ctx) // c, finish, 0)


def _retention(hin, dec_f, dec_b, n_lat):
    b, t, _ = hin.shape
    h = RET_HEADS
    seq = lambda col0: pl.BlockSpec((1, t, LANES), lambda bi, hi: (bi, 0, col0 + hi))
    dec = pl.BlockSpec((1, 1, LANES), lambda bi, hi: (hi, 0, 0))
    return pl.pallas_call(
        functools.partial(_retention_kernel, n_lat=n_lat, n_ctx=t - n_lat),
        grid=(b, h),
        in_specs=[seq(4), seq(8), seq(12), seq(16), dec, dec],
        out_specs=pl.BlockSpec((1, t, LANES), lambda bi, hi: (bi, 0, hi)),
        out_shape=jax.ShapeDtypeStruct((b, t, h * LANES), BF16),
        scratch_shapes=[
            pltpu.VMEM((t, LANES), F32),
            pltpu.VMEM((t, LANES), F32),
            pltpu.VMEM((8, RET_CHUNK, LANES), F32),
        ],
        compiler_params=_cparams("parallel", "parallel"),
        name="retention",
    )(hin, hin, hin, hin, dec_f, dec_b)


def _merge_kernel(*refs, n_parts, n_lat, tm):
    x_ref, modl_ref, modc_ref = refs[:3]
    a_refs = refs[3:3 + n_parts]
    w_refs = refs[3 + n_parts:3 + 2 * n_parts]
    o_ref = refs[3 + 2 * n_parts]
    row0 = pl.program_id(1) * tm
    y = jnp.dot(a_refs[0][0], w_refs[0][...], preferred_element_type=F32)
    for p in range(1, n_parts):
        y = y + jnp.dot(a_refs[p][0], w_refs[p][...], preferred_element_type=F32)
    gate = _gate_rows(modl_ref, modc_ref, 2, row0, tm, n_lat)
    o_ref[0] = x_ref[0] + gate * y


def _merge(x, mod, parts, weights, n_lat):
    b, t, d = x.shape
    tm = _pick_tile(t, (768, 256))
    n_parts = len(parts)
    in_specs = [
        pl.BlockSpec((1, tm, d), lambda bi, i: (bi, i, 0)),
        pl.BlockSpec((1, N_MOD, d), lambda bi, i: (bi, 0, 0)),
        pl.BlockSpec((1, N_MOD, d), lambda bi, i: (b, 0, 0)),
    ]
    in_specs += [pl.BlockSpec((1, tm, p.shape[2]), lambda bi, i: (bi, i, 0)) for p in parts]
    in_specs += [_resident_spec(w.shape) for w in weights]
    return pl.pallas_call(
        functools.partial(_merge_kernel, n_parts=n_parts, n_lat=n_lat, tm=tm),
        grid=(b, t // tm),
        in_specs=in_specs,
        out_specs=pl.BlockSpec((1, tm, d), lambda bi, i: (bi, i, 0)),
        out_shape=jax.ShapeDtypeStruct((b, t, d), F32),
        input_output_aliases={0: 0},
        compiler_params=_cparams("parallel", "parallel"),
        name=f"merge_{n_parts}",
    )(x, mod, mod, *parts, *weights)


def _attn_kernel(prev_ref, q_ref, k_ref, v_ref, lam_ref, sg_ref, o_ref, *, tk, lam_init):
    del prev_ref
    tq = q_ref.shape[1]
    n_keys = k_ref.shape[1]
    q = q_ref[0]
    lane = lax.broadcasted_iota(jnp.int32, (tq, LANES), 1)
    zero = jnp.zeros_like(q)
    q_sub = (jnp.where(lane < HEAD_DIM, q, zero), jnp.where(lane >= HEAD_DIM, q, zero))
    ones = jnp.ones((tk, LANES), BF16)
    m = [None, None]
    acc = [None, None]
    for j in range(n_keys // tk):
        kc = k_ref[0, j * tk:(j + 1) * tk, :]
        v1 = jnp.concatenate([v_ref[0, j * tk:(j + 1) * tk, :], ones], axis=1)
        for sub in range(2):
            s = lax.dot_general(q_sub[sub], kc, (((1,), (1,)), ((), ())), preferred_element_type=F32)
            mx = jnp.max(s, axis=-1, keepdims=True)
            if j == 0:
                m[sub] = mx
                p = jnp.exp2(s - mx).astype(BF16)
                acc[sub] = jnp.dot(p, v1, preferred_element_type=F32)
            else:
                m_new = jnp.maximum(m[sub], mx)
                alpha = jnp.exp2(m[sub] - m_new)
                p = jnp.exp2(s - m_new).astype(BF16)
                acc[sub] = alpha * acc[sub] + jnp.dot(p, v1, preferred_element_type=F32)
                m[sub] = m_new

    lv = lam_ref[...]
    s1 = jnp.sum(lv[0:1] * lv[1:2], axis=-1, keepdims=True)
    s2 = jnp.sum(lv[2:3] * lv[3:4], axis=-1, keepdims=True)
    lam = jnp.exp(s1) - jnp.exp(s2) + lam_init
    o = acc[0][:, :LANES] / acc[0][:, LANES:] - lam * (acc[1][:, :LANES] / acc[1][:, LANES:])
    o = o * lax.rsqrt(jnp.mean(o * o, axis=-1, keepdims=True) + EPS)
    o_ref[0] = (o * sg_ref[...] * (1.0 - lam_init)).astype(BF16)


def _attention(out, qkv, lam_vecs, sub_gain, lam_init, n_lat, ctx_queries):
    b, t, _ = qkv.shape
    h = DIFF_HEADS
    n_ctx = t - n_lat
    if ctx_queries:
        tq, n_q_tiles, q_off = n_ctx, 1, n_lat // n_ctx
        keys, k_off = n_ctx, n_lat // n_ctx
        tk = n_ctx
    else:
        tq = _pick_tile(n_lat, (512, 256))
        n_q_tiles, q_off = n_lat // tq, 0
        keys, k_off = t, 0
        tk = _pick_tile(t, (768, 256))
    return pl.pallas_call(
        functools.partial(_attn_kernel, tk=tk, lam_init=lam_init),
        grid=(b, h, n_q_tiles),
        in_specs=[
            pl.BlockSpec(memory_space=pl.ANY),
            pl.BlockSpec((1, tq, LANES), lambda bi, hi, i: (bi, q_off + i, hi)),
            pl.BlockSpec((1, keys, LANES), lambda bi, hi, i: (bi, k_off, h + hi)),
            pl.BlockSpec((1, keys, LANES), lambda bi, hi, i: (bi, k_off, 2 * h + hi)),
            _const_spec((4, LANES)),
            _const_spec((1, LANES)),
        ],
        out_specs=pl.BlockSpec((1, tq, LANES), lambda bi, hi, i: (bi, q_off + i, hi)),
        out_shape=jax.ShapeDtypeStruct((b, t, h * LANES), BF16),
        input_output_aliases={0: 0},
        compiler_params=_cparams("parallel", "parallel", "arbitrary"),
        name="diff_attn_ctx" if ctx_queries else "diff_attn_lat",
    )(out, qkv, qkv, qkv, lam_vecs, sub_gain)


def _ffn_kernel(x_ref, modl_ref, modc_ref, w1_ref, w3_ref, w2_ref, o_ref, *, n_lat, tm, tf):
    row0 = pl.program_id(1) * tm
    x = x_ref[0]
    h = _modulated(x, modl_ref, modc_ref, 3, row0, n_lat).astype(BF16)
    f_dim = w1_ref.shape[1]
    y = jnp.zeros((tm, x.shape[1]), F32)
    for c in range(f_dim // tf):
        sl = slice(c * tf, (c + 1) * tf)
        a = jnp.dot(h, w1_ref[:, sl], preferred_element_type=F32)
        g = jnp.dot(h, w3_ref[:, sl], preferred_element_type=F32)
        u = (a * _sigmoid(a) * g).astype(BF16)
        y = y + jnp.dot(u, w2_ref[sl, :], preferred_element_type=F32)
    gate = _gate_rows(modl_ref, modc_ref, 5, row0, tm, n_lat)
    o_ref[0] = x + gate * y


def _ffn(x, mod, w1, w3, w2, n_lat):
    b, t, d = x.shape
    tm = _pick_tile(t, (768, 256))
    tf = _pick_tile(w1.shape[1], (256, 128))
    return pl.pallas_call(
        functools.partial(_ffn_kernel, n_lat=n_lat, tm=tm, tf=tf),
        grid=(b, t // tm),
        in_specs=[
            pl.BlockSpec((1, tm, d), lambda bi, i: (bi, i, 0)),
            pl.BlockSpec((1, N_MOD, d), lambda bi, i: (bi, 0, 0)),
            pl.BlockSpec((1, N_MOD, d), lambda bi, i: (b, 0, 0)),
            _resident_spec(w1.shape),
            _resident_spec(w3.shape),
            _resident_spec(w2.shape),
        ],
        out_specs=pl.BlockSpec((1, tm, d), lambda bi, i: (bi, i, 0)),
        out_shape=jax.ShapeDtypeStruct((b, t, d), F32),
        input_output_aliases={0: 0},
        compiler_params=_cparams("parallel", "parallel"),
        name="ffn_swiglu",
    )(x, mod, mod, w1, w3, w2)


ROUTE_E1, ROUTE_E2, ROUTE_G1, ROUTE_G2 = 0, 1, 2, 3
SLABS = D_MODEL // LANES
DMA_LOOP_UNROLL = 8


def _top2_route(logits):
    tm = logits.shape[0]
    lane = lax.broadcasted_iota(jnp.int32, (tm, LANES), 1)
    neg = jnp.float32(-jnp.inf)
    lg = jnp.where(lane < N_EXPERTS, logits, neg)
    m1 = jnp.max(lg, axis=-1, keepdims=True)
    i1 = jnp.min(jnp.where(lg == m1, lane, LANES), axis=-1, keepdims=True)
    lg2 = jnp.where(lane == i1, neg, lg)
    m2 = jnp.max(lg2, axis=-1, keepdims=True)
    i2 = jnp.min(jnp.where(lg2 == m2, lane, LANES), axis=-1, keepdims=True)
    e = jnp.exp(m2 - m1)
    g1 = 1.0 / (1.0 + e)
    g2 = e / (1.0 + e)
    rec = jnp.where(lane == ROUTE_E1, i1.astype(F32), 0.0) + jnp.where(lane == ROUTE_E2, i2.astype(F32), 0.0)
    return rec + jnp.where(lane == ROUTE_G1, g1, 0.0) + jnp.where(lane == ROUTE_G2, g2, 0.0)


def _slab_store(ref, val, n_rows):
    for s in range(SLABS):
        ref[pl.ds(s, n_rows, stride=SLABS), :] = val[:, s * LANES:(s + 1) * LANES]


def _slab_load(ref, n_rows):
    return jnp.concatenate([ref[pl.ds(s, n_rows, stride=SLABS), :] for s in range(SLABS)], axis=1)


def _route_kernel(x_ref, modl_ref, modc_ref, wr_ref, h_ref, rec_ref, *, n_lat, tm):
    row0 = pl.program_id(1) * tm
    hf = _modulated(x_ref[0], modl_ref, modc_ref, 3, row0, n_lat)
    logits = jnp.dot(hf, wr_ref[...], preferred_element_type=F32, precision=lax.Precision.HIGHEST)
    rec_ref[0] = _top2_route(logits)
    _slab_store(h_ref, hf, tm)


def _moe_route(x, mod, w_router, n_lat):
    b, t, d = x.shape
    tm = _pick_tile(t, (768, 256))
    tiles = t // tm
    return pl.pallas_call(
        functools.partial(_route_kernel, n_lat=n_lat, tm=tm),
        grid=(b, tiles),
        in_specs=[
            pl.BlockSpec((1, tm, d), lambda bi, i: (bi, i, 0)),
            pl.BlockSpec((1, N_MOD, d), lambda bi, i: (bi, 0, 0)),
            pl.BlockSpec((1, N_MOD, d), lambda bi, i: (b, 0, 0)),
            _const_spec((d, LANES)),
        ],
        out_specs=[
            pl.BlockSpec((tm * SLABS, LANES), lambda bi, i: (bi * tiles + i, 0)),
            pl.BlockSpec((1, tm, LANES), lambda bi, i: (bi, i, 0)),
        ],
        out_shape=[
            jax.ShapeDtypeStruct((b * t * SLABS, LANES), F32),
            jax.ShapeDtypeStruct((b, t, LANES), F32),
        ],
        compiler_params=_cparams("parallel", "parallel"),
        name="moe_route",
    )(x, mod, mod, w_router)


def _row_copy(src_ref, src_row, dst_ref, dst_row, sem):
    return pltpu.make_async_copy(
        src_ref.at[pl.ds(pl.multiple_of(src_row * SLABS, SLABS), SLABS), :],
        dst_ref.at[pl.ds(pl.multiple_of(dst_row * SLABS, SLABS), SLABS), :],
        sem)


def _dispatch_kernel(dest_ref, h_ref, init_ref, xs_ref, sem, *, tm):
    del init_ref

    def start(r, carry):
        for slot in range(2):
            _row_copy(h_ref, r, xs_ref, dest_ref[0, 0, 2 * r + slot], sem).start()
        return carry

    def wait(r, carry):
        for slot in range(2):
            _row_copy(h_ref, r, xs_ref, dest_ref[0, 0, 2 * r + slot], sem).wait()
        return carry

    lax.fori_loop(0, tm, start, 0, unroll=DMA_LOOP_UNROLL)
    lax.fori_loop(0, tm, wait, 0, unroll=DMA_LOOP_UNROLL)


def _moe_dispatch(h_slabs, dest, n_sorted_rows):
    n_tok = h_slabs.shape[0] // SLABS
    tm = _pick_tile(n_tok, (512, 256))
    steps = n_tok // tm
    return pl.pallas_call(
        functools.partial(_dispatch_kernel, tm=tm),
        grid=(steps,),
        in_specs=[
            pl.BlockSpec((1, 1, 2 * tm), lambda i: (i, 0, 0), memory_space=pltpu.SMEM),
            pl.BlockSpec((tm * SLABS, LANES), lambda i: (i, 0)),
            pl.BlockSpec(memory_space=pl.ANY),
        ],
        out_specs=pl.BlockSpec(memory_space=pl.ANY),
        out_shape=jax.ShapeDtypeStruct((n_sorted_rows * SLABS, LANES), F32),
        scratch_shapes=[pltpu.SemaphoreType.DMA(())],
        input_output_aliases={2: 0},
        compiler_params=_cparams("arbitrary"),
        name="moe_dispatch",
    )(dest.reshape(steps, 1, 2 * tm), h_slabs, jnp.zeros((n_sorted_rows * SLABS, LANES), F32))


def _experts_kernel(te_ref, used_ref, x_ref, w1_ref, w3_ref, w2_ref, o_ref, h_scr, acc_scr, *, tm, tf):
    del te_ref
    i = pl.program_id(0)
    f = pl.program_id(1)
    valid = i < used_ref[0]

    @pl.when(f == 0)
    def _():
        acc_scr[...] = jnp.zeros_like(acc_scr)

    @pl.when(valid & (f == 0))
    def _():
        h_scr[...] = _slab_load(x_ref, tm).astype(BF16)

    @pl.when(valid)
    def _():
        h = h_scr[...]
        f_blk = w1_ref.shape[2]
        y = jnp.zeros(acc_scr.shape, F32)
        for c in range(f_blk // tf):
            sl = slice(c * tf, (c + 1) * tf)
            a = jnp.dot(h, w1_ref[0, :, sl], preferred_element_type=F32)
            g = jnp.dot(h, w3_ref[0, :, sl], preferred_element_type=F32)
            u = (a * _sigmoid(a) * g).astype(BF16)
            y = y + jnp.dot(u, w2_ref[0, sl, :], preferred_element_type=F32)
        acc_scr[...] += y

    @pl.when(f == pl.num_programs(1) - 1)
    def _():
        _slab_store(o_ref, acc_scr[...], tm)


def _moe_experts(xs, tile_expert, n_used, w1, w3, w2, tm):
    n_exp, d, f_dim = w1.shape
    n_tiles = xs.shape[0] // (tm * SLABS)
    f_blk = _pick_tile(f_dim, (1792, 512, 256))
    tf = _pick_tile(f_blk, (256, 128))
    n_f = f_dim // f_blk

    def f_eff(i, f, used):
        return jnp.where(i < used[0], f, n_f - 1)

    grid_spec = pltpu.PrefetchScalarGridSpec(
        num_scalar_prefetch=2,
        grid=(n_tiles, n_f),
        in_specs=[
            pl.BlockSpec((tm * SLABS, LANES), lambda i, f, te, used: (jnp.minimum(i, used[0] - 1), 0)),
            pl.BlockSpec((1, d, f_blk), lambda i, f, te, used: (te[i], 0, f_eff(i, f, used))),
            pl.BlockSpec((1, d, f_blk), lambda i, f, te, used: (te[i], 0, f_eff(i, f, used))),
            pl.BlockSpec((1, f_blk, d), lambda i, f, te, used: (te[i], f_eff(i, f, used), 0)),
        ],
        out_specs=pl.BlockSpec((tm * SLABS, LANES), lambda i, f, te, used: (i, 0)),
        scratch_shapes=[pltpu.VMEM((tm, d), BF16), pltpu.VMEM((tm, d), F32)],
    )
    return pl.pallas_call(
        functools.partial(_experts_kernel, tm=tm, tf=tf),
        grid_spec=grid_spec,
        out_shape=jax.ShapeDtypeStruct(xs.shape, F32),
        compiler_params=_cparams("arbitrary", "arbitrary"),
        name="moe_experts",
    )(tile_expert, n_used, xs, w1, w3, w2)


def _combine_kernel(dcur_ref, dnext_ref, x_ref, modl_ref, modc_ref, rec_ref, ys_ref, o_ref, buf, sems,
                    *, n_lat, tm, tiles_per_batch):
    i = pl.program_id(0)
    n = pl.num_programs(0)

    def gather(dest_ref, slot_buf, op):
        def body(r, carry):
            for slot in range(2):
                cp = _row_copy(ys_ref, dest_ref[0, 0, 2 * r + slot], buf.at[slot_buf, slot], r, sems.at[slot_buf])
                cp.start() if op == "start" else cp.wait()
            return carry
        lax.fori_loop(0, tm, body, 0, unroll=DMA_LOOP_UNROLL)

    @pl.when(i == 0)
    def _():
        gather(dcur_ref, 0, "start")

    for parity in range(2):
        @pl.when((i + 1 < n) & ((i + 1) % 2 == parity))
        def _():
            gather(dnext_ref, parity, "start")

    for parity in range(2):
        @pl.when(i % 2 == parity)
        def _():
            gather(dcur_ref, parity, "wait")
            rec = rec_ref[0]
            y = (rec[:, ROUTE_G1:ROUTE_G1 + 1] * _slab_load(buf.at[parity, 0], tm)
                 + rec[:, ROUTE_G2:ROUTE_G2 + 1] * _slab_load(buf.at[parity, 1], tm))
            row0 = (i % tiles_per_batch) * tm
            gate = _gate_rows(modl_ref, modc_ref, 5, row0, tm, n_lat)
            o_ref[0] = x_ref[0] + gate * y


def _moe_combine(x, mod, rec, ys, dest, n_lat):
    b, t, d = x.shape
    tm = _pick_tile(t, (256,))
    tpb = t // tm
    n_tiles = b * tpb
    dest3 = dest.reshape(n_tiles, 1, 2 * tm)
    tok = lambda i: (i // tpb, i % tpb, 0)
    return pl.pallas_call(
        functools.partial(_combine_kernel, n_lat=n_lat, tm=tm, tiles_per_batch=tpb),
        grid=(n_tiles,),
        in_specs=[
            pl.BlockSpec((1, 1, 2 * tm), lambda i: (i, 0, 0), memory_space=pltpu.SMEM),
            pl.BlockSpec((1, 1, 2 * tm), lambda i: (jnp.minimum(i + 1, n_tiles - 1), 0, 0),
                         memory_space=pltpu.SMEM),
            pl.BlockSpec((1, tm, d), tok),
            pl.BlockSpec((1, N_MOD, d), lambda i: (i // tpb, 0, 0)),
            pl.BlockSpec((1, N_MOD, d), lambda i: (b, 0, 0)),
            pl.BlockSpec((1, tm, LANES), tok),
            pl.BlockSpec(memory_space=pl.ANY),
        ],
        out_specs=pl.BlockSpec((1, tm, d), tok),
        out_shape=jax.ShapeDtypeStruct((b, t, d), F32),
        scratch_shapes=[pltpu.VMEM((2, 2, tm * SLABS, LANES), F32), pltpu.SemaphoreType.DMA((2,))],
        input_output_aliases={2: 0},
        compiler_params=_cparams("arbitrary"),
        name="moe_combine",
    )(dest3, dest3, x, mod, mod, rec, ys)


def _dispatch_plan(rec, tm, n_tiles):
    experts = rec[:, ROUTE_E1:ROUTE_E2 + 1].astype(jnp.int32).reshape(-1)
    onehot = (experts[:, None] == jnp.arange(N_EXPERTS, dtype=jnp.int32)[None, :]).astype(jnp.int32)
    csum = jnp.cumsum(onehot, axis=0)
    counts = csum[-1]
    padded = (counts + tm - 1) // tm * tm
    ends = jnp.cumsum(padded)
    dest = jnp.sum(onehot * (csum - 1 + (ends - padded)[None, :]), axis=1)
    n_used = (ends[-1] // tm).astype(jnp.int32)
    tile_start = jnp.arange(n_tiles, dtype=jnp.int32) * tm
    tile_expert = jnp.sum((tile_start[:, None] >= ends[None, :]).astype(jnp.int32), axis=1)
    tile_expert = jnp.minimum(tile_expert, tile_expert[n_used - 1])
    return dest.astype(jnp.int32), tile_expert.astype(jnp.int32), n_used.reshape(1)


def _moe(x, mod, w_router, w1, w3, w2, first_expert, n_lat):
    b, t, d = x.shape
    n_tok = b * t
    tm = 512
    n_tiles = (2 * n_tok + N_EXPERTS * (tm - 1)) // tm
    h_slabs, rec = _moe_route(x, mod, w_router, n_lat)
    dest, tile_expert, n_used = _dispatch_plan(rec.reshape(n_tok, LANES), tm, n_tiles)
    xs = _moe_dispatch(h_slabs, dest, n_tiles * tm)
    ys = _moe_experts(xs, tile_expert + first_expert, n_used, w1, w3, w2, tm)
    return _moe_combine(x, mod, rec, ys, dest, n_lat)


def _rot_cols(w):
    half = HEAD_DIM // 2
    return jnp.concatenate([-w[..., half:], w[..., :half]], axis=-1)


def _with_rot(w_heads):
    d, n, _ = w_heads.shape
    return jnp.concatenate([w_heads, _rot_cols(w_heads)], axis=-1).reshape(d, n * LANES)


def _gain_lanes(gain):
    half = HEAD_DIM // 2
    perm = jnp.concatenate([gain[half:], gain[:half]])
    return jnp.concatenate([gain, perm]).reshape(1, LANES).astype(F32)


def _rope_table(n_lat, n_ctx):
    rows = n_lat // GRID_W
    row = jnp.repeat(jnp.arange(rows, dtype=F32), GRID_W)
    col = jnp.tile(jnp.arange(GRID_W, dtype=F32), rows)
    quarter = HEAD_DIM // 4
    inv_freq = ROPE_BASE ** (-jnp.arange(quarter, dtype=F32) / quarter)
    ang = jnp.concatenate([row[:, None] * inv_freq, col[:, None] * inv_freq], axis=-1)
    cos, sin = jnp.cos(ang), jnp.sin(ang)
    lat = jnp.concatenate([cos, cos, sin, sin], axis=-1)
    ctx = jnp.concatenate([jnp.ones((n_ctx, HEAD_DIM), F32), jnp.zeros((n_ctx, HEAD_DIM), F32)], axis=-1)
    return jnp.concatenate([lat, ctx], axis=0)


def _chan_tables():
    idx = jnp.arange(FNET_GROUP_DIM, dtype=jnp.int32)
    ang = ((idx[:, None] * idx[None, :]) % FNET_GROUP_DIM).astype(F32) * (2.0 * math.pi / FNET_GROUP_DIM)
    return jnp.cos(ang).astype(BF16), jnp.sin(ang).astype(BF16)


def kernel(x, c, ctx, c_ctx, w_mod, b_mod, w_in_even, w_out_even, ret_decay_fwd, ret_decay_bwd,
           ffn_w1, ffn_w3, ffn_w2, w_in_odd, w_out_odd, q_norm_gain, k_norm_gain,
           lambda_q1, lambda_k1, lambda_q2, lambda_k2, subln_gain, w_router, moe_w1, moe_w3, moe_w2):
    b, n_lat, d = x.shape
    n_ctx = ctx.shape[1]
    depth = w_mod.shape[0]
    assert d == D_MODEL and b < MOD_ROWS and n_lat % n_ctx == 0 and n_ctx % RET_CHUNK == 0

    stream = jnp.concatenate([x, ctx], axis=1)
    cond = jnp.zeros((MOD_ROWS, d), F32).at[:b].set(c).at[b].set(c_ctx)
    mods = _adaln(cond, w_mod, b_mod).reshape(depth, MOD_ROWS, N_MOD, d)
    tab = _rope_table(n_lat, n_ctx)
    chan_tabs = _chan_tables()
    stack_experts = lambda w: w.astype(BF16).reshape((-1,) + w.shape[2:])
    moe_w1_all, moe_w3_all, moe_w2_all = stack_experts(moe_w1), stack_experts(moe_w3), stack_experts(moe_w2)

    for layer in range(depth):
        i = layer // 2
        mod = mods[layer]
        if layer % 2 == 0:
            w = w_in_even[i]
            hq = RET_HEADS * HEAD_DIM
            wq = w[:, FNET_WIDTH:FNET_WIDTH + hq].reshape(d, RET_HEADS, HEAD_DIM)
            wk = w[:, FNET_WIDTH + hq:FNET_WIDTH + 2 * hq].reshape(d, RET_HEADS, HEAD_DIM)
            w_in = jnp.concatenate(
                [w[:, :FNET_WIDTH], _with_rot(wq), _with_rot(wk), w[:, FNET_WIDTH + 2 * hq:]], axis=1).astype(BF16)
            hin = _inproj(stream, mod, w_in, tab, (), n_lat, w_in.shape[1], True)
            four = jnp.concatenate(
                [_fourier_seq(hin, 0, n_lat, chan_tabs), _fourier_seq(hin, n_lat, n_ctx, chan_tabs)], axis=1)
            dec_f = jnp.broadcast_to(ret_decay_fwd[i].astype(F32)[:, None, None], (RET_HEADS, 1, LANES))
            dec_b = jnp.broadcast_to(ret_decay_bwd[i].astype(F32)[:, None, None], (RET_HEADS, 1, LANES))
            ret = _retention(hin, dec_f, dec_b, n_lat)
            w_out = w_out_even[i].astype(BF16)
            stream = _merge(stream, mod, [four, ret], [w_out[:FNET_WIDTH], w_out[FNET_WIDTH:]], n_lat)
            stream = _ffn(stream, mod, ffn_w1[i].astype(BF16), ffn_w3[i].astype(BF16),
                          ffn_w2[i].astype(BF16), n_lat)
        else:
            w = w_in_odd[i]
            n_sub = 2 * DIFF_HEADS
            wq = w[:, :n_sub * HEAD_DIM].reshape(d, n_sub, HEAD_DIM)
            wk = w[:, n_sub * HEAD_DIM:2 * n_sub * HEAD_DIM].reshape(d, n_sub, HEAD_DIM)
            w_in = jnp.concatenate([_with_rot(wq), _with_rot(wk), w[:, 2 * n_sub * HEAD_DIM:]], axis=1).astype(BF16)
            gains = (_gain_lanes(q_norm_gain[i]), _gain_lanes(k_norm_gain[i]))
            qkv = _inproj(stream, mod, w_in, tab, gains, n_lat, 3 * DIFF_HEADS * LANES, False)
            lam_init = 0.8 - 0.6 * float(np.exp(-0.3 * layer))
            lam_vecs = jnp.zeros((4, LANES), F32).at[:, :HEAD_DIM].set(
                jnp.stack([lambda_q1[i], lambda_k1[i], lambda_q2[i], lambda_k2[i]]).astype(F32))
            sub_gain = subln_gain[i].astype(F32).reshape(1, LANES)
            att = jnp.zeros((b, n_lat + n_ctx, DIFF_HEADS * LANES), BF16)
            att = _attention(att, qkv, lam_vecs, sub_gain, lam_init, n_lat, False)
            att = _attention(att, qkv, lam_vecs, sub_gain, lam_init, n_lat, True)
            stream = _merge(stream, mod, [att], [w_out_odd[i].astype(BF16)], n_lat)
            wr = jnp.zeros((d, LANES), F32).at[:, :N_EXPERTS].set(w_router[i].astype(F32))
            stream = _moe(stream, mod, wr, moe_w1_all, moe_w3_all, moe_w2_all, i * N_EXPERTS, n_lat)
    return stream[:, :n_lat]
```

```python
import functools
import math

import jax
import jax.numpy as jnp
import numpy as np
from jax import lax
from jax.experimental import pallas as pl
from jax.experimental.pallas import tpu as pltpu

F32 = jnp.float32
BF16 = jnp.bfloat16

D_MODEL = 1024
GRID_W = 64
HEAD_DIM = 64
LANES = 128
FNET_GROUPS = 4
FNET_GROUP_DIM = 128
FNET_WIDTH = FNET_GROUPS * FNET_GROUP_DIM
RET_HEADS = 4
RET_CHUNK = 128
DIFF_HEADS = 8
N_EXPERTS = 8
N_MOD = 6
ROPE_BASE = 10000.0
EPS = 1e-6
LOG2_E = math.log2(math.e)
VMEM_LIMIT_BYTES = 56 * 1024 * 1024
MOD_ROWS = 8


def _cparams(*sem):
    return pltpu.CompilerParams(dimension_semantics=sem, vmem_limit_bytes=VMEM_LIMIT_BYTES)


def _pick_tile(n, candidates):
    for c in candidates:
        if n % c == 0:
            return c
    raise ValueError(f"no tile in {candidates} divides {n}")


def _const_spec(shape):
    nd = len(shape)
    return pl.BlockSpec(shape, lambda *_: (0,) * nd)


def _resident_spec(shape):
    nd = len(shape)
    return pl.BlockSpec(shape, lambda *_: (0,) * nd, pipeline_mode=pl.Buffered(1))


def _sigmoid(x):
    return 1.0 / (1.0 + jnp.exp(-x))


def _modulated(x, modl_ref, modc_ref, k_shift, row0, n_lat):
    tm = x.shape[0]
    ms = jnp.mean(x * x, axis=-1, keepdims=True)
    xn = x * lax.rsqrt(ms + EPS)
    rows = row0 + lax.broadcasted_iota(jnp.int32, (tm, 1), 0)
    is_ctx = rows >= n_lat
    shift = jnp.where(is_ctx, modc_ref[0, k_shift:k_shift + 1, :], modl_ref[0, k_shift:k_shift + 1, :])
    scale = jnp.where(is_ctx, modc_ref[0, k_shift + 1:k_shift + 2, :], modl_ref[0, k_shift + 1:k_shift + 2, :])
    return xn * (1.0 + scale) + shift


def _gate_rows(modl_ref, modc_ref, k_gate, row0, tm, n_lat):
    rows = row0 + lax.broadcasted_iota(jnp.int32, (tm, 1), 0)
    return jnp.where(rows >= n_lat, modc_ref[0, k_gate:k_gate + 1, :], modl_ref[0, k_gate:k_gate + 1, :])


def _adaln_kernel(c_ref, w_ref, b_ref, o_ref):
    c = c_ref[...]
    s = c * _sigmoid(c)
    o_ref[0] = jnp.dot(s, w_ref[0], preferred_element_type=F32, precision=lax.Precision.HIGHEST) + b_ref[0]


def _adaln(cond_rows, w_mod, b_mod):
    depth, d, n = w_mod.shape
    tn = _pick_tile(n, (1536, 1024, 512))
    return pl.pallas_call(
        _adaln_kernel,
        grid=(depth, n // tn),
        in_specs=[
            _const_spec((MOD_ROWS, d)),
            pl.BlockSpec((1, d, tn), lambda l, j: (l, 0, j)),
            pl.BlockSpec((1, 1, tn), lambda l, j: (l, 0, j)),
        ],
        out_specs=pl.BlockSpec((1, MOD_ROWS, tn), lambda l, j: (l, 0, j)),
        out_shape=jax.ShapeDtypeStruct((depth, MOD_ROWS, n), F32),
        compiler_params=_cparams("parallel", "parallel"),
        name="adaln",
    )(cond_rows, w_mod, b_mod.reshape(depth, 1, n))


def _rope_dup(blk, tab):
    t = blk * tab
    return t + pltpu.roll(t, HEAD_DIM, axis=1)


def _inproj_even_kernel(x_ref, modl_ref, modc_ref, w_ref, tab_ref, o_ref, *, n_lat, tm):
    row0 = pl.program_id(1) * tm
    h = _modulated(x_ref[0], modl_ref, modc_ref, 0, row0, n_lat).astype(BF16)
    tab = tab_ref[...]
    lane = lax.broadcasted_iota(jnp.int32, (tm, LANES), 1)
    n_out = o_ref.shape[2]
    for c in range(n_out // 256):
        y = jnp.dot(h, w_ref[:, c * 256:(c + 1) * 256], preferred_element_type=F32)
        for half in range(2):
            blk = y[:, half * LANES:(half + 1) * LANES]
            col = c * 2 + half
            if 4 <= col < 8:
                blk = _rope_dup(blk, tab)
            elif 8 <= col < 12:
                blk = jnp.where(lane < HEAD_DIM, _rope_dup(blk, tab) * (HEAD_DIM ** -0.5), 0.0)
            o_ref[0, :, col * LANES:(col + 1) * LANES] = blk.astype(BF16)


def _inproj_odd_kernel(x_ref, modl_ref, modc_ref, w_ref, tab_ref, gq_ref, gk_ref, o_ref, *, n_lat, tm):
    row0 = pl.program_id(1) * tm
    h = _modulated(x_ref[0], modl_ref, modc_ref, 0, row0, n_lat).astype(BF16)
    tab = tab_ref[...]
    lane = lax.broadcasted_iota(jnp.int32, (tm, LANES), 1)
    n_qk = 2 * DIFF_HEADS
    for c in range(n_qk):
        y = jnp.dot(h, w_ref[:, c * 256:(c + 1) * 256], preferred_element_type=F32)
        gain = gq_ref[...] * (HEAD_DIM ** -0.5 * LOG2_E) if c < DIFF_HEADS else gk_ref[...]
        subs = []
        for half in range(2):
            blk = y[:, half * LANES:(half + 1) * LANES]
            ms = jnp.sum(blk * blk, axis=-1, keepdims=True) * (1.0 / LANES)
            subs.append(_rope_dup(blk * lax.rsqrt(ms + EPS) * gain, tab))
        o_ref[0, :, c * LANES:(c + 1) * LANES] = jnp.where(lane < HEAD_DIM, subs[0], subs[1]).astype(BF16)
    for c in range(DIFF_HEADS // 2):
        col0 = n_qk * 256 + c * 256
        y = jnp.dot(h, w_ref[:, col0:col0 + 256], preferred_element_type=F32)
        out0 = n_qk * LANES + c * 256
        o_ref[0, :, out0:out0 + 256] = y.astype(BF16)


def _inproj(x, mod, w, tab, gains, n_lat, n_out, even):
    b, t, d = x.shape
    tm = _pick_tile(t, (768, 256))
    kern = _inproj_even_kernel if even else _inproj_odd_kernel
    in_specs = [
        pl.BlockSpec((1, tm, d), lambda bi, i: (bi, i, 0)),
        pl.BlockSpec((1, N_MOD, d), lambda bi, i: (bi, 0, 0)),
        pl.BlockSpec((1, N_MOD, d), lambda bi, i: (b, 0, 0)),
        _resident_spec(w.shape),
        pl.BlockSpec((tm, LANES), lambda bi, i: (i, 0)),
    ] + [_const_spec((1, LANES))] * len(gains)
    return pl.pallas_call(
        functools.partial(kern, n_lat=n_lat, tm=tm),
        grid=(b, t // tm),
        in_specs=in_specs,
        out_specs=pl.BlockSpec((1, tm, n_out), lambda bi, i: (bi, i, 0)),
        out_shape=jax.ShapeDtypeStruct((b, t, n_out), BF16),
        compiler_params=_cparams("parallel", "parallel"),
        name="inproj_even" if even else "inproj_odd",
    )(x, mod, mod, w, tab, *gains)


def _dft_kernel(c_ref, s_ref, u_ref, cc_ref, sc_ref, o_ref, acc_c, acc_s, *, nb, scale):
    j = pl.program_id(1)

    @pl.when(j == 0)
    def _():
        acc_c[...] = jnp.zeros_like(acc_c)
        acc_s[...] = jnp.zeros_like(acc_s)

    cm = c_ref[...]
    sm = s_ref[...]
    for bi in range(nb):
        u = u_ref[bi]
        acc_c[bi] += jnp.dot(cm, u, preferred_element_type=F32)
        acc_s[bi] += jnp.dot(sm, u, preferred_element_type=F32)

    @pl.when(j == pl.num_programs(1) - 1)
    def _():
        cc = cc_ref[...]
        sc = sc_ref[...]
        for bi in range(nb):
            for g in range(FNET_GROUPS):
                sl = slice(g * FNET_GROUP_DIM, (g + 1) * FNET_GROUP_DIM)
                a = acc_c[bi, :, sl].astype(BF16)
                bm = acc_s[bi, :, sl].astype(BF16)
                y = jnp.dot(a, cc, preferred_element_type=F32) - jnp.dot(bm, sc, preferred_element_type=F32)
                o_ref[bi, :, sl] = (y * scale).astype(BF16)


def _angle_tables(n_rows, n_cols, stride, period):
    k = jnp.arange(n_rows, dtype=jnp.int32)[:, None]
    n = jnp.arange(n_cols, dtype=jnp.int32)[None, :]
    ang = ((k * n * stride) % period).astype(F32) * (2.0 * math.pi / period)
    return jnp.cos(ang), jnp.sin(ang)


def _dft_tables(n):
    if n <= LANES:
        c, s = _angle_tables(n, n, 1, n)
        return c.astype(BF16), s.astype(BF16)
    ca, sa = _angle_tables(n // LANES, n, LANES, n)
    cb, sb = _angle_tables(LANES, n, 1, n)
    ca, sa, cb, sb = ca[:, None, :], sa[:, None, :], cb[None, :, :], sb[None, :, :]
    cm = (ca * cb - sa * sb).astype(BF16).reshape(n, n)
    sm = (sa * cb + ca * sb).astype(BF16).reshape(n, n)
    return cm, sm


def _fourier_seq(hin, row0, n, chan_tabs):
    b = hin.shape[0]
    cm, sm = _dft_tables(n)
    cc, sc = chan_tabs
    tk = _pick_tile(n, (1024, 512, 256))
    tn = tk
    off = row0 // tn
    return pl.pallas_call(
        functools.partial(_dft_kernel, nb=b, scale=1.0 / math.sqrt(n * FNET_GROUP_DIM)),
        grid=(n // tk, n // tn),
        in_specs=[
            pl.BlockSpec((tk, tn), lambda i, j: (i, j)),
            pl.BlockSpec((tk, tn), lambda i, j: (i, j)),
            pl.BlockSpec((b, tn, FNET_WIDTH), lambda i, j: (0, off + j, 0)),
            _const_spec((FNET_GROUP_DIM, FNET_GROUP_DIM)),
            _const_spec((FNET_GROUP_DIM, FNET_GROUP_DIM)),
        ],
        out_specs=pl.BlockSpec((b, tk, FNET_WIDTH), lambda i, j: (0, i, 0)),
        out_shape=jax.ShapeDtypeStruct((b, n, FNET_WIDTH), BF16),
        scratch_shapes=[pltpu.VMEM((b, tk, FNET_WIDTH), F32), pltpu.VMEM((b, tk, FNET_WIDTH), F32)],
        compiler_params=_cparams("parallel", "arbitrary"),
        name=f"fourier_seq_{n}",
    )(cm, sm, hin, cc, sc)


def _log_sigmoid(x):
    return jnp.minimum(x, 0.0) - jnp.log(1.0 + jnp.exp(-jnp.abs(x)))


def _retention_kernel(q_ref, k_ref, v_ref, g_ref, decf_ref, decb_ref, o_ref,
                      of_scr, ob_scr, dec_scr, *, n_lat, n_ctx):
    c = RET_CHUNK
    pos_i = lax.broadcasted_iota(jnp.int32, (c, c), 0).astype(F32)
    pos_j = lax.broadcasted_iota(jnp.int32, (c, c), 1).astype(F32)
    lg_f = _log_sigmoid(decf_ref[0])
    lg_b = _log_sigmoid(decb_ref[0])
    rel = pos_i - pos_j
    mask_f = rel >= 0.0
    mask_b = rel < 0.0
    dec_scr[0] = jnp.where(mask_f, jnp.exp(lg_f * jnp.where(mask_f, rel, 0.0)), 0.0)
    dec_scr[1] = jnp.where(mask_b, jnp.exp(lg_b * jnp.where(mask_b, -rel, 0.0)), 0.0)
    dec_scr[2] = jnp.exp(lg_f * (pos_i + 1.0))
    dec_scr[3] = jnp.exp(lg_b * (c - pos_i))
    dec_scr[4] = jnp.exp(lg_f * (c - 1.0 - pos_i))
    dec_scr[5] = jnp.exp(lg_b * pos_i)
    dec_scr[6] = jnp.exp(lg_f * (c + 0.0 * pos_i))
    dec_scr[7] = jnp.exp(lg_b * (c + 0.0 * pos_i))
    def one_dir(d, start, out_scr, state):
        qc = q_ref[0, pl.ds(start, c), :]
        kc = k_ref[0, pl.ds(start, c), :]
        vc = v_ref[0, pl.ds(start, c), :]
        scores = lax.dot_general(qc, kc, (((1,), (1,)), ((), ())), preferred_element_type=F32)
        intra = jnp.dot((scores * dec_scr[d]).astype(BF16), vc, preferred_element_type=F32)
        inter = jnp.dot(qc, state.astype(BF16), preferred_element_type=F32)
        out_scr[pl.ds(start, c), :] = intra + dec_scr[2 + d] * inter
        vd = (vc.astype(F32) * dec_scr[4 + d]).astype(BF16)
        kv = lax.dot_general(kc, vd, (((0,), (0,)), ((), ())), preferred_element_type=F32)
        return dec_scr[6 + d] * state + kv

    s_f = jnp.zeros((LANES, LANES), F32)
    s_b = jnp.zeros((LANES, LANES), F32)
    n_ctx_chunks = n_ctx // c
    for ci in range(n_ctx_chunks):
        s_f = one_dir(0, n_lat + ci * c, of_scr, s_f)
        s_b = one_dir(1, n_lat + (n_ctx_chunks - 1 - ci) * c, ob_scr, s_b)

    n_lat_chunks = n_lat // c

    def body(i, states):
        s_f = one_dir(0, pl.multiple_of(i * c, c), of_scr, states[0])
        s_b = one_dir(1, pl.multiple_of((n_lat_chunks - 1 - i) * c, c), ob_scr, states[1])
        return s_f, s_b

    lax.fori_loop(0, n_lat_chunks, body, (s_f, s_b), unroll=_pick_tile(n_lat_chunks, (4, 2, 1)))

    def finish(i, carry):
        r0 = pl.multiple_of(i * c, c)
        o = of_scr[pl.ds(r0, c), :] + ob_scr[pl.ds(r0, c), :]
        o = o * lax.rsqrt(jnp.mean(o * o, axis=-1, keepdims=True) + EPS)
        g = g_ref[0, pl.ds(r0, c), :].astype(F32)
        o_ref[0, pl.ds(r0, c), :] = (o * (g * _sigmoid(g))).astype(BF16)
        return carry

    lax.fori_loop(0, (n_lat + n_ctx) // c, finish, 0)


def _retention(hin, dec_f, dec_b, n_lat):
    b, t, _ = hin.shape
    h = RET_HEADS
    seq = lambda col0: pl.BlockSpec((1, t, LANES), lambda bi, hi: (bi, 0, col0 + hi))
    dec = pl.BlockSpec((1, 1, LANES), lambda bi, hi: (hi, 0, 0))
    return pl.pallas_call(
        functools.partial(_retention_kernel, n_lat=n_lat, n_ctx=t - n_lat),
        grid=(b, h),
        in_specs=[seq(4), seq(8), seq(12), seq(16), dec, dec],
        out_specs=pl.BlockSpec((1, t, LANES), lambda bi, hi: (bi, 0, hi)),
        out_shape=jax.ShapeDtypeStruct((b, t, h * LANES), BF16),
        scratch_shapes=[
            pltpu.VMEM((t, LANES), F32),
            pltpu.VMEM((t, LANES), F32),
            pltpu.VMEM((8, RET_CHUNK, LANES), F32),
        ],
        compiler_params=_cparams("parallel", "parallel"),
        name="retention",
    )(hin, hin, hin, hin, dec_f, dec_b)


def _merge_kernel(*refs, n_parts, n_lat, tm):
    x_ref, modl_ref, modc_ref = refs[:3]
    a_refs = refs[3:3 + n_parts]
    w_refs = refs[3 + n_parts:3 + 2 * n_parts]
    o_ref = refs[3 + 2 * n_parts]
    row0 = pl.program_id(1) * tm
    y = jnp.dot(a_refs[0][0], w_refs[0][...], preferred_element_type=F32)
    for p in range(1, n_parts):
        y = y + jnp.dot(a_refs[p][0], w_refs[p][...], preferred_element_type=F32)
    gate = _gate_rows(modl_ref, modc_ref, 2, row0, tm, n_lat)
    o_ref[0] = x_ref[0] + gate * y


def _merge(x, mod, parts, weights, n_lat):
    b, t, d = x.shape
    tm = _pick_tile(t, (768, 256))
    n_parts = len(parts)
    in_specs = [
        pl.BlockSpec((1, tm, d), lambda bi, i: (bi, i, 0)),
        pl.BlockSpec((1, N_MOD, d), lambda bi, i: (bi, 0, 0)),
        pl.BlockSpec((1, N_MOD, d), lambda bi, i: (b, 0, 0)),
    ]
    in_specs += [pl.BlockSpec((1, tm, p.shape[2]), lambda bi, i: (bi, i, 0)) for p in parts]
    in_specs += [_resident_spec(w.shape) for w in weights]
    return pl.pallas_call(
        functools.partial(_merge_kernel, n_parts=n_parts, n_lat=n_lat, tm=tm),
        grid=(b, t // tm),
        in_specs=in_specs,
        out_specs=pl.BlockSpec((1, tm, d), lambda bi, i: (bi, i, 0)),
        out_shape=jax.ShapeDtypeStruct((b, t, d), F32),
        input_output_aliases={0: 0},
        compiler_params=_cparams("parallel", "parallel"),
        name=f"merge_{n_parts}",
    )(x, mod, mod, *parts, *weights)


def _attn_kernel(prev_ref, q_ref, k_ref, v_ref, lam_ref, sg_ref, o_ref, *, tk, lam_init):
    del prev_ref
    tq = q_ref.shape[1]
    n_keys = k_ref.shape[1]
    q = q_ref[0]
    lane = lax.broadcasted_iota(jnp.int32, (tq, LANES), 1)
    zero = jnp.zeros_like(q)
    q_sub = (jnp.where(lane < HEAD_DIM, q, zero), jnp.where(lane >= HEAD_DIM, q, zero))
    ones = jnp.ones((tk, LANES), BF16)
    m = [None, None]
    acc = [None, None]
    for j in range(n_keys // tk):
        kc = k_ref[0, j * tk:(j + 1) * tk, :]
        v1 = jnp.concatenate([v_ref[0, j * tk:(j + 1) * tk, :], ones], axis=1)
        for sub in range(2):
            s = lax.dot_general(q_sub[sub], kc, (((1,), (1,)), ((), ())), preferred_element_type=F32)
            mx = jnp.max(s, axis=-1, keepdims=True)
            if j == 0:
                m[sub] = mx
                p = jnp.exp2(s - mx).astype(BF16)
                acc[sub] = jnp.dot(p, v1, preferred_element_type=F32)
            else:
                m_new = jnp.maximum(m[sub], mx)
                alpha = jnp.exp2(m[sub] - m_new)
                p = jnp.exp2(s - m_new).astype(BF16)
                acc[sub] = alpha * acc[sub] + jnp.dot(p, v1, preferred_element_type=F32)
                m[sub] = m_new

    lv = lam_ref[...]
    s1 = jnp.sum(lv[0:1] * lv[1:2], axis=-1, keepdims=True)
    s2 = jnp.sum(lv[2:3] * lv[3:4], axis=-1, keepdims=True)
    lam = jnp.exp(s1) - jnp.exp(s2) + lam_init
    o = acc[0][:, :LANES] / acc[0][:, LANES:] - lam * (acc[1][:, :LANES] / acc[1][:, LANES:])
    o = o * lax.rsqrt(jnp.mean(o * o, axis=-1, keepdims=True) + EPS)
    o_ref[0] = (o * sg_ref[...] * (1.0 - lam_init)).astype(BF16)


def _attention(out, qkv, lam_vecs, sub_gain, lam_init, n_lat, ctx_queries):
    b, t, _ = qkv.shape
    h = DIFF_HEADS
    n_ctx = t - n_lat
    if ctx_queries:
        tq, n_q_tiles, q_off = n_ctx, 1, n_lat // n_ctx
        keys, k_off = n_ctx, n_lat // n_ctx
        tk = n_ctx
    else:
        tq = _pick_tile(n_lat, (1024, 512, 256))
        n_q_tiles, q_off = n_lat // tq, 0
        keys, k_off = t, 0
        tk = _pick_tile(t, (768, 256))
    return pl.pallas_call(
        functools.partial(_attn_kernel, tk=tk, lam_init=lam_init),
        grid=(b, h, n_q_tiles),
        in_specs=[
            pl.BlockSpec(memory_space=pl.ANY),
            pl.BlockSpec((1, tq, LANES), lambda bi, hi, i: (bi, q_off + i, hi)),
            pl.BlockSpec((1, keys, LANES), lambda bi, hi, i: (bi, k_off, h + hi)),
            pl.BlockSpec((1, keys, LANES), lambda bi, hi, i: (bi, k_off, 2 * h + hi)),
            _const_spec((4, LANES)),
            _const_spec((1, LANES)),
        ],
        out_specs=pl.BlockSpec((1, tq, LANES), lambda bi, hi, i: (bi, q_off + i, hi)),
        out_shape=jax.ShapeDtypeStruct((b, t, h * LANES), BF16),
        input_output_aliases={0: 0},
        compiler_params=_cparams("parallel", "parallel", "arbitrary"),
        name="diff_attn_ctx" if ctx_queries else "diff_attn_lat",
    )(out, qkv, qkv, qkv, lam_vecs, sub_gain)


def _ffn_kernel(x_ref, modl_ref, modc_ref, w1_ref, w3_ref, w2_ref, o_ref, *, n_lat, tm, tf):
    row0 = pl.program_id(1) * tm
    x = x_ref[0]
    h = _modulated(x, modl_ref, modc_ref, 3, row0, n_lat).astype(BF16)
    f_dim = w1_ref.shape[1]
    y = jnp.zeros((tm, x.shape[1]), F32)
    for c in range(f_dim // tf):
        sl = slice(c * tf, (c + 1) * tf)
        a = jnp.dot(h, w1_ref[:, sl], preferred_element_type=F32)
        g = jnp.dot(h, w3_ref[:, sl], preferred_element_type=F32)
        u = (a * _sigmoid(a) * g).astype(BF16)
        y = y + jnp.dot(u, w2_ref[sl, :], preferred_element_type=F32)
    gate = _gate_rows(modl_ref, modc_ref, 5, row0, tm, n_lat)
    o_ref[0] = x + gate * y


def _ffn(x, mod, w1, w3, w2, n_lat):
    b, t, d = x.shape
    tm = _pick_tile(t, (768, 256))
    tf = _pick_tile(w1.shape[1], (256, 128))
    return pl.pallas_call(
        functools.partial(_ffn_kernel, n_lat=n_lat, tm=tm, tf=tf),
        grid=(b, t // tm),
        in_specs=[
            pl.BlockSpec((1, tm, d), lambda bi, i: (bi, i, 0)),
            pl.BlockSpec((1, N_MOD, d), lambda bi, i: (bi, 0, 0)),
            pl.BlockSpec((1, N_MOD, d), lambda bi, i: (b, 0, 0)),
            _resident_spec(w1.shape),
            _resident_spec(w3.shape),
            _resident_spec(w2.shape),
        ],
        out_specs=pl.BlockSpec((1, tm, d), lambda bi, i: (bi, i, 0)),
        out_shape=jax.ShapeDtypeStruct((b, t, d), F32),
        input_output_aliases={0: 0},
        compiler_params=_cparams("parallel", "parallel"),
        name="ffn_swiglu",
    )(x, mod, mod, w1, w3, w2)


ROUTE_E1, ROUTE_E2, ROUTE_G1, ROUTE_G2 = 0, 1, 2, 3
SLABS = D_MODEL // LANES
DMA_LOOP_UNROLL = 8


def _top2_route(logits):
    tm = logits.shape[0]
    lane = lax.broadcasted_iota(jnp.int32, (tm, LANES), 1)
    neg = jnp.float32(-jnp.inf)
    lg = jnp.where(lane < N_EXPERTS, logits, neg)
    m1 = jnp.max(lg, axis=-1, keepdims=True)
    i1 = jnp.min(jnp.where(lg == m1, lane, LANES), axis=-1, keepdims=True)
    lg2 = jnp.where(lane == i1, neg, lg)
    m2 = jnp.max(lg2, axis=-1, keepdims=True)
    i2 = jnp.min(jnp.where(lg2 == m2, lane, LANES), axis=-1, keepdims=True)
    e = jnp.exp(m2 - m1)
    g1 = 1.0 / (1.0 + e)
    g2 = e / (1.0 + e)
    rec = jnp.where(lane == ROUTE_E1, i1.astype(F32), 0.0) + jnp.where(lane == ROUTE_E2, i2.astype(F32), 0.0)
    return rec + jnp.where(lane == ROUTE_G1, g1, 0.0) + jnp.where(lane == ROUTE_G2, g2, 0.0)


def _slab_store(ref, val, n_rows):
    for s in range(SLABS):
        ref[pl.ds(s, n_rows, stride=SLABS), :] = val[:, s * LANES:(s + 1) * LANES]


def _slab_load(ref, n_rows):
    return jnp.concatenate([ref[pl.ds(s, n_rows, stride=SLABS), :] for s in range(SLABS)], axis=1)


def _route_kernel(x_ref, modl_ref, modc_ref, wr_ref, h_ref, rec_ref, *, n_lat, tm):
    row0 = pl.program_id(1) * tm
    hf = _modulated(x_ref[0], modl_ref, modc_ref, 3, row0, n_lat)
    logits = jnp.dot(hf, wr_ref[...], preferred_element_type=F32, precision=lax.Precision.HIGHEST)
    rec_ref[0] = _top2_route(logits)
    _slab_store(h_ref, hf, tm)


def _moe_route(x, mod, w_router, n_lat):
    b, t, d = x.shape
    tm = _pick_tile(t, (768, 256))
    tiles = t // tm
    return pl.pallas_call(
        functools.partial(_route_kernel, n_lat=n_lat, tm=tm),
        grid=(b, tiles),
        in_specs=[
            pl.BlockSpec((1, tm, d), lambda bi, i: (bi, i, 0)),
            pl.BlockSpec((1, N_MOD, d), lambda bi, i: (bi, 0, 0)),
            pl.BlockSpec((1, N_MOD, d), lambda bi, i: (b, 0, 0)),
            _const_spec((d, LANES)),
        ],
        out_specs=[
            pl.BlockSpec((tm * SLABS, LANES), lambda bi, i: (bi * tiles + i, 0)),
            pl.BlockSpec((1, tm, LANES), lambda bi, i: (bi, i, 0)),
        ],
        out_shape=[
            jax.ShapeDtypeStruct((b * t * SLABS, LANES), F32),
            jax.ShapeDtypeStruct((b, t, LANES), F32),
        ],
        compiler_params=_cparams("parallel", "parallel"),
        name="moe_route",
    )(x, mod, mod, w_router)


def _row_copy(src_ref, src_row, dst_ref, dst_row, sem):
    return pltpu.make_async_copy(
        src_ref.at[pl.ds(pl.multiple_of(src_row * SLABS, SLABS), SLABS), :],
        dst_ref.at[pl.ds(pl.multiple_of(dst_row * SLABS, SLABS), SLABS), :],
        sem)


def _dispatch_kernel(dest_ref, h_ref, init_ref, xs_ref, sem, *, tm):
    del init_ref

    def start(r, carry):
        for slot in range(2):
            _row_copy(h_ref, r, xs_ref, dest_ref[0, 0, 2 * r + slot], sem).start()
        return carry

    def wait(r, carry):
        for slot in range(2):
            _row_copy(h_ref, r, xs_ref, dest_ref[0, 0, 2 * r + slot], sem).wait()
        return carry

    lax.fori_loop(0, tm, start, 0, unroll=DMA_LOOP_UNROLL)
    lax.fori_loop(0, tm, wait, 0, unroll=DMA_LOOP_UNROLL)


def _moe_dispatch(h_slabs, dest, n_sorted_rows):
    n_tok = h_slabs.shape[0] // SLABS
    tm = _pick_tile(n_tok, (512, 256))
    steps = n_tok // tm
    return pl.pallas_call(
        functools.partial(_dispatch_kernel, tm=tm),
        grid=(steps,),
        in_specs=[
            pl.BlockSpec((1, 1, 2 * tm), lambda i: (i, 0, 0), memory_space=pltpu.SMEM),
            pl.BlockSpec((tm * SLABS, LANES), lambda i: (i, 0)),
            pl.BlockSpec(memory_space=pl.ANY),
        ],
        out_specs=pl.BlockSpec(memory_space=pl.ANY),
        out_shape=jax.ShapeDtypeStruct((n_sorted_rows * SLABS, LANES), F32),
        scratch_shapes=[pltpu.SemaphoreType.DMA(())],
        input_output_aliases={2: 0},
        compiler_params=_cparams("arbitrary"),
        name="moe_dispatch",
    )(dest.reshape(steps, 1, 2 * tm), h_slabs, jnp.zeros((n_sorted_rows * SLABS, LANES), F32))


def _experts_kernel(te_ref, used_ref, x_ref, w1_ref, w3_ref, w2_ref, o_ref, h_scr, acc_scr, *, tm, tf):
    del te_ref
    i = pl.program_id(0)
    f = pl.program_id(1)
    valid = i < used_ref[0]

    @pl.when(f == 0)
    def _():
        acc_scr[...] = jnp.zeros_like(acc_scr)

    @pl.when(valid & (f == 0))
    def _():
        h_scr[...] = _slab_load(x_ref, tm).astype(BF16)

    @pl.when(valid)
    def _():
        h = h_scr[...]
        f_blk = w1_ref.shape[2]
        y = jnp.zeros(acc_scr.shape, F32)
        for c in range(f_blk // tf):
            sl = slice(c * tf, (c + 1) * tf)
            a = jnp.dot(h, w1_ref[0, :, sl], preferred_element_type=F32)
            g = jnp.dot(h, w3_ref[0, :, sl], preferred_element_type=F32)
            u = (a * _sigmoid(a) * g).astype(BF16)
            y = y + jnp.dot(u, w2_ref[0, sl, :], preferred_element_type=F32)
        acc_scr[...] += y

    @pl.when(f == pl.num_programs(1) - 1)
    def _():
        _slab_store(o_ref, acc_scr[...], tm)


def _moe_experts(xs, tile_expert, n_used, w1, w3, w2, tm):
    n_exp, d, f_dim = w1.shape
    n_tiles = xs.shape[0] // (tm * SLABS)
    f_blk = _pick_tile(f_dim, (1792, 512, 256))
    tf = _pick_tile(f_blk, (256, 128))
    n_f = f_dim // f_blk

    def f_eff(i, f, used):
        return jnp.where(i < used[0], f, n_f - 1)

    grid_spec = pltpu.PrefetchScalarGridSpec(
        num_scalar_prefetch=2,
        grid=(n_tiles, n_f),
        in_specs=[
            pl.BlockSpec((tm * SLABS, LANES), lambda i, f, te, used: (jnp.minimum(i, used[0] - 1), 0)),
            pl.BlockSpec((1, d, f_blk), lambda i, f, te, used: (te[i], 0, f_eff(i, f, used))),
            pl.BlockSpec((1, d, f_blk), lambda i, f, te, used: (te[i], 0, f_eff(i, f, used))),
            pl.BlockSpec((1, f_blk, d), lambda i, f, te, used: (te[i], f_eff(i, f, used), 0)),
        ],
        out_specs=pl.BlockSpec((tm * SLABS, LANES), lambda i, f, te, used: (i, 0)),
        scratch_shapes=[pltpu.VMEM((tm, d), BF16), pltpu.VMEM((tm, d), F32)],
    )
    return pl.pallas_call(
        functools.partial(_experts_kernel, tm=tm, tf=tf),
        grid_spec=grid_spec,
        out_shape=jax.ShapeDtypeStruct(xs.shape, F32),
        compiler_params=_cparams("arbitrary", "arbitrary"),
        name="moe_experts",
    )(tile_expert, n_used, xs, w1, w3, w2)


def _combine_kernel(dcur_ref, dnext_ref, x_ref, modl_ref, modc_ref, rec_ref, ys_ref, o_ref, buf, sems,
                    *, n_lat, tm, tiles_per_batch):
    i = pl.program_id(0)
    n = pl.num_programs(0)

    def gather(dest_ref, slot_buf, op):
        def body(r, carry):
            for slot in range(2):
                cp = _row_copy(ys_ref, dest_ref[0, 0, 2 * r + slot], buf.at[slot_buf, slot], r, sems.at[slot_buf])
                cp.start() if op == "start" else cp.wait()
            return carry
        lax.fori_loop(0, tm, body, 0, unroll=DMA_LOOP_UNROLL)

    @pl.when(i == 0)
    def _():
        gather(dcur_ref, 0, "start")

    for parity in range(2):
        @pl.when((i + 1 < n) & ((i + 1) % 2 == parity))
        def _():
            gather(dnext_ref, parity, "start")

    for parity in range(2):
        @pl.when(i % 2 == parity)
        def _():
            gather(dcur_ref, parity, "wait")
            rec = rec_ref[0]
            y = (rec[:, ROUTE_G1:ROUTE_G1 + 1] * _slab_load(buf.at[parity, 0], tm)
                 + rec[:, ROUTE_G2:ROUTE_G2 + 1] * _slab_load(buf.at[parity, 1], tm))
            row0 = (i % tiles_per_batch) * tm
            gate = _gate_rows(modl_ref, modc_ref, 5, row0, tm, n_lat)
            o_ref[0] = x_ref[0] + gate * y


def _moe_combine(x, mod, rec, ys, dest, n_lat):
    b, t, d = x.shape
    tm = _pick_tile(t, (256,))
    tpb = t // tm
    n_tiles = b * tpb
    dest3 = dest.reshape(n_tiles, 1, 2 * tm)
    tok = lambda i: (i // tpb, i % tpb, 0)
    return pl.pallas_call(
        functools.partial(_combine_kernel, n_lat=n_lat, tm=tm, tiles_per_batch=tpb),
        grid=(n_tiles,),
        in_specs=[
            pl.BlockSpec((1, 1, 2 * tm), lambda i: (i, 0, 0), memory_space=pltpu.SMEM),
            pl.BlockSpec((1, 1, 2 * tm), lambda i: (jnp.minimum(i + 1, n_tiles - 1), 0, 0),
                         memory_space=pltpu.SMEM),
            pl.BlockSpec((1, tm, d), tok),
            pl.BlockSpec((1, N_MOD, d), lambda i: (i // tpb, 0, 0)),
            pl.BlockSpec((1, N_MOD, d), lambda i: (b, 0, 0)),
            pl.BlockSpec((1, tm, LANES), tok),
            pl.BlockSpec(memory_space=pl.ANY),
        ],
        out_specs=pl.BlockSpec((1, tm, d), tok),
        out_shape=jax.ShapeDtypeStruct((b, t, d), F32),
        scratch_shapes=[pltpu.VMEM((2, 2, tm * SLABS, LANES), F32), pltpu.SemaphoreType.DMA((2,))],
        input_output_aliases={2: 0},
        compiler_params=_cparams("arbitrary"),
        name="moe_combine",
    )(dest3, dest3, x, mod, mod, rec, ys)


def _dispatch_plan(rec, tm, n_tiles):
    experts = rec[:, ROUTE_E1:ROUTE_E2 + 1].astype(jnp.int32).reshape(-1)
    onehot = (experts[:, None] == jnp.arange(N_EXPERTS, dtype=jnp.int32)[None, :]).astype(jnp.int32)
    csum = jnp.cumsum(onehot, axis=0)
    counts = csum[-1]
    padded = (counts + tm - 1) // tm * tm
    ends = jnp.cumsum(padded)
    dest = jnp.sum(onehot * (csum - 1 + (ends - padded)[None, :]), axis=1)
    n_used = (ends[-1] // tm).astype(jnp.int32)
    tile_start = jnp.arange(n_tiles, dtype=jnp.int32) * tm
    tile_expert = jnp.sum((tile_start[:, None] >= ends[None, :]).astype(jnp.int32), axis=1)
    tile_expert = jnp.minimum(tile_expert, tile_expert[n_used - 1])
    return dest.astype(jnp.int32), tile_expert.astype(jnp.int32), n_used.reshape(1)


def _moe(x, mod, w_router, w1, w3, w2, first_expert, n_lat):
    b, t, d = x.shape
    n_tok = b * t
    tm = 512
    n_tiles = (2 * n_tok + N_EXPERTS * (tm - 1)) // tm
    h_slabs, rec = _moe_route(x, mod, w_router, n_lat)
    dest, tile_expert, n_used = _dispatch_plan(rec.reshape(n_tok, LANES), tm, n_tiles)
    xs = _moe_dispatch(h_slabs, dest, n_tiles * tm)
    ys = _moe_experts(xs, tile_expert + first_expert, n_used, w1, w3, w2, tm)
    return _moe_combine(x, mod, rec, ys, dest, n_lat)


def _rot_cols(w):
    half = HEAD_DIM // 2
    return jnp.concatenate([-w[..., half:], w[..., :half]], axis=-1)


def _with_rot(w_heads):
    d, n, _ = w_heads.shape
    return jnp.concatenate([w_heads, _rot_cols(w_heads)], axis=-1).reshape(d, n * LANES)


def _gain_lanes(gain):
    half = HEAD_DIM // 2
    perm = jnp.concatenate([gain[half:], gain[:half]])
    return jnp.concatenate([gain, perm]).reshape(1, LANES).astype(F32)


def _rope_table(n_lat, n_ctx):
    rows = n_lat // GRID_W
    row = jnp.repeat(jnp.arange(rows, dtype=F32), GRID_W)
    col = jnp.tile(jnp.arange(GRID_W, dtype=F32), rows)
    quarter = HEAD_DIM // 4
    inv_freq = ROPE_BASE ** (-jnp.arange(quarter, dtype=F32) / quarter)
    ang = jnp.concatenate([row[:, None] * inv_freq, col[:, None] * inv_freq], axis=-1)
    cos, sin = jnp.cos(ang), jnp.sin(ang)
    lat = jnp.concatenate([cos, cos, sin, sin], axis=-1)
    ctx = jnp.concatenate([jnp.ones((n_ctx, HEAD_DIM), F32), jnp.zeros((n_ctx, HEAD_DIM), F32)], axis=-1)
    return jnp.concatenate([lat, ctx], axis=0)


def _chan_tables():
    idx = jnp.arange(FNET_GROUP_DIM, dtype=jnp.int32)
    ang = ((idx[:, None] * idx[None, :]) % FNET_GROUP_DIM).astype(F32) * (2.0 * math.pi / FNET_GROUP_DIM)
    return jnp.cos(ang).astype(BF16), jnp.sin(ang).astype(BF16)


def kernel(x, c, ctx, c_ctx, w_mod, b_mod, w_in_even, w_out_even, ret_decay_fwd, ret_decay_bwd,
           ffn_w1, ffn_w3, ffn_w2, w_in_odd, w_out_odd, q_norm_gain, k_norm_gain,
           lambda_q1, lambda_k1, lambda_q2, lambda_k2, subln_gain, w_router, moe_w1, moe_w3, moe_w2):
    b, n_lat, d = x.shape
    n_ctx = ctx.shape[1]
    depth = w_mod.shape[0]
    assert d == D_MODEL and b < MOD_ROWS and n_lat % n_ctx == 0 and n_ctx % RET_CHUNK == 0

    stream = jnp.concatenate([x, ctx], axis=1)
    cond = jnp.zeros((MOD_ROWS, d), F32).at[:b].set(c).at[b].set(c_ctx)
    mods = _adaln(cond, w_mod, b_mod).reshape(depth, MOD_ROWS, N_MOD, d)
    tab = _rope_table(n_lat, n_ctx)
    chan_tabs = _chan_tables()
    stack_experts = lambda w: w.astype(BF16).reshape((-1,) + w.shape[2:])
    moe_w1_all, moe_w3_all, moe_w2_all = stack_experts(moe_w1), stack_experts(moe_w3), stack_experts(moe_w2)

    for layer in range(depth):
        i = layer // 2
        mod = mods[layer]
        if layer % 2 == 0:
            w = w_in_even[i]
            hq = RET_HEADS * HEAD_DIM
            wq = w[:, FNET_WIDTH:FNET_WIDTH + hq].reshape(d, RET_HEADS, HEAD_DIM)
            wk = w[:, FNET_WIDTH + hq:FNET_WIDTH + 2 * hq].reshape(d, RET_HEADS, HEAD_DIM)
            w_in = jnp.concatenate(
                [w[:, :FNET_WIDTH], _with_rot(wq), _with_rot(wk), w[:, FNET_WIDTH + 2 * hq:]], axis=1).astype(BF16)
            hin = _inproj(stream, mod, w_in, tab, (), n_lat, w_in.shape[1], True)
            four = jnp.concatenate(
                [_fourier_seq(hin, 0, n_lat, chan_tabs), _fourier_seq(hin, n_lat, n_ctx, chan_tabs)], axis=1)
            dec_f = jnp.broadcast_to(ret_decay_fwd[i].astype(F32)[:, None, None], (RET_HEADS, 1, LANES))
            dec_b = jnp.broadcast_to(ret_decay_bwd[i].astype(F32)[:, None, None], (RET_HEADS, 1, LANES))
            ret = _retention(hin, dec_f, dec_b, n_lat)
            w_out = w_out_even[i].astype(BF16)
            stream = _merge(stream, mod, [four, ret], [w_out[:FNET_WIDTH], w_out[FNET_WIDTH:]], n_lat)
            stream = _ffn(stream, mod, ffn_w1[i].astype(BF16), ffn_w3[i].astype(BF16),
                          ffn_w2[i].astype(BF16), n_lat)
        else:
            w = w_in_odd[i]
            n_sub = 2 * DIFF_HEADS
            wq = w[:, :n_sub * HEAD_DIM].reshape(d, n_sub, HEAD_DIM)
            wk = w[:, n_sub * HEAD_DIM:2 * n_sub * HEAD_DIM].reshape(d, n_sub, HEAD_DIM)
            w_in = jnp.concatenate([_with_rot(wq), _with_rot(wk), w[:, 2 * n_sub * HEAD_DIM:]], axis=1).astype(BF16)
            gains = (_gain_lanes(q_norm_gain[i]), _gain_lanes(k_norm_gain[i]))
            qkv = _inproj(stream, mod, w_in, tab, gains, n_lat, 3 * DIFF_HEADS * LANES, False)
            lam_init = 0.8 - 0.6 * float(np.exp(-0.3 * layer))
            lam_vecs = jnp.zeros((4, LANES), F32).at[:, :HEAD_DIM].set(
                jnp.stack([lambda_q1[i], lambda_k1[i], lambda_q2[i], lambda_k2[i]]).astype(F32))
            sub_gain = subln_gain[i].astype(F32).reshape(1, LANES)
            att = jnp.zeros((b, n_lat + n_ctx, DIFF_HEADS * LANES), BF16)
            att = _attention(att, qkv, lam_vecs, sub_gain, lam_init, n_lat, False)
            att = _attention(att, qkv, lam_vecs, sub_gain, lam_init, n_lat, True)
            stream = _merge(stream, mod, [att], [w_out_odd[i].astype(BF16)], n_lat)
            wr = jnp.zeros((d, LANES), F32).at[:, :N_EXPERTS].set(w_router[i].astype(F32))
            stream = _moe(stream, mod, wr, moe_w1_all, moe_w3_all, moe_w2_all, i * N_EXPERTS, n_lat)
    return stream[:, :n_lat]
```

```python
import functools
import math

import jax
import jax.numpy as jnp
import numpy as np
from jax import lax
from jax.experimental import pallas as pl
from jax.experimental.pallas import tpu as pltpu

F32 = jnp.float32
BF16 = jnp.bfloat16

D_MODEL = 1024
GRID_W = 64
HEAD_DIM = 64
LANES = 128
FNET_GROUPS = 4
FNET_GROUP_DIM = 128
FNET_WIDTH = FNET_GROUPS * FNET_GROUP_DIM
RET_HEADS = 4
RET_CHUNK = 128
DIFF_HEADS = 8
N_EXPERTS = 8
N_MOD = 6
ROPE_BASE = 10000.0
EPS = 1e-6
LOG2_E = math.log2(math.e)
VMEM_LIMIT_BYTES = 56 * 1024 * 1024
MOD_ROWS = 8


def _cparams(*sem):
    return pltpu.CompilerParams(dimension_semantics=sem, vmem_limit_bytes=VMEM_LIMIT_BYTES)


def _pick_tile(n, candidates):
    for c in candidates:
        if n % c == 0:
            return c
    raise ValueError(f"no tile in {candidates} divides {n}")


def _const_spec(shape):
    nd = len(shape)
    return pl.BlockSpec(shape, lambda *_: (0,) * nd)


def _resident_spec(shape):
    nd = len(shape)
    return pl.BlockSpec(shape, lambda *_: (0,) * nd, pipeline_mode=pl.Buffered(1))


def _sigmoid(x):
    return 1.0 / (1.0 + jnp.exp(-x))


def _modulated(x, modl_ref, modc_ref, k_shift, row0, n_lat):
    tm = x.shape[0]
    ms = jnp.mean(x * x, axis=-1, keepdims=True)
    xn = x * lax.rsqrt(ms + EPS)
    rows = row0 + lax.broadcasted_iota(jnp.int32, (tm, 1), 0)
    is_ctx = rows >= n_lat
    shift = jnp.where(is_ctx, modc_ref[0, k_shift:k_shift + 1, :], modl_ref[0, k_shift:k_shift + 1, :])
    scale = jnp.where(is_ctx, modc_ref[0, k_shift + 1:k_shift + 2, :], modl_ref[0, k_shift + 1:k_shift + 2, :])
    return xn * (1.0 + scale) + shift


def _gate_rows(modl_ref, modc_ref, k_gate, row0, tm, n_lat):
    rows = row0 + lax.broadcasted_iota(jnp.int32, (tm, 1), 0)
    return jnp.where(rows >= n_lat, modc_ref[0, k_gate:k_gate + 1, :], modl_ref[0, k_gate:k_gate + 1, :])


def _adaln_kernel(c_ref, w_ref, b_ref, o_ref):
    c = c_ref[...]
    s = c * _sigmoid(c)
    o_ref[0] = jnp.dot(s, w_ref[0], preferred_element_type=F32, precision=lax.Precision.HIGHEST) + b_ref[0]


def _adaln(cond_rows, w_mod, b_mod):
    depth, d, n = w_mod.shape
    tn = _pick_tile(n, (1536, 1024, 512))
    return pl.pallas_call(
        _adaln_kernel,
        grid=(depth, n // tn),
        in_specs=[
            _const_spec((MOD_ROWS, d)),
            pl.BlockSpec((1, d, tn), lambda l, j: (l, 0, j)),
            pl.BlockSpec((1, 1, tn), lambda l, j: (l, 0, j)),
        ],
        out_specs=pl.BlockSpec((1, MOD_ROWS, tn), lambda l, j: (l, 0, j)),
        out_shape=jax.ShapeDtypeStruct((depth, MOD_ROWS, n), F32),
        compiler_params=_cparams("parallel", "parallel"),
        name="adaln",
    )(cond_rows, w_mod, b_mod.reshape(depth, 1, n))


def _rope_dup(blk, tab):
    t = blk * tab
    return t + pltpu.roll(t, HEAD_DIM, axis=1)


def _inproj_even_kernel(x_ref, modl_ref, modc_ref, w_ref, tab_ref, o_ref, *, n_lat, tm):
    row0 = pl.program_id(1) * tm
    h = _modulated(x_ref[0], modl_ref, modc_ref, 0, row0, n_lat).astype(BF16)
    tab = tab_ref[...]
    lane = lax.broadcasted_iota(jnp.int32, (tm, LANES), 1)
    n_out = o_ref.shape[2]
    for c in range(n_out // 256):
        y = jnp.dot(h, w_ref[:, c * 256:(c + 1) * 256], preferred_element_type=F32)
        for half in range(2):
            blk = y[:, half * LANES:(half + 1) * LANES]
            col = c * 2 + half
            if 4 <= col < 8:
                blk = _rope_dup(blk, tab)
            elif 8 <= col < 12:
                blk = jnp.where(lane < HEAD_DIM, _rope_dup(blk, tab) * (HEAD_DIM ** -0.5), 0.0)
            o_ref[0, :, col * LANES:(col + 1) * LANES] = blk.astype(BF16)


def _swap_halves(y, first_half):
    fwd = pltpu.roll(y, HEAD_DIM // 2, axis=1)
    bwd = pltpu.roll(y, LANES - HEAD_DIM // 2, axis=1)
    return jnp.where(first_half, bwd, fwd)


def _inproj_odd_kernel(x_ref, modl_ref, modc_ref, w_ref, cos_ref, sin_ref, gq_ref, gk_ref, seg_ref, o_ref,
                       *, n_lat, tm):
    row0 = pl.program_id(1) * tm
    h = _modulated(x_ref[0], modl_ref, modc_ref, 0, row0, n_lat).astype(BF16)
    cos = cos_ref[...]
    sin = sin_ref[...]
    seg = seg_ref[...]
    lane = lax.broadcasted_iota(jnp.int32, (tm, LANES), 1)
    first_half = lane % HEAD_DIM < HEAD_DIM // 2
    gains = (gq_ref[...] * (HEAD_DIM ** -0.5 * LOG2_E), gk_ref[...])
    for c in range(w_ref.shape[1] // 256):
        y = jnp.dot(h, w_ref[:, c * 256:(c + 1) * 256], preferred_element_type=F32)
        part = c // (DIFF_HEADS // 2)
        if part < 2:
            for half in range(2):
                blk = y[:, half * LANES:(half + 1) * LANES]
                ms = jnp.dot((blk * blk).astype(BF16), seg, preferred_element_type=F32) * (1.0 / HEAD_DIM)
                yn = blk * lax.rsqrt(ms + EPS) * gains[part]
                out = yn * cos + _swap_halves(yn, first_half) * sin
                o_ref[0, :, (2 * c + half) * LANES:(2 * c + half + 1) * LANES] = out.astype(BF16)
        else:
            o_ref[0, :, c * 256:(c + 1) * 256] = y.astype(BF16)


def _inproj(x, mod, w, row_tabs, consts, n_lat, n_out, even):
    b, t, d = x.shape
    tm = _pick_tile(t, (768, 256))
    kern = _inproj_even_kernel if even else _inproj_odd_kernel
    in_specs = [
        pl.BlockSpec((1, tm, d), lambda bi, i: (bi, i, 0)),
        pl.BlockSpec((1, N_MOD, d), lambda bi, i: (bi, 0, 0)),
        pl.BlockSpec((1, N_MOD, d), lambda bi, i: (b, 0, 0)),
        _resident_spec(w.shape),
    ]
    in_specs += [pl.BlockSpec((tm, LANES), lambda bi, i: (i, 0)) for _ in row_tabs]
    in_specs += [_const_spec(c.shape) for c in consts]
    return pl.pallas_call(
        functools.partial(kern, n_lat=n_lat, tm=tm),
        grid=(b, t // tm),
        in_specs=in_specs,
        out_specs=pl.BlockSpec((1, tm, n_out), lambda bi, i: (bi, i, 0)),
        out_shape=jax.ShapeDtypeStruct((b, t, n_out), BF16),
        compiler_params=_cparams("parallel", "parallel"),
        name="inproj_even" if even else "inproj_odd",
    )(x, mod, mod, w, *row_tabs, *consts)


def _fold_kernel(ud_ref, ua_ref, ub_ref, p_ref, m_ref, *, nb, tf):
    j = pl.program_id(0)
    row = lax.broadcasted_iota(jnp.int32, (tf, tf), 0)
    col = lax.broadcasted_iota(jnp.int32, (tf, tf), 1)
    flip = ((row + col == tf) & (row >= 1)).astype(BF16)
    corner = ((row == 0) & (col == 0) & (j > 0)).astype(BF16)
    for bi in range(nb):
        mirrored = (jnp.dot(flip, ua_ref[bi], preferred_element_type=F32)
                    + jnp.dot(corner, ub_ref[bi], preferred_element_type=F32))
        ud = ud_ref[bi].astype(F32)
        p_ref[bi] = (ud + mirrored).astype(BF16)
        m_ref[bi] = (ud - mirrored).astype(BF16)


def _dft_kernel(c_ref, s_ref, p_ref, m_ref, mid_ref, cc_ref, sc_ref, o_ref, acc_c, acc_s, *, nb, tk, scale):
    i = pl.program_id(0)
    j = pl.program_id(1)

    @pl.when(j == 0)
    def _():
        acc_c[...] = jnp.zeros_like(acc_c)
        acc_s[...] = jnp.zeros_like(acc_s)

    cm = c_ref[...]
    sm = s_ref[...]
    for bi in range(nb):
        acc_c[bi] += jnp.dot(cm, p_ref[bi], preferred_element_type=F32)
        acc_s[bi] += jnp.dot(sm, m_ref[bi], preferred_element_type=F32)

    @pl.when(j == pl.num_programs(1) - 1)
    def _():
        cc = cc_ref[...]
        sc = sc_ref[...]
        k = i * tk + lax.broadcasted_iota(jnp.int32, (tk, 1), 0)
        sign = (1 - 2 * (k & 1)).astype(F32)
        for bi in range(nb):
            mid = mid_ref[bi, 0:1, :].astype(F32)
            for g in range(FNET_GROUPS):
                sl = slice(g * FNET_GROUP_DIM, (g + 1) * FNET_GROUP_DIM)
                a = (acc_c[bi, :, sl] + sign * mid[:, sl]).astype(BF16)
                bm = acc_s[bi, :, sl].astype(BF16)
                y = jnp.dot(a, cc, preferred_element_type=F32) - jnp.dot(bm, sc, preferred_element_type=F32)
                o_ref[bi, :, sl] = (y * scale).astype(BF16)


def _angle_tables(n_rows, n_cols, stride, period):
    k = jnp.arange(n_rows, dtype=jnp.int32)[:, None]
    n = jnp.arange(n_cols, dtype=jnp.int32)[None, :]
    ang = ((k * n * stride) % period).astype(F32) * (2.0 * math.pi / period)
    return jnp.cos(ang), jnp.sin(ang)


def _dft_tables(n):
    ca, sa = _angle_tables(n // LANES, n // 2, LANES, n)
    cb, sb = _angle_tables(LANES, n // 2, 1, n)
    ca, sa, cb, sb = ca[:, None, :], sa[:, None, :], cb[None, :, :], sb[None, :, :]
    cm = (ca * cb - sa * sb).astype(BF16).reshape(n, n // 2)
    sm = (sa * cb + ca * sb).astype(BF16).reshape(n, n // 2)
    return cm, sm


def _fourier_seq(hin, row0, n, chan_tabs):
    b = hin.shape[0]
    half = n // 2
    tf = _pick_tile(half, (256, 128))
    off_f, last = row0 // tf, n // tf - 1
    seq = lambda index: pl.BlockSpec((b, tf, FNET_WIDTH), lambda j: (0, off_f + index(j), 0))
    folded = jax.ShapeDtypeStruct((b, half, FNET_WIDTH), BF16)
    u_plus, u_minus = pl.pallas_call(
        functools.partial(_fold_kernel, nb=b, tf=tf),
        grid=(half // tf,),
        in_specs=[seq(lambda j: j), seq(lambda j: last - j), seq(lambda j: jnp.minimum(last + 1 - j, last))],
        out_specs=[pl.BlockSpec((b, tf, FNET_WIDTH), lambda j: (0, j, 0))] * 2,
        out_shape=[folded, folded],
        compiler_params=_cparams("parallel"),
        name=f"fourier_fold_{n}",
    )(hin, hin, hin)

    cm, sm = _dft_tables(n)
    cc, sc = chan_tabs
    tk = _pick_tile(n, (1024, 512, 256))
    tn = _pick_tile(half, (1024, 512, 256, 128))
    mid_rows = 16
    folded_spec = pl.BlockSpec((b, tn, FNET_WIDTH), lambda i, j: (0, j, 0))
    return pl.pallas_call(
        functools.partial(_dft_kernel, nb=b, tk=tk, scale=1.0 / math.sqrt(n * FNET_GROUP_DIM)),
        grid=(n // tk, half // tn),
        in_specs=[
            pl.BlockSpec((tk, tn), lambda i, j: (i, j)),
            pl.BlockSpec((tk, tn), lambda i, j: (i, j)),
            folded_spec,
            folded_spec,
            pl.BlockSpec((b, mid_rows, FNET_WIDTH), lambda i, j: (0, (row0 + half) // mid_rows, 0)),
            _const_spec((FNET_GROUP_DIM, FNET_GROUP_DIM)),
            _const_spec((FNET_GROUP_DIM, FNET_GROUP_DIM)),
        ],
        out_specs=pl.BlockSpec((b, tk, FNET_WIDTH), lambda i, j: (0, i, 0)),
        out_shape=jax.ShapeDtypeStruct((b, n, FNET_WIDTH), BF16),
        scratch_shapes=[pltpu.VMEM((b, tk, FNET_WIDTH), F32), pltpu.VMEM((b, tk, FNET_WIDTH), F32)],
        compiler_params=_cparams("parallel", "arbitrary"),
        name=f"fourier_seq_{n}",
    )(cm, sm, u_plus, u_minus, hin, cc, sc)


def _log_sigmoid(x):
    return jnp.minimum(x, 0.0) - jnp.log(1.0 + jnp.exp(-jnp.abs(x)))


def _retention_kernel(q_ref, k_ref, v_ref, g_ref, decf_ref, decb_ref, o_ref,
                      of_scr, ob_scr, dec_scr, *, n_lat, n_ctx):
    c = RET_CHUNK
    pos_i = lax.broadcasted_iota(jnp.int32, (c, c), 0).astype(F32)
    pos_j = lax.broadcasted_iota(jnp.int32, (c, c), 1).astype(F32)
    lg_f = _log_sigmoid(decf_ref[0])
    lg_b = _log_sigmoid(decb_ref[0])
    rel = pos_i - pos_j
    mask_f = rel >= 0.0
    mask_b = rel < 0.0
    dec_scr[0] = jnp.where(mask_f, jnp.exp(lg_f * jnp.where(mask_f, rel, 0.0)), 0.0)
    dec_scr[1] = jnp.where(mask_b, jnp.exp(lg_b * jnp.where(mask_b, -rel, 0.0)), 0.0)
    dec_scr[2] = jnp.exp(lg_f * (pos_i + 1.0))
    dec_scr[3] = jnp.exp(lg_b * (c - pos_i))
    dec_scr[4] = jnp.exp(lg_f * (c - 1.0 - pos_i))
    dec_scr[5] = jnp.exp(lg_b * pos_i)
    dec_scr[6] = jnp.exp(lg_f * (c + 0.0 * pos_i))
    dec_scr[7] = jnp.exp(lg_b * (c + 0.0 * pos_i))
    def one_dir(d, start, out_scr, state):
        qc = q_ref[0, pl.ds(start, c), :]
        kc = k_ref[0, pl.ds(start, c), :]
        vc = v_ref[0, pl.ds(start, c), :]
        scores = lax.dot_general(qc, kc, (((1,), (1,)), ((), ())), preferred_element_type=F32)
        intra = jnp.dot((scores * dec_scr[d]).astype(BF16), vc, preferred_element_type=F32)
        inter = jnp.dot(qc, state.astype(BF16), preferred_element_type=F32)
        out_scr[pl.ds(start, c), :] = intra + dec_scr[2 + d] * inter
        vd = (vc.astype(F32) * dec_scr[4 + d]).astype(BF16)
        kv = lax.dot_general(kc, vd, (((0,), (0,)), ((), ())), preferred_element_type=F32)
        return dec_scr[6 + d] * state + kv

    s_f = jnp.zeros((LANES, LANES), F32)
    s_b = jnp.zeros((LANES, LANES), F32)
    n_ctx_chunks = n_ctx // c
    for ci in range(n_ctx_chunks):
        s_f = one_dir(0, n_lat + ci * c, of_scr, s_f)
        s_b = one_dir(1, n_lat + (n_ctx_chunks - 1 - ci) * c, ob_scr, s_b)

    n_lat_chunks = n_lat // c

    def body(i, states):
        s_f = one_dir(0, pl.multiple_of(i * c, c), of_scr, states[0])
        s_b = one_dir(1, pl.multiple_of((n_lat_chunks - 1 - i) * c, c), ob_scr, states[1])
        return s_f, s_b

    lax.fori_loop(0, n_lat_chunks, body, (s_f, s_b), unroll=_pick_tile(n_lat_chunks, (8, 4, 2, 1)))

    def finish(i, carry):
        r0 = pl.multiple_of(i * c, c)
        o = of_scr[pl.ds(r0, c), :] + ob_scr[pl.ds(r0, c), :]
        o = o * lax.rsqrt(jnp.mean(o * o, axis=-1, keepdims=True) + EPS)
        g = g_ref[0, pl.ds(r0, c), :].astype(F32)
        o_ref[0, pl.ds(r0, c), :] = (o * (g * _sigmoid(g))).astype(BF16)
        return carry

    n_chunks = (n_lat + n_ctx) // c
    lax.fori_loop(0, n_chunks, finish, 0, unroll=_pick_tile(n_chunks, (6, 3, 2, 1)))


def _retention(hin, dec_f, dec_b, n_lat):
    b, t, _ = hin.shape
    h = RET_HEADS
    seq = lambda col0: pl.BlockSpec((1, t, LANES), lambda bi, hi: (bi, 0, col0 + hi))
    dec = pl.BlockSpec((1, 1, LANES), lambda bi, hi: (hi, 0, 0))
    return pl.pallas_call(
        functools.partial(_retention_kernel, n_lat=n_lat, n_ctx=t - n_lat),
        grid=(b, h),
        in_specs=[seq(4), seq(8), seq(12), seq(16), dec, dec],
        out_specs=pl.BlockSpec((1, t, LANES), lambda bi, hi: (bi, 0, hi)),
        out_shape=jax.ShapeDtypeStruct((b, t, h * LANES), BF16),
        scratch_shapes=[
            pltpu.VMEM((t, LANES), F32),
            pltpu.VMEM((t, LANES), F32),
            pltpu.VMEM((8, RET_CHUNK, LANES), F32),
        ],
        compiler_params=_cparams("parallel", "parallel"),
        name="retention",
    )(hin, hin, hin, hin, dec_f, dec_b)


def _merge_kernel(*refs, n_parts, n_lat, tm):
    x_ref, modl_ref, modc_ref = refs[:3]
    a_refs = refs[3:3 + n_parts]
    w_refs = refs[3 + n_parts:3 + 2 * n_parts]
    o_ref = refs[3 + 2 * n_parts]
    row0 = pl.program_id(1) * tm
    y = jnp.dot(a_refs[0][0], w_refs[0][...], preferred_element_type=F32)
    for p in range(1, n_parts):
        y = y + jnp.dot(a_refs[p][0], w_refs[p][...], preferred_element_type=F32)
    gate = _gate_rows(modl_ref, modc_ref, 2, row0, tm, n_lat)
    o_ref[0] = x_ref[0] + gate * y


def _merge(x, mod, parts, weights, n_lat):
    b, t, d = x.shape
    tm = _pick_tile(t, (768, 256))
    n_parts = len(parts)
    in_specs = [
        pl.BlockSpec((1, tm, d), lambda bi, i: (bi, i, 0)),
        pl.BlockSpec((1, N_MOD, d), lambda bi, i: (bi, 0, 0)),
        pl.BlockSpec((1, N_MOD, d), lambda bi, i: (b, 0, 0)),
    ]
    in_specs += [pl.BlockSpec((1, tm, p.shape[2]), lambda bi, i: (bi, i, 0)) for p in parts]
    in_specs += [_resident_spec(w.shape) for w in weights]
    return pl.pallas_call(
        functools.partial(_merge_kernel, n_parts=n_parts, n_lat=n_lat, tm=tm),
        grid=(b, t // tm),
        in_specs=in_specs,
        out_specs=pl.BlockSpec((1, tm, d), lambda bi, i: (bi, i, 0)),
        out_shape=jax.ShapeDtypeStruct((b, t, d), F32),
        input_output_aliases={0: 0},
        compiler_params=_cparams("parallel", "parallel"),
        name=f"merge_{n_parts}",
    )(x, mod, mod, *parts, *weights)


def _attn_kernel(prev_ref, q_ref, k_ref, v_ref, lam_ref, sg_ref, o_ref, *, tk, lam_init):
    del prev_ref
    tq = q_ref.shape[1]
    n_keys = k_ref.shape[1]
    q = q_ref[0]
    lane = lax.broadcasted_iota(jnp.int32, (tq, LANES), 1)
    zero = jnp.zeros_like(q)
    q_sub = (jnp.where(lane < HEAD_DIM, q, zero), jnp.where(lane >= HEAD_DIM, q, zero))
    ones = jnp.ones((tk, LANES), BF16)
    m = [None, None]
    acc = [None, None]
    for j in range(n_keys // tk):
        kc = k_ref[0, j * tk:(j + 1) * tk, :]
        v1 = jnp.concatenate([v_ref[0, j * tk:(j + 1) * tk, :], ones], axis=1)
        for sub in range(2):
            s = lax.dot_general(q_sub[sub], kc, (((1,), (1,)), ((), ())), preferred_element_type=F32)
            mx = jnp.max(s, axis=-1, keepdims=True)
            if j == 0:
                m[sub] = mx
                p = jnp.exp2(s - mx).astype(BF16)
                acc[sub] = jnp.dot(p, v1, preferred_element_type=F32)
            else:
                m_new = jnp.maximum(m[sub], mx)
                alpha = jnp.exp2(m[sub] - m_new)
                p = jnp.exp2(s - m_new).astype(BF16)
                acc[sub] = alpha * acc[sub] + jnp.dot(p, v1, preferred_element_type=F32)
                m[sub] = m_new

    lv = lam_ref[...]
    s1 = jnp.sum(lv[0:1] * lv[1:2], axis=-1, keepdims=True)
    s2 = jnp.sum(lv[2:3] * lv[3:4], axis=-1, keepdims=True)
    lam = jnp.exp(s1) - jnp.exp(s2) + lam_init
    o = acc[0][:, :LANES] / acc[0][:, LANES:] - lam * (acc[1][:, :LANES] / acc[1][:, LANES:])
    o = o * lax.rsqrt(jnp.mean(o * o, axis=-1, keepdims=True) + EPS)
    o_ref[0] = (o * sg_ref[...] * (1.0 - lam_init)).astype(BF16)


def _attention(out, qkv, lam_vecs, sub_gain, lam_init, n_lat, ctx_queries):
    b, t, _ = qkv.shape
    h = DIFF_HEADS
    n_ctx = t - n_lat
    if ctx_queries:
        tq, n_q_tiles, q_off = n_ctx, 1, n_lat // n_ctx
        keys, k_off = n_ctx, n_lat // n_ctx
        tk = n_ctx
    else:
        tq = _pick_tile(n_lat, (1024, 512, 256))
        n_q_tiles, q_off = n_lat // tq, 0
        keys, k_off = t, 0
        tk = _pick_tile(t, (768, 256))
    return pl.pallas_call(
        functools.partial(_attn_kernel, tk=tk, lam_init=lam_init),
        grid=(b, h, n_q_tiles),
        in_specs=[
            pl.BlockSpec(memory_space=pl.ANY),
            pl.BlockSpec((1, tq, LANES), lambda bi, hi, i: (bi, q_off + i, hi)),
            pl.BlockSpec((1, keys, LANES), lambda bi, hi, i: (bi, k_off, h + hi)),
            pl.BlockSpec((1, keys, LANES), lambda bi, hi, i: (bi, k_off, 2 * h + hi)),
            _const_spec((4, LANES)),
            _const_spec((1, LANES)),
        ],
        out_specs=pl.BlockSpec((1, tq, LANES), lambda bi, hi, i: (bi, q_off + i, hi)),
        out_shape=jax.ShapeDtypeStruct((b, t, h * LANES), BF16),
        input_output_aliases={0: 0},
        compiler_params=_cparams("parallel", "parallel", "arbitrary"),
        name="diff_attn_ctx" if ctx_queries else "diff_attn_lat",
    )(out, qkv, qkv, qkv, lam_vecs, sub_gain)


def _ffn_kernel(x_ref, modl_ref, modc_ref, w1_ref, w3_ref, w2_ref, o_ref, *, n_lat, tm, tf):
    row0 = pl.program_id(1) * tm
    x = x_ref[0]
    h = _modulated(x, modl_ref, modc_ref, 3, row0, n_lat).astype(BF16)
    f_dim = w1_ref.shape[1]
    y = jnp.zeros((tm, x.shape[1]), F32)
    for c in range(f_dim // tf):
        sl = slice(c * tf, (c + 1) * tf)
        a = jnp.dot(h, w1_ref[:, sl], preferred_element_type=F32)
        g = jnp.dot(h, w3_ref[:, sl], preferred_element_type=F32)
        u = (a * _sigmoid(a) * g).astype(BF16)
        y = y + jnp.dot(u, w2_ref[sl, :], preferred_element_type=F32)
    gate = _gate_rows(modl_ref, modc_ref, 5, row0, tm, n_lat)
    o_ref[0] = x + gate * y


def _ffn(x, mod, w1, w3, w2, n_lat):
    b, t, d = x.shape
    tm = _pick_tile(t, (768, 256))
    tf = _pick_tile(w1.shape[1], (256, 128))
    return pl.pallas_call(
        functools.partial(_ffn_kernel, n_lat=n_lat, tm=tm, tf=tf),
        grid=(b, t // tm),
        in_specs=[
            pl.BlockSpec((1, tm, d), lambda bi, i: (bi, i, 0)),
            pl.BlockSpec((1, N_MOD, d), lambda bi, i: (bi, 0, 0)),
            pl.BlockSpec((1, N_MOD, d), lambda bi, i: (b, 0, 0)),
            _resident_spec(w1.shape),
            _resident_spec(w3.shape),
            _resident_spec(w2.shape),
        ],
        out_specs=pl.BlockSpec((1, tm, d), lambda bi, i: (bi, i, 0)),
        out_shape=jax.ShapeDtypeStruct((b, t, d), F32),
        input_output_aliases={0: 0},
        compiler_params=_cparams("parallel", "parallel"),
        name="ffn_swiglu",
    )(x, mod, mod, w1, w3, w2)


ROUTE_E1, ROUTE_E2, ROUTE_G1, ROUTE_G2 = 0, 1, 2, 3
SLABS = D_MODEL // LANES
DMA_LOOP_UNROLL = 8


def _top2_route(logits):
    tm = logits.shape[0]
    lane = lax.broadcasted_iota(jnp.int32, (tm, LANES), 1)
    neg = jnp.float32(-jnp.inf)
    lg = jnp.where(lane < N_EXPERTS, logits, neg)
    m1 = jnp.max(lg, axis=-1, keepdims=True)
    i1 = jnp.min(jnp.where(lg == m1, lane, LANES), axis=-1, keepdims=True)
    lg2 = jnp.where(lane == i1, neg, lg)
    m2 = jnp.max(lg2, axis=-1, keepdims=True)
    i2 = jnp.min(jnp.where(lg2 == m2, lane, LANES), axis=-1, keepdims=True)
    e = jnp.exp(m2 - m1)
    g1 = 1.0 / (1.0 + e)
    g2 = e / (1.0 + e)
    rec = jnp.where(lane == ROUTE_E1, i1.astype(F32), 0.0) + jnp.where(lane == ROUTE_E2, i2.astype(F32), 0.0)
    return rec + jnp.where(lane == ROUTE_G1, g1, 0.0) + jnp.where(lane == ROUTE_G2, g2, 0.0)


def _slab_store(ref, val, n_rows):
    for s in range(SLABS):
        ref[pl.ds(s, n_rows, stride=SLABS), :] = val[:, s * LANES:(s + 1) * LANES]


def _slab_load(ref, n_rows):
    return jnp.concatenate([ref[pl.ds(s, n_rows, stride=SLABS), :] for s in range(SLABS)], axis=1)


def _route_kernel(x_ref, modl_ref, modc_ref, wr_ref, h_ref, rec_ref, *, n_lat, tm):
    row0 = pl.program_id(1) * tm
    hf = _modulated(x_ref[0], modl_ref, modc_ref, 3, row0, n_lat)
    logits = jnp.dot(hf, wr_ref[...], preferred_element_type=F32, precision=lax.Precision.HIGHEST)
    rec_ref[0] = _top2_route(logits)
    _slab_store(h_ref, hf, tm)


def _moe_route(x, mod, w_router, n_lat):
    b, t, d = x.shape
    tm = _pick_tile(t, (768, 256))
    tiles = t // tm
    return pl.pallas_call(
        functools.partial(_route_kernel, n_lat=n_lat, tm=tm),
        grid=(b, tiles),
        in_specs=[
            pl.BlockSpec((1, tm, d), lambda bi, i: (bi, i, 0)),
            pl.BlockSpec((1, N_MOD, d), lambda bi, i: (bi, 0, 0)),
            pl.BlockSpec((1, N_MOD, d), lambda bi, i: (b, 0, 0)),
            _const_spec((d, LANES)),
        ],
        out_specs=[
            pl.BlockSpec((tm * SLABS, LANES), lambda bi, i: (bi * tiles + i, 0)),
            pl.BlockSpec((1, tm, LANES), lambda bi, i: (bi, i, 0)),
        ],
        out_shape=[
            jax.ShapeDtypeStruct((b * t * SLABS, LANES), F32),
            jax.ShapeDtypeStruct((b, t, LANES), F32),
        ],
        compiler_params=_cparams("parallel", "parallel"),
        name="moe_route",
    )(x, mod, mod, w_router)


def _row_copy(src_ref, src_row, dst_ref, dst_row, sem):
    return pltpu.make_async_copy(
        src_ref.at[pl.ds(pl.multiple_of(src_row * SLABS, SLABS), SLABS), :],
        dst_ref.at[pl.ds(pl.multiple_of(dst_row * SLABS, SLABS), SLABS), :],
        sem)


def _dispatch_kernel(dest_ref, h_ref, init_ref, xs_ref, sem, *, tm):
    del init_ref

    def start(r, carry):
        for slot in range(2):
            _row_copy(h_ref, r, xs_ref, dest_ref[0, 0, 2 * r + slot], sem).start()
        return carry

    def wait(r, carry):
        for slot in range(2):
            _row_copy(h_ref, r, xs_ref, dest_ref[0, 0, 2 * r + slot], sem).wait()
        return carry

    lax.fori_loop(0, tm, start, 0, unroll=DMA_LOOP_UNROLL)
    lax.fori_loop(0, tm, wait, 0, unroll=DMA_LOOP_UNROLL)


def _moe_dispatch(h_slabs, dest, n_sorted_rows):
    n_tok = h_slabs.shape[0] // SLABS
    tm = _pick_tile(n_tok, (512, 256))
    steps = n_tok // tm
    return pl.pallas_call(
        functools.partial(_dispatch_kernel, tm=tm),
        grid=(steps,),
        in_specs=[
            pl.BlockSpec((1, 1, 2 * tm), lambda i: (i, 0, 0), memory_space=pltpu.SMEM),
            pl.BlockSpec((tm * SLABS, LANES), lambda i: (i, 0)),
            pl.BlockSpec(memory_space=pl.ANY),
        ],
        out_specs=pl.BlockSpec(memory_space=pl.ANY),
        out_shape=jax.ShapeDtypeStruct((n_sorted_rows * SLABS, LANES), F32),
        scratch_shapes=[pltpu.SemaphoreType.DMA(())],
        input_output_aliases={2: 0},
        compiler_params=_cparams("arbitrary"),
        name="moe_dispatch",
    )(dest.reshape(steps, 1, 2 * tm), h_slabs, jnp.zeros((n_sorted_rows * SLABS, LANES), F32))


def _experts_kernel(te_ref, used_ref, x_ref, w1_ref, w3_ref, w2_ref, o_ref, h_scr, acc_scr, *, tm, tf):
    del te_ref
    i = pl.program_id(0)
    f = pl.program_id(1)
    valid = i < used_ref[0]

    @pl.when(f == 0)
    def _():
        acc_scr[...] = jnp.zeros_like(acc_scr)

    @pl.when(valid & (f == 0))
    def _():
        h_scr[...] = _slab_load(x_ref, tm).astype(BF16)

    @pl.when(valid)
    def _():
        h = h_scr[...]
        f_blk = w1_ref.shape[2]
        y = jnp.zeros(acc_scr.shape, F32)
        for c in range(f_blk // tf):
            sl = slice(c * tf, (c + 1) * tf)
            a = jnp.dot(h, w1_ref[0, :, sl], preferred_element_type=F32)
            g = jnp.dot(h, w3_ref[0, :, sl], preferred_element_type=F32)
            u = (a * _sigmoid(a) * g).astype(BF16)
            y = y + jnp.dot(u, w2_ref[0, sl, :], preferred_element_type=F32)
        acc_scr[...] += y

    @pl.when(f == pl.num_programs(1) - 1)
    def _():
        _slab_store(o_ref, acc_scr[...], tm)


def _moe_experts(xs, tile_expert, n_used, w1, w3, w2, tm):
    n_exp, d, f_dim = w1.shape
    n_tiles = xs.shape[0] // (tm * SLABS)
    f_blk = _pick_tile(f_dim, (1792, 512, 256))
    tf = _pick_tile(f_blk, (256, 128))
    n_f = f_dim // f_blk

    def f_eff(i, f, used):
        return jnp.where(i < used[0], f, n_f - 1)

    grid_spec = pltpu.PrefetchScalarGridSpec(
        num_scalar_prefetch=2,
        grid=(n_tiles, n_f),
        in_specs=[
            pl.BlockSpec((tm * SLABS, LANES), lambda i, f, te, used: (jnp.minimum(i, used[0] - 1), 0)),
            pl.BlockSpec((1, d, f_blk), lambda i, f, te, used: (te[i], 0, f_eff(i, f, used))),
            pl.BlockSpec((1, d, f_blk), lambda i, f, te, used: (te[i], 0, f_eff(i, f, used))),
            pl.BlockSpec((1, f_blk, d), lambda i, f, te, used: (te[i], f_eff(i, f, used), 0)),
        ],
        out_specs=pl.BlockSpec((tm * SLABS, LANES), lambda i, f, te, used: (i, 0)),
        scratch_shapes=[pltpu.VMEM((tm, d), BF16), pltpu.VMEM((tm, d), F32)],
    )
    return pl.pallas_call(
        functools.partial(_experts_kernel, tm=tm, tf=tf),
        grid_spec=grid_spec,
        out_shape=jax.ShapeDtypeStruct(xs.shape, F32),
        compiler_params=_cparams("arbitrary", "arbitrary"),
        name="moe_experts",
    )(tile_expert, n_used, xs, w1, w3, w2)


def _combine_kernel(dcur_ref, dnext_ref, x_ref, modl_ref, modc_ref, rec_ref, ys_ref, o_ref, buf, sems,
                    *, n_lat, tm, tiles_per_batch):
    i = pl.program_id(0)
    n = pl.num_programs(0)

    def gather(dest_ref, slot_buf, op):
        def body(r, carry):
            for slot in range(2):
                cp = _row_copy(ys_ref, dest_ref[0, 0, 2 * r + slot], buf.at[slot_buf, slot], r, sems.at[slot_buf])
                cp.start() if op == "start" else cp.wait()
            return carry
        lax.fori_loop(0, tm, body, 0, unroll=DMA_LOOP_UNROLL)

    @pl.when(i == 0)
    def _():
        gather(dcur_ref, 0, "start")

    for parity in range(2):
        @pl.when((i + 1 < n) & ((i + 1) % 2 == parity))
        def _():
            gather(dnext_ref, parity, "start")

    for parity in range(2):
        @pl.when(i % 2 == parity)
        def _():
            gather(dcur_ref, parity, "wait")
            rec = rec_ref[0]
            y = (rec[:, ROUTE_G1:ROUTE_G1 + 1] * _slab_load(buf.at[parity, 0], tm)
                 + rec[:, ROUTE_G2:ROUTE_G2 + 1] * _slab_load(buf.at[parity, 1], tm))
            row0 = (i % tiles_per_batch) * tm
            gate = _gate_rows(modl_ref, modc_ref, 5, row0, tm, n_lat)
            o_ref[0] = x_ref[0] + gate * y


def _moe_combine(x, mod, rec, ys, dest, n_lat):
    b, t, d = x.shape
    tm = _pick_tile(t, (256,))
    tpb = t // tm
    n_tiles = b * tpb
    dest3 = dest.reshape(n_tiles, 1, 2 * tm)
    tok = lambda i: (i // tpb, i % tpb, 0)
    return pl.pallas_call(
        functools.partial(_combine_kernel, n_lat=n_lat, tm=tm, tiles_per_batch=tpb),
        grid=(n_tiles,),
        in_specs=[
            pl.BlockSpec((1, 1, 2 * tm), lambda i: (i, 0, 0), memory_space=pltpu.SMEM),
            pl.BlockSpec((1, 1, 2 * tm), lambda i: (jnp.minimum(i + 1, n_tiles - 1), 0, 0),
                         memory_space=pltpu.SMEM),
            pl.BlockSpec((1, tm, d), tok),
            pl.BlockSpec((1, N_MOD, d), lambda i: (i // tpb, 0, 0)),
            pl.BlockSpec((1, N_MOD, d), lambda i: (b, 0, 0)),
            pl.BlockSpec((1, tm, LANES), tok),
            pl.BlockSpec(memory_space=pl.ANY),
        ],
        out_specs=pl.BlockSpec((1, tm, d), tok),
        out_shape=jax.ShapeDtypeStruct((b, t, d), F32),
        scratch_shapes=[pltpu.VMEM((2, 2, tm * SLABS, LANES), F32), pltpu.SemaphoreType.DMA((2,))],
        input_output_aliases={2: 0},
        compiler_params=_cparams("arbitrary"),
        name="moe_combine",
    )(dest3, dest3, x, mod, mod, rec, ys)


def _dispatch_plan(rec, tm, n_tiles):
    experts = rec[:, ROUTE_E1:ROUTE_E2 + 1].astype(jnp.int32).reshape(-1)
    onehot = (experts[:, None] == jnp.arange(N_EXPERTS, dtype=jnp.int32)[None, :]).astype(jnp.int32)
    csum = jnp.cumsum(onehot, axis=0)
    counts = csum[-1]
    padded = (counts + tm - 1) // tm * tm
    ends = jnp.cumsum(padded)
    dest = jnp.sum(onehot * (csum - 1 + (ends - padded)[None, :]), axis=1)
    n_used = (ends[-1] // tm).astype(jnp.int32)
    tile_start = jnp.arange(n_tiles, dtype=jnp.int32) * tm
    tile_expert = jnp.sum((tile_start[:, None] >= ends[None, :]).astype(jnp.int32), axis=1)
    tile_expert = jnp.minimum(tile_expert, tile_expert[n_used - 1])
    return dest.astype(jnp.int32), tile_expert.astype(jnp.int32), n_used.reshape(1)


def _moe(x, mod, w_router, w1, w3, w2, first_expert, n_lat):
    b, t, d = x.shape
    n_tok = b * t
    tm = 512
    n_tiles = (2 * n_tok + N_EXPERTS * (tm - 1)) // tm
    h_slabs, rec = _moe_route(x, mod, w_router, n_lat)
    dest, tile_expert, n_used = _dispatch_plan(rec.reshape(n_tok, LANES), tm, n_tiles)
    xs = _moe_dispatch(h_slabs, dest, n_tiles * tm)
    ys = _moe_experts(xs, tile_expert + first_expert, n_used, w1, w3, w2, tm)
    return _moe_combine(x, mod, rec, ys, dest, n_lat)


def _rot_cols(w):
    half = HEAD_DIM // 2
    return jnp.concatenate([-w[..., half:], w[..., :half]], axis=-1)


def _with_rot(w_heads):
    d, n, _ = w_heads.shape
    return jnp.concatenate([w_heads, _rot_cols(w_heads)], axis=-1).reshape(d, n * LANES)


def _gain_lanes(gain):
    return jnp.tile(gain.astype(F32), LANES // HEAD_DIM).reshape(1, LANES)


def _head_sum_matrix():
    idx = jnp.arange(LANES, dtype=jnp.int32) // HEAD_DIM
    return (idx[:, None] == idx[None, :]).astype(BF16)


def _rope_table(n_lat, n_ctx):
    rows = n_lat // GRID_W
    row = jnp.repeat(jnp.arange(rows, dtype=F32), GRID_W)
    col = jnp.tile(jnp.arange(GRID_W, dtype=F32), rows)
    quarter = HEAD_DIM // 4
    inv_freq = ROPE_BASE ** (-jnp.arange(quarter, dtype=F32) / quarter)
    ang = jnp.concatenate([row[:, None] * inv_freq, col[:, None] * inv_freq], axis=-1)
    cos, sin = jnp.cos(ang), jnp.sin(ang)
    ones = jnp.ones((n_ctx, HEAD_DIM), F32)
    zeros = jnp.zeros((n_ctx, HEAD_DIM), F32)
    rows_of = lambda lat, ctx: jnp.concatenate([lat, ctx], axis=0)
    dup = rows_of(jnp.concatenate([cos, cos, sin, sin], axis=-1), jnp.concatenate([ones, zeros], axis=-1))
    cos_packed = rows_of(jnp.concatenate([cos] * 4, axis=-1), jnp.concatenate([ones, ones], axis=-1))
    sin_packed = rows_of(jnp.concatenate([-sin, sin, -sin, sin], axis=-1), jnp.concatenate([zeros, zeros], axis=-1))
    return dup, cos_packed, sin_packed


def _chan_tables():
    idx = jnp.arange(FNET_GROUP_DIM, dtype=jnp.int32)
    ang = ((idx[:, None] * idx[None, :]) % FNET_GROUP_DIM).astype(F32) * (2.0 * math.pi / FNET_GROUP_DIM)
    return jnp.cos(ang).astype(BF16), jnp.sin(ang).astype(BF16)


def kernel(x, c, ctx, c_ctx, w_mod, b_mod, w_in_even, w_out_even, ret_decay_fwd, ret_decay_bwd,
           ffn_w1, ffn_w3, ffn_w2, w_in_odd, w_out_odd, q_norm_gain, k_norm_gain,
           lambda_q1, lambda_k1, lambda_q2, lambda_k2, subln_gain, w_router, moe_w1, moe_w3, moe_w2):
    b, n_lat, d = x.shape
    n_ctx = ctx.shape[1]
    depth = w_mod.shape[0]
    assert d == D_MODEL and b < MOD_ROWS and n_lat % n_ctx == 0 and n_ctx % RET_CHUNK == 0

    stream = jnp.concatenate([x, ctx], axis=1)
    cond = jnp.zeros((MOD_ROWS, d), F32).at[:b].set(c).at[b].set(c_ctx)
    mods = _adaln(cond, w_mod, b_mod).reshape(depth, MOD_ROWS, N_MOD, d)
    tab, cos_packed, sin_packed = _rope_table(n_lat, n_ctx)
    chan_tabs = _chan_tables()
    stack_experts = lambda w: w.astype(BF16).reshape((-1,) + w.shape[2:])
    moe_w1_all, moe_w3_all, moe_w2_all = stack_experts(moe_w1), stack_experts(moe_w3), stack_experts(moe_w2)

    for layer in range(depth):
        i = layer // 2
        mod = mods[layer]
        if layer % 2 == 0:
            w = w_in_even[i]
            hq = RET_HEADS * HEAD_DIM
            wq = w[:, FNET_WIDTH:FNET_WIDTH + hq].reshape(d, RET_HEADS, HEAD_DIM)
            wk = w[:, FNET_WIDTH + hq:FNET_WIDTH + 2 * hq].reshape(d, RET_HEADS, HEAD_DIM)
            w_in = jnp.concatenate(
                [w[:, :FNET_WIDTH], _with_rot(wq), _with_rot(wk), w[:, FNET_WIDTH + 2 * hq:]], axis=1).astype(BF16)
            hin = _inproj(stream, mod, w_in, [tab], [], n_lat, w_in.shape[1], True)
            four = jnp.concatenate(
                [_fourier_seq(hin, 0, n_lat, chan_tabs), _fourier_seq(hin, n_lat, n_ctx, chan_tabs)], axis=1)
            dec_f = jnp.broadcast_to(ret_decay_fwd[i].astype(F32)[:, None, None], (RET_HEADS, 1, LANES))
            dec_b = jnp.broadcast_to(ret_decay_bwd[i].astype(F32)[:, None, None], (RET_HEADS, 1, LANES))
            ret = _retention(hin, dec_f, dec_b, n_lat)
            w_out = w_out_even[i].astype(BF16)
            stream = _merge(stream, mod, [four, ret], [w_out[:FNET_WIDTH], w_out[FNET_WIDTH:]], n_lat)
            stream = _ffn(stream, mod, ffn_w1[i].astype(BF16), ffn_w3[i].astype(BF16),
                          ffn_w2[i].astype(BF16), n_lat)
        else:
            consts = [_gain_lanes(q_norm_gain[i]), _gain_lanes(k_norm_gain[i]), _head_sum_matrix()]
            qkv = _inproj(stream, mod, w_in_odd[i].astype(BF16), [cos_packed, sin_packed], consts, n_lat,
                          3 * DIFF_HEADS * LANES, False)
            lam_init = 0.8 - 0.6 * float(np.exp(-0.3 * layer))
            lam_vecs = jnp.zeros((4, LANES), F32).at[:, :HEAD_DIM].set(
                jnp.stack([lambda_q1[i], lambda_k1[i], lambda_q2[i], lambda_k2[i]]).astype(F32))
            sub_gain = subln_gain[i].astype(F32).reshape(1, LANES)
            att = jnp.zeros((b, n_lat + n_ctx, DIFF_HEADS * LANES), BF16)
            att = _attention(att, qkv, lam_vecs, sub_gain, lam_init, n_lat, False)
            att = _attention(att, qkv, lam_vecs, sub_gain, lam_init, n_lat, True)
            stream = _merge(stream, mod, [att], [w_out_odd[i].astype(BF16)], n_lat)
            wr = jnp.zeros((d, LANES), F32).at[:, :N_EXPERTS].set(w_router[i].astype(F32))
            stream = _moe(stream, mod, wr, moe_w1_all, moe_w3_all, moe_w2_all, i * N_EXPERTS, n_lat)
    return stream[:, :n_lat]
```

```python
import functools
import math

import jax
import jax.numpy as jnp
import numpy as np
from jax import lax
from jax.experimental import pallas as pl
from jax.experimental.pallas import tpu as pltpu

F32 = jnp.float32
BF16 = jnp.bfloat16

D_MODEL = 1024
GRID_W = 64
HEAD_DIM = 64
LANES = 128
FNET_GROUPS = 4
FNET_GROUP_DIM = 128
FNET_WIDTH = FNET_GROUPS * FNET_GROUP_DIM
RET_HEADS = 4
RET_CHUNK = 128
DIFF_HEADS = 8
N_EXPERTS = 8
N_MOD = 6
ROPE_BASE = 10000.0
EPS = 1e-6
LOG2_E = math.log2(math.e)
VMEM_LIMIT_BYTES = 56 * 1024 * 1024
MOD_ROWS = 8


def _cparams(*sem):
    return pltpu.CompilerParams(dimension_semantics=sem, vmem_limit_bytes=VMEM_LIMIT_BYTES)


def _pick_tile(n, candidates):
    for c in candidates:
        if n % c == 0:
            return c
    raise ValueError(f"no tile in {candidates} divides {n}")


def _const_spec(shape):
    nd = len(shape)
    return pl.BlockSpec(shape, lambda *_: (0,) * nd)


def _resident_spec(shape):
    nd = len(shape)
    return pl.BlockSpec(shape, lambda *_: (0,) * nd, pipeline_mode=pl.Buffered(1))


def _sigmoid(x):
    return 1.0 / (1.0 + jnp.exp(-x))


def _modulated(x, modl_ref, modc_ref, k_shift, row0, n_lat):
    tm = x.shape[0]
    ms = jnp.mean(x * x, axis=-1, keepdims=True)
    xn = x * lax.rsqrt(ms + EPS)
    rows = row0 + lax.broadcasted_iota(jnp.int32, (tm, 1), 0)
    is_ctx = rows >= n_lat
    shift = jnp.where(is_ctx, modc_ref[0, k_shift:k_shift + 1, :], modl_ref[0, k_shift:k_shift + 1, :])
    scale = jnp.where(is_ctx, modc_ref[0, k_shift + 1:k_shift + 2, :], modl_ref[0, k_shift + 1:k_shift + 2, :])
    return xn * (1.0 + scale) + shift


def _gate_rows(modl_ref, modc_ref, k_gate, row0, tm, n_lat):
    rows = row0 + lax.broadcasted_iota(jnp.int32, (tm, 1), 0)
    return jnp.where(rows >= n_lat, modc_ref[0, k_gate:k_gate + 1, :], modl_ref[0, k_gate:k_gate + 1, :])


def _adaln_kernel(c_ref, w_ref, b_ref, o_ref):
    c = c_ref[...]
    s = c * _sigmoid(c)
    o_ref[0] = jnp.dot(s, w_ref[0], preferred_element_type=F32, precision=lax.Precision.HIGHEST) + b_ref[0]


def _adaln(cond_rows, w_mod, b_mod):
    depth, d, n = w_mod.shape
    tn = _pick_tile(n, (1536, 1024, 512))
    return pl.pallas_call(
        _adaln_kernel,
        grid=(depth, n // tn),
        in_specs=[
            _const_spec((MOD_ROWS, d)),
            pl.BlockSpec((1, d, tn), lambda l, j: (l, 0, j)),
            pl.BlockSpec((1, 1, tn), lambda l, j: (l, 0, j)),
        ],
        out_specs=pl.BlockSpec((1, MOD_ROWS, tn), lambda l, j: (l, 0, j)),
        out_shape=jax.ShapeDtypeStruct((depth, MOD_ROWS, n), F32),
        compiler_params=_cparams("parallel", "parallel"),
        name="adaln",
    )(cond_rows, w_mod, b_mod.reshape(depth, 1, n))


def _rope_dup(blk, tab):
    t = blk * tab
    return t + pltpu.roll(t, HEAD_DIM, axis=1)


def _inproj_even_kernel(x_ref, modl_ref, modc_ref, w_ref, tab_ref, o_ref, *, n_lat, tm):
    row0 = pl.program_id(1) * tm
    h = _modulated(x_ref[0], modl_ref, modc_ref, 0, row0, n_lat).astype(BF16)
    tab = tab_ref[...]
    lane = lax.broadcasted_iota(jnp.int32, (tm, LANES), 1)
    n_out = o_ref.shape[2]
    for c in range(n_out // 256):
        y = jnp.dot(h, w_ref[:, c * 256:(c + 1) * 256], preferred_element_type=F32)
        for half in range(2):
            blk = y[:, half * LANES:(half + 1) * LANES]
            col = c * 2 + half
            if 4 <= col < 8:
                blk = _rope_dup(blk, tab)
            elif 8 <= col < 12:
                blk = jnp.where(lane < HEAD_DIM, _rope_dup(blk, tab) * (HEAD_DIM ** -0.5), 0.0)
            o_ref[0, :, col * LANES:(col + 1) * LANES] = blk.astype(BF16)


def _swap_halves(y, first_half):
    fwd = pltpu.roll(y, HEAD_DIM // 2, axis=1)
    bwd = pltpu.roll(y, LANES - HEAD_DIM // 2, axis=1)
    return jnp.where(first_half, bwd, fwd)


def _inproj_odd_kernel(x_ref, modl_ref, modc_ref, w_ref, cos_ref, sin_ref, gq_ref, gk_ref, seg_ref, o_ref,
                       *, n_lat, tm):
    row0 = pl.program_id(1) * tm
    h = _modulated(x_ref[0], modl_ref, modc_ref, 0, row0, n_lat).astype(BF16)
    cos = cos_ref[...]
    sin = sin_ref[...]
    seg = seg_ref[...]
    lane = lax.broadcasted_iota(jnp.int32, (tm, LANES), 1)
    first_half = lane % HEAD_DIM < HEAD_DIM // 2
    gains = (gq_ref[...] * (HEAD_DIM ** -0.5 * LOG2_E), gk_ref[...])
    for c in range(w_ref.shape[1] // 256):
        y = jnp.dot(h, w_ref[:, c * 256:(c + 1) * 256], preferred_element_type=F32)
        part = c // (DIFF_HEADS // 2)
        if part < 2:
            for half in range(2):
                blk = y[:, half * LANES:(half + 1) * LANES]
                ms = jnp.dot((blk * blk).astype(BF16), seg, preferred_element_type=F32) * (1.0 / HEAD_DIM)
                yn = blk * lax.rsqrt(ms + EPS) * gains[part]
                out = yn * cos + _swap_halves(yn, first_half) * sin
                o_ref[0, :, (2 * c + half) * LANES:(2 * c + half + 1) * LANES] = out.astype(BF16)
        else:
            o_ref[0, :, c * 256:(c + 1) * 256] = y.astype(BF16)


def _inproj(x, mod, w, row_tabs, consts, n_lat, n_out, even):
    b, t, d = x.shape
    tm = _pick_tile(t, (768, 256))
    kern = _inproj_even_kernel if even else _inproj_odd_kernel
    in_specs = [
        pl.BlockSpec((1, tm, d), lambda bi, i: (bi, i, 0)),
        pl.BlockSpec((1, N_MOD, d), lambda bi, i: (bi, 0, 0)),
        pl.BlockSpec((1, N_MOD, d), lambda bi, i: (b, 0, 0)),
        _resident_spec(w.shape),
    ]
    in_specs += [pl.BlockSpec((tm, LANES), lambda bi, i: (i, 0)) for _ in row_tabs]
    in_specs += [_const_spec(c.shape) for c in consts]
    return pl.pallas_call(
        functools.partial(kern, n_lat=n_lat, tm=tm),
        grid=(b, t // tm),
        in_specs=in_specs,
        out_specs=pl.BlockSpec((1, tm, n_out), lambda bi, i: (bi, i, 0)),
        out_shape=jax.ShapeDtypeStruct((b, t, n_out), BF16),
        compiler_params=_cparams("parallel", "parallel"),
        name="inproj_even" if even else "inproj_odd",
    )(x, mod, mod, w, *row_tabs, *consts)


def _fold_kernel(ud_ref, ua_ref, ub_ref, p_ref, m_ref, *, nb, tf):
    j = pl.program_id(0)
    row = lax.broadcasted_iota(jnp.int32, (tf, tf), 0)
    col = lax.broadcasted_iota(jnp.int32, (tf, tf), 1)
    flip = ((row + col == tf) & (row >= 1)).astype(BF16)
    corner = ((row == 0) & (col == 0) & (j > 0)).astype(BF16)
    for bi in range(nb):
        mirrored = (jnp.dot(flip, ua_ref[bi], preferred_element_type=F32)
                    + jnp.dot(corner, ub_ref[bi], preferred_element_type=F32))
        ud = ud_ref[bi].astype(F32)
        p_ref[bi] = (ud + mirrored).astype(BF16)
        m_ref[bi] = (ud - mirrored).astype(BF16)


def _dft_kernel(c_ref, s_ref, p_ref, m_ref, mid_ref, cc_ref, sc_ref, o_ref, acc_c, acc_s, *, nb, tk, scale):
    i = pl.program_id(0)
    j = pl.program_id(1)

    @pl.when(j == 0)
    def _():
        acc_c[...] = jnp.zeros_like(acc_c)
        acc_s[...] = jnp.zeros_like(acc_s)

    cm = c_ref[...]
    sm = s_ref[...]
    for bi in range(nb):
        acc_c[bi] += jnp.dot(cm, p_ref[bi], preferred_element_type=F32)
        acc_s[bi] += jnp.dot(sm, m_ref[bi], preferred_element_type=F32)

    @pl.when(j == pl.num_programs(1) - 1)
    def _():
        cc = cc_ref[...]
        sc = sc_ref[...]
        k = i * tk + lax.broadcasted_iota(jnp.int32, (tk, 1), 0)
        sign = (1 - 2 * (k & 1)).astype(F32)
        for bi in range(nb):
            mid = mid_ref[bi, 0:1, :].astype(F32)
            for g in range(FNET_GROUPS):
                sl = slice(g * FNET_GROUP_DIM, (g + 1) * FNET_GROUP_DIM)
                a = (acc_c[bi, :, sl] + sign * mid[:, sl]).astype(BF16)
                bm = acc_s[bi, :, sl].astype(BF16)
                y = jnp.dot(a, cc, preferred_element_type=F32) - jnp.dot(bm, sc, preferred_element_type=F32)
                o_ref[bi, :, sl] = (y * scale).astype(BF16)


def _angle_tables(n_rows, n_cols, stride, period):
    k = jnp.arange(n_rows, dtype=jnp.int32)[:, None]
    n = jnp.arange(n_cols, dtype=jnp.int32)[None, :]
    ang = ((k * n * stride) % period).astype(F32) * (2.0 * math.pi / period)
    return jnp.cos(ang), jnp.sin(ang)


def _dft_tables(n):
    ca, sa = _angle_tables(n // LANES, n // 2, LANES, n)
    cb, sb = _angle_tables(LANES, n // 2, 1, n)
    ca, sa, cb, sb = ca[:, None, :], sa[:, None, :], cb[None, :, :], sb[None, :, :]
    cm = (ca * cb - sa * sb).astype(BF16).reshape(n, n // 2)
    sm = (sa * cb + ca * sb).astype(BF16).reshape(n, n // 2)
    return cm, sm


def _fourier_seq(hin, row0, n, chan_tabs):
    b = hin.shape[0]
    half = n // 2
    tf = _pick_tile(half, (256, 128))
    off_f, last = row0 // tf, n // tf - 1
    seq = lambda index: pl.BlockSpec((b, tf, FNET_WIDTH), lambda j: (0, off_f + index(j), 0))
    folded = jax.ShapeDtypeStruct((b, half, FNET_WIDTH), BF16)
    u_plus, u_minus = pl.pallas_call(
        functools.partial(_fold_kernel, nb=b, tf=tf),
        grid=(half // tf,),
        in_specs=[seq(lambda j: j), seq(lambda j: last - j), seq(lambda j: jnp.minimum(last + 1 - j, last))],
        out_specs=[pl.BlockSpec((b, tf, FNET_WIDTH), lambda j: (0, j, 0))] * 2,
        out_shape=[folded, folded],
        compiler_params=_cparams("parallel"),
        name=f"fourier_fold_{n}",
    )(hin, hin, hin)

    cm, sm = _dft_tables(n)
    cc, sc = chan_tabs
    tk = _pick_tile(n, (1024, 512, 256))
    tn = _pick_tile(half, (1024, 512, 256, 128))
    mid_rows = 16
    folded_spec = pl.BlockSpec((b, tn, FNET_WIDTH), lambda i, j: (0, j, 0))
    return pl.pallas_call(
        functools.partial(_dft_kernel, nb=b, tk=tk, scale=1.0 / math.sqrt(n * FNET_GROUP_DIM)),
        grid=(n // tk, half // tn),
        in_specs=[
            pl.BlockSpec((tk, tn), lambda i, j: (i, j)),
            pl.BlockSpec((tk, tn), lambda i, j: (i, j)),
            folded_spec,
            folded_spec,
            pl.BlockSpec((b, mid_rows, FNET_WIDTH), lambda i, j: (0, (row0 + half) // mid_rows, 0)),
            _const_spec((FNET_GROUP_DIM, FNET_GROUP_DIM)),
            _const_spec((FNET_GROUP_DIM, FNET_GROUP_DIM)),
        ],
        out_specs=pl.BlockSpec((b, tk, FNET_WIDTH), lambda i, j: (0, i, 0)),
        out_shape=jax.ShapeDtypeStruct((b, n, FNET_WIDTH), BF16),
        scratch_shapes=[pltpu.VMEM((b, tk, FNET_WIDTH), F32), pltpu.VMEM((b, tk, FNET_WIDTH), F32)],
        compiler_params=_cparams("parallel", "arbitrary"),
        name=f"fourier_seq_{n}",
    )(cm, sm, u_plus, u_minus, hin, cc, sc)


def _log_sigmoid(x):
    return jnp.minimum(x, 0.0) - jnp.log(1.0 + jnp.exp(-jnp.abs(x)))


def _retention_kernel(q_ref, k_ref, v_ref, g_ref, decf_ref, decb_ref, o_ref,
                      of_scr, ob_scr, dec_scr, *, n_lat, n_ctx):
    c = RET_CHUNK
    pos_i = lax.broadcasted_iota(jnp.int32, (c, c), 0).astype(F32)
    pos_j = lax.broadcasted_iota(jnp.int32, (c, c), 1).astype(F32)
    lg_f = _log_sigmoid(decf_ref[0])
    lg_b = _log_sigmoid(decb_ref[0])
    rel = pos_i - pos_j
    mask_f = rel >= 0.0
    mask_b = rel < 0.0
    dec_scr[0] = jnp.where(mask_f, jnp.exp(lg_f * jnp.where(mask_f, rel, 0.0)), 0.0)
    dec_scr[1] = jnp.where(mask_b, jnp.exp(lg_b * jnp.where(mask_b, -rel, 0.0)), 0.0)
    dec_scr[2] = jnp.exp(lg_f * (pos_i + 1.0))
    dec_scr[3] = jnp.exp(lg_b * (c - pos_i))
    dec_scr[4] = jnp.exp(lg_f * (c - 1.0 - pos_i))
    dec_scr[5] = jnp.exp(lg_b * pos_i)
    dec_scr[6] = jnp.exp(lg_f * (c + 0.0 * pos_i))
    dec_scr[7] = jnp.exp(lg_b * (c + 0.0 * pos_i))
    def one_dir(d, start, out_scr, state):
        qc = q_ref[0, pl.ds(start, c), :]
        kc = k_ref[0, pl.ds(start, c), :]
        vc = v_ref[0, pl.ds(start, c), :]
        scores = lax.dot_general(qc, kc, (((1,), (1,)), ((), ())), preferred_element_type=F32)
        intra = jnp.dot((scores * dec_scr[d]).astype(BF16), vc, preferred_element_type=F32)
        inter = jnp.dot(qc, state.astype(BF16), preferred_element_type=F32)
        out_scr[pl.ds(start, c), :] = intra + dec_scr[2 + d] * inter
        vd = (vc.astype(F32) * dec_scr[4 + d]).astype(BF16)
        kv = lax.dot_general(kc, vd, (((0,), (0,)), ((), ())), preferred_element_type=F32)
        return dec_scr[6 + d] * state + kv

    s_f = jnp.zeros((LANES, LANES), F32)
    s_b = jnp.zeros((LANES, LANES), F32)
    n_ctx_chunks = n_ctx // c
    for ci in range(n_ctx_chunks):
        s_f = one_dir(0, n_lat + ci * c, of_scr, s_f)
        s_b = one_dir(1, n_lat + (n_ctx_chunks - 1 - ci) * c, ob_scr, s_b)

    n_lat_chunks = n_lat // c

    def body(i, states):
        s_f = one_dir(0, pl.multiple_of(i * c, c), of_scr, states[0])
        s_b = one_dir(1, pl.multiple_of((n_lat_chunks - 1 - i) * c, c), ob_scr, states[1])
        return s_f, s_b

    lax.fori_loop(0, n_lat_chunks, body, (s_f, s_b), unroll=_pick_tile(n_lat_chunks, (8, 4, 2, 1)))

    def finish(i, carry):
        r0 = pl.multiple_of(i * c, c)
        o = of_scr[pl.ds(r0, c), :] + ob_scr[pl.ds(r0, c), :]
        o = o * lax.rsqrt(jnp.mean(o * o, axis=-1, keepdims=True) + EPS)
        g = g_ref[0, pl.ds(r0, c), :].astype(F32)
        o_ref[0, pl.ds(r0, c), :] = (o * (g * _sigmoid(g))).astype(BF16)
        return carry

    n_chunks = (n_lat + n_ctx) // c
    lax.fori_loop(0, n_chunks, finish, 0, unroll=_pick_tile(n_chunks, (6, 3, 2, 1)))


def _retention(hin, dec_f, dec_b, n_lat):
    b, t, _ = hin.shape
    h = RET_HEADS
    seq = lambda col0: pl.BlockSpec((1, t, LANES), lambda bi, hi: (bi, 0, col0 + hi))
    dec = pl.BlockSpec((1, 1, LANES), lambda bi, hi: (hi, 0, 0))
    return pl.pallas_call(
        functools.partial(_retention_kernel, n_lat=n_lat, n_ctx=t - n_lat),
        grid=(b, h),
        in_specs=[seq(4), seq(8), seq(12), seq(16), dec, dec],
        out_specs=pl.BlockSpec((1, t, LANES), lambda bi, hi: (bi, 0, hi)),
        out_shape=jax.ShapeDtypeStruct((b, t, h * LANES), BF16),
        scratch_shapes=[
            pltpu.VMEM((t, LANES), F32),
            pltpu.VMEM((t, LANES), F32),
            pltpu.VMEM((8, RET_CHUNK, LANES), F32),
        ],
        compiler_params=_cparams("parallel", "parallel"),
        name="retention",
    )(hin, hin, hin, hin, dec_f, dec_b)


def _mixer_residual(x, a_refs, w_refs, modl_ref, modc_ref, row0, n_lat):
    y = jnp.dot(a_refs[0][0], w_refs[0][...], preferred_element_type=F32)
    for a_ref, w_ref in zip(a_refs[1:], w_refs[1:]):
        y = y + jnp.dot(a_ref[0], w_ref[...], preferred_element_type=F32)
    return x + _gate_rows(modl_ref, modc_ref, 2, row0, x.shape[0], n_lat) * y


def _mixer_specs(parts, weights, tm):
    specs = [pl.BlockSpec((1, tm, p.shape[2]), lambda bi, i: (bi, i, 0)) for p in parts]
    return specs + [_resident_spec(w.shape) for w in weights]


def _attn_kernel(prev_ref, q_ref, k_ref, v_ref, lam_ref, sg_ref, o_ref, *, tk, lam_init):
    del prev_ref
    tq = q_ref.shape[1]
    n_keys = k_ref.shape[1]
    q = q_ref[0]
    lane = lax.broadcasted_iota(jnp.int32, (tq, LANES), 1)
    zero = jnp.zeros_like(q)
    q_sub = (jnp.where(lane < HEAD_DIM, q, zero), jnp.where(lane >= HEAD_DIM, q, zero))
    ones = jnp.ones((tk, LANES), BF16)
    m = [None, None]
    acc = [None, None]
    for j in range(n_keys // tk):
        kc = k_ref[0, j * tk:(j + 1) * tk, :]
        v1 = jnp.concatenate([v_ref[0, j * tk:(j + 1) * tk, :], ones], axis=1)
        for sub in range(2):
            s = lax.dot_general(q_sub[sub], kc, (((1,), (1,)), ((), ())), preferred_element_type=F32)
            mx = jnp.max(s, axis=-1, keepdims=True)
            if j == 0:
                m[sub] = mx
                p = jnp.exp2(s - mx).astype(BF16)
                acc[sub] = jnp.dot(p, v1, preferred_element_type=F32)
            else:
                m_new = jnp.maximum(m[sub], mx)
                alpha = jnp.exp2(m[sub] - m_new)
                p = jnp.exp2(s - m_new).astype(BF16)
                acc[sub] = alpha * acc[sub] + jnp.dot(p, v1, preferred_element_type=F32)
                m[sub] = m_new

    lv = lam_ref[...]
    s1 = jnp.sum(lv[0:1] * lv[1:2], axis=-1, keepdims=True)
    s2 = jnp.sum(lv[2:3] * lv[3:4], axis=-1, keepdims=True)
    lam = jnp.exp(s1) - jnp.exp(s2) + lam_init
    o = acc[0][:, :LANES] / acc[0][:, LANES:] - lam * (acc[1][:, :LANES] / acc[1][:, LANES:])
    o = o * lax.rsqrt(jnp.mean(o * o, axis=-1, keepdims=True) + EPS)
    o_ref[0] = (o * sg_ref[...] * (1.0 - lam_init)).astype(BF16)


def _attention(out, qkv, lam_vecs, sub_gain, lam_init, n_lat, ctx_queries):
    b, t, _ = qkv.shape
    h = DIFF_HEADS
    n_ctx = t - n_lat
    if ctx_queries:
        tq, n_q_tiles, q_off = n_ctx, 1, n_lat // n_ctx
        keys, k_off = n_ctx, n_lat // n_ctx
        tk = n_ctx
    else:
        tq = _pick_tile(n_lat, (1024, 512, 256))
        n_q_tiles, q_off = n_lat // tq, 0
        keys, k_off = t, 0
        tk = _pick_tile(t, (768, 256))
    return pl.pallas_call(
        functools.partial(_attn_kernel, tk=tk, lam_init=lam_init),
        grid=(b, h, n_q_tiles),
        in_specs=[
            pl.BlockSpec(memory_space=pl.ANY),
            pl.BlockSpec((1, tq, LANES), lambda bi, hi, i: (bi, q_off + i, hi)),
            pl.BlockSpec((1, keys, LANES), lambda bi, hi, i: (bi, k_off, h + hi)),
            pl.BlockSpec((1, keys, LANES), lambda bi, hi, i: (bi, k_off, 2 * h + hi)),
            _const_spec((4, LANES)),
            _const_spec((1, LANES)),
        ],
        out_specs=pl.BlockSpec((1, tq, LANES), lambda bi, hi, i: (bi, q_off + i, hi)),
        out_shape=jax.ShapeDtypeStruct((b, t, h * LANES), BF16),
        input_output_aliases={0: 0},
        compiler_params=_cparams("parallel", "parallel", "arbitrary"),
        name="diff_attn_ctx" if ctx_queries else "diff_attn_lat",
    )(out, qkv, qkv, qkv, lam_vecs, sub_gain)


def _ffn_kernel(*refs, n_parts, n_lat, tm, tf):
    x_ref, modl_ref, modc_ref = refs[:3]
    a_refs, w_refs = refs[3:3 + n_parts], refs[3 + n_parts:3 + 2 * n_parts]
    w1_ref, w3_ref, w2_ref, o_ref = refs[3 + 2 * n_parts:]
    row0 = pl.program_id(1) * tm
    x = _mixer_residual(x_ref[0], a_refs, w_refs, modl_ref, modc_ref, row0, n_lat)
    h = _modulated(x, modl_ref, modc_ref, 3, row0, n_lat).astype(BF16)
    f_dim = w1_ref.shape[1]
    y = jnp.zeros((tm, x.shape[1]), F32)
    for c in range(f_dim // tf):
        sl = slice(c * tf, (c + 1) * tf)
        a = jnp.dot(h, w1_ref[:, sl], preferred_element_type=F32)
        g = jnp.dot(h, w3_ref[:, sl], preferred_element_type=F32)
        u = (a * _sigmoid(a) * g).astype(BF16)
        y = y + jnp.dot(u, w2_ref[sl, :], preferred_element_type=F32)
    gate = _gate_rows(modl_ref, modc_ref, 5, row0, tm, n_lat)
    o_ref[0] = x + gate * y


def _ffn(x, mod, parts, out_weights, w1, w3, w2, n_lat):
    b, t, d = x.shape
    tm = _pick_tile(t, (768, 256))
    tf = _pick_tile(w1.shape[1], (256, 128))
    return pl.pallas_call(
        functools.partial(_ffn_kernel, n_parts=len(parts), n_lat=n_lat, tm=tm, tf=tf),
        grid=(b, t // tm),
        in_specs=[
            pl.BlockSpec((1, tm, d), lambda bi, i: (bi, i, 0)),
            pl.BlockSpec((1, N_MOD, d), lambda bi, i: (bi, 0, 0)),
            pl.BlockSpec((1, N_MOD, d), lambda bi, i: (b, 0, 0)),
            *_mixer_specs(parts, out_weights, tm),
            _resident_spec(w1.shape),
            _resident_spec(w3.shape),
            _resident_spec(w2.shape),
        ],
        out_specs=pl.BlockSpec((1, tm, d), lambda bi, i: (bi, i, 0)),
        out_shape=jax.ShapeDtypeStruct((b, t, d), F32),
        input_output_aliases={0: 0},
        compiler_params=_cparams("parallel", "parallel"),
        name="mixer_out_ffn",
    )(x, mod, mod, *parts, *out_weights, w1, w3, w2)


ROUTE_E1, ROUTE_E2, ROUTE_G1, ROUTE_G2 = 0, 1, 2, 3
SLABS = D_MODEL // LANES
DMA_LOOP_UNROLL = 8


def _top2_route(logits):
    tm = logits.shape[0]
    lane = lax.broadcasted_iota(jnp.int32, (tm, LANES), 1)
    neg = jnp.float32(-jnp.inf)
    lg = jnp.where(lane < N_EXPERTS, logits, neg)
    m1 = jnp.max(lg, axis=-1, keepdims=True)
    i1 = jnp.min(jnp.where(lg == m1, lane, LANES), axis=-1, keepdims=True)
    lg2 = jnp.where(lane == i1, neg, lg)
    m2 = jnp.max(lg2, axis=-1, keepdims=True)
    i2 = jnp.min(jnp.where(lg2 == m2, lane, LANES), axis=-1, keepdims=True)
    e = jnp.exp(m2 - m1)
    g1 = 1.0 / (1.0 + e)
    g2 = e / (1.0 + e)
    rec = jnp.where(lane == ROUTE_E1, i1.astype(F32), 0.0) + jnp.where(lane == ROUTE_E2, i2.astype(F32), 0.0)
    return rec + jnp.where(lane == ROUTE_G1, g1, 0.0) + jnp.where(lane == ROUTE_G2, g2, 0.0)


def _slab_store(ref, val, n_rows):
    for s in range(SLABS):
        ref[pl.ds(s, n_rows, stride=SLABS), :] = val[:, s * LANES:(s + 1) * LANES]


def _slab_load(ref, n_rows):
    return jnp.concatenate([ref[pl.ds(s, n_rows, stride=SLABS), :] for s in range(SLABS)], axis=1)


def _route_kernel(*refs, n_parts, n_lat, tm):
    x_ref, modl_ref, modc_ref = refs[:3]
    a_refs, w_refs = refs[3:3 + n_parts], refs[3 + n_parts:3 + 2 * n_parts]
    wr_ref, xo_ref, h_ref, rec_ref = refs[3 + 2 * n_parts:]
    row0 = pl.program_id(1) * tm
    x = _mixer_residual(x_ref[0], a_refs, w_refs, modl_ref, modc_ref, row0, n_lat)
    xo_ref[0] = x
    hf = _modulated(x, modl_ref, modc_ref, 3, row0, n_lat)
    logits = jnp.dot(hf, wr_ref[...], preferred_element_type=F32, precision=lax.Precision.HIGHEST)
    rec_ref[0] = _top2_route(logits)
    _slab_store(h_ref, hf, tm)


def _moe_route(x, mod, parts, out_weights, w_router, n_lat):
    b, t, d = x.shape
    tm = _pick_tile(t, (768, 256))
    tiles = t // tm
    return pl.pallas_call(
        functools.partial(_route_kernel, n_parts=len(parts), n_lat=n_lat, tm=tm),
        grid=(b, tiles),
        in_specs=[
            pl.BlockSpec((1, tm, d), lambda bi, i: (bi, i, 0)),
            pl.BlockSpec((1, N_MOD, d), lambda bi, i: (bi, 0, 0)),
            pl.BlockSpec((1, N_MOD, d), lambda bi, i: (b, 0, 0)),
            *_mixer_specs(parts, out_weights, tm),
            _const_spec((d, LANES)),
        ],
        out_specs=[
            pl.BlockSpec((1, tm, d), lambda bi, i: (bi, i, 0)),
            pl.BlockSpec((tm * SLABS, LANES), lambda bi, i: (bi * tiles + i, 0)),
            pl.BlockSpec((1, tm, LANES), lambda bi, i: (bi, i, 0)),
        ],
        out_shape=[
            jax.ShapeDtypeStruct((b, t, d), F32),
            jax.ShapeDtypeStruct((b * t * SLABS, LANES), F32),
            jax.ShapeDtypeStruct((b, t, LANES), F32),
        ],
        input_output_aliases={0: 0},
        compiler_params=_cparams("parallel", "parallel"),
        name="mixer_out_route",
    )(x, mod, mod, *parts, *out_weights, w_router)


def _row_copy(src_ref, src_row, dst_ref, dst_row, sem):
    return pltpu.make_async_copy(
        src_ref.at[pl.ds(pl.multiple_of(src_row * SLABS, SLABS), SLABS), :],
        dst_ref.at[pl.ds(pl.multiple_of(dst_row * SLABS, SLABS), SLABS), :],
        sem)


def _dispatch_kernel(dest_ref, h_ref, init_ref, xs_ref, sem, *, tm):
    del init_ref

    def start(r, carry):
        for slot in range(2):
            _row_copy(h_ref, r, xs_ref, dest_ref[0, 0, 2 * r + slot], sem).start()
        return carry

    def wait(r, carry):
        for slot in range(2):
            _row_copy(h_ref, r, xs_ref, dest_ref[0, 0, 2 * r + slot], sem).wait()
        return carry

    lax.fori_loop(0, tm, start, 0, unroll=DMA_LOOP_UNROLL)
    lax.fori_loop(0, tm, wait, 0, unroll=DMA_LOOP_UNROLL)


def _moe_dispatch(h_slabs, dest, n_sorted_rows):
    n_tok = h_slabs.shape[0] // SLABS
    tm = _pick_tile(n_tok, (512, 256))
    steps = n_tok // tm
    return pl.pallas_call(
        functools.partial(_dispatch_kernel, tm=tm),
        grid=(steps,),
        in_specs=[
            pl.BlockSpec((1, 1, 2 * tm), lambda i: (i, 0, 0), memory_space=pltpu.SMEM),
            pl.BlockSpec((tm * SLABS, LANES), lambda i: (i, 0)),
            pl.BlockSpec(memory_space=pl.ANY),
        ],
        out_specs=pl.BlockSpec(memory_space=pl.ANY),
        out_shape=jax.ShapeDtypeStruct((n_sorted_rows * SLABS, LANES), F32),
        scratch_shapes=[pltpu.SemaphoreType.DMA(())],
        input_output_aliases={2: 0},
        compiler_params=_cparams("arbitrary"),
        name="moe_dispatch",
    )(dest.reshape(steps, 1, 2 * tm), h_slabs, jnp.zeros((n_sorted_rows * SLABS, LANES), F32))


def _experts_kernel(te_ref, used_ref, x_ref, w1_ref, w3_ref, w2_ref, o_ref, acc_scr, *, tm, tf, n_steps_f):
    del te_ref
    i = pl.program_id(0)
    f = pl.program_id(1)
    valid = i < used_ref[0]
    n_f = pl.num_programs(1)

    def step(first, last):
        h = _slab_load(x_ref, tm).astype(BF16)
        f_blk = w1_ref.shape[2]
        y = None if first else acc_scr[...]
        for c in range(f_blk // tf):
            sl = slice(c * tf, (c + 1) * tf)
            a = jnp.dot(h, w1_ref[0, :, sl], preferred_element_type=F32)
            g = jnp.dot(h, w3_ref[0, :, sl], preferred_element_type=F32)
            u = (a * _sigmoid(a) * g).astype(BF16)
            part = jnp.dot(u, w2_ref[0, sl, :], preferred_element_type=F32)
            y = part if y is None else y + part
        if last:
            _slab_store(o_ref, y, tm)
        else:
            acc_scr[...] = y

    if n_steps_f == 1:
        pl.when(valid)(lambda: step(True, True))
    else:
        pl.when(valid & (f == 0))(lambda: step(True, False))
        pl.when(valid & (f == n_f - 1))(lambda: step(False, True))
        if n_steps_f > 2:
            pl.when(valid & (f > 0) & (f < n_f - 1))(lambda: step(False, False))

    @pl.when(jnp.logical_not(valid) & (f == n_f - 1))
    def _():
        _slab_store(o_ref, jnp.zeros(acc_scr.shape, F32), tm)


def _moe_experts(xs, tile_expert, n_used, w1, w3, w2, tm):
    n_exp, d, f_dim = w1.shape
    n_tiles = xs.shape[0] // (tm * SLABS)
    f_blk = _pick_tile(f_dim, (1792, 512, 256))
    tf = _pick_tile(f_blk, (256, 128))
    n_f = f_dim // f_blk

    def f_eff(i, f, used):
        return jnp.where(i < used[0], f, n_f - 1)

    grid_spec = pltpu.PrefetchScalarGridSpec(
        num_scalar_prefetch=2,
        grid=(n_tiles, n_f),
        in_specs=[
            pl.BlockSpec((tm * SLABS, LANES), lambda i, f, te, used: (jnp.minimum(i, used[0] - 1), 0)),
            pl.BlockSpec((1, d, f_blk), lambda i, f, te, used: (te[i], 0, f_eff(i, f, used))),
            pl.BlockSpec((1, d, f_blk), lambda i, f, te, used: (te[i], 0, f_eff(i, f, used))),
            pl.BlockSpec((1, f_blk, d), lambda i, f, te, used: (te[i], f_eff(i, f, used), 0)),
        ],
        out_specs=pl.BlockSpec((tm * SLABS, LANES), lambda i, f, te, used: (i, 0)),
        scratch_shapes=[pltpu.VMEM((tm, d), F32)],
    )
    return pl.pallas_call(
        functools.partial(_experts_kernel, tm=tm, tf=tf, n_steps_f=n_f),
        grid_spec=grid_spec,
        out_shape=jax.ShapeDtypeStruct(xs.shape, F32),
        compiler_params=_cparams("arbitrary", "arbitrary"),
        name="moe_experts",
    )(tile_expert, n_used, xs, w1, w3, w2)


def _combine_kernel(dcur_ref, dnext_ref, x_ref, modl_ref, modc_ref, rec_ref, ys_ref, o_ref, buf, sems,
                    *, n_lat, tm, tiles_per_batch):
    i = pl.program_id(0)
    n = pl.num_programs(0)

    def gather(dest_ref, slot_buf, op):
        def body(r, carry):
            for slot in range(2):
                cp = _row_copy(ys_ref, dest_ref[0, 0, 2 * r + slot], buf.at[slot_buf, slot], r, sems.at[slot_buf])
                cp.start() if op == "start" else cp.wait()
            return carry
        lax.fori_loop(0, tm, body, 0, unroll=DMA_LOOP_UNROLL)

    @pl.when(i == 0)
    def _():
        gather(dcur_ref, 0, "start")

    for parity in range(2):
        @pl.when((i + 1 < n) & ((i + 1) % 2 == parity))
        def _():
            gather(dnext_ref, parity, "start")

    for parity in range(2):
        @pl.when(i % 2 == parity)
        def _():
            gather(dcur_ref, parity, "wait")
            rec = rec_ref[0]
            y = (rec[:, ROUTE_G1:ROUTE_G1 + 1] * _slab_load(buf.at[parity, 0], tm)
                 + rec[:, ROUTE_G2:ROUTE_G2 + 1] * _slab_load(buf.at[parity, 1], tm))
            row0 = (i % tiles_per_batch) * tm
            gate = _gate_rows(modl_ref, modc_ref, 5, row0, tm, n_lat)
            o_ref[0] = x_ref[0] + gate * y


def _moe_combine(x, mod, rec, ys, dest, n_lat):
    b, t, d = x.shape
    tm = _pick_tile(t, (256,))
    tpb = t // tm
    n_tiles = b * tpb
    dest3 = dest.reshape(n_tiles, 1, 2 * tm)
    tok = lambda i: (i // tpb, i % tpb, 0)
    return pl.pallas_call(
        functools.partial(_combine_kernel, n_lat=n_lat, tm=tm, tiles_per_batch=tpb),
        grid=(n_tiles,),
        in_specs=[
            pl.BlockSpec((1, 1, 2 * tm), lambda i: (i, 0, 0), memory_space=pltpu.SMEM),
            pl.BlockSpec((1, 1, 2 * tm), lambda i: (jnp.minimum(i + 1, n_tiles - 1), 0, 0),
                         memory_space=pltpu.SMEM),
            pl.BlockSpec((1, tm, d), tok),
            pl.BlockSpec((1, N_MOD, d), lambda i: (i // tpb, 0, 0)),
            pl.BlockSpec((1, N_MOD, d), lambda i: (b, 0, 0)),
            pl.BlockSpec((1, tm, LANES), tok),
            pl.BlockSpec(memory_space=pl.ANY),
        ],
        out_specs=pl.BlockSpec((1, tm, d), tok),
        out_shape=jax.ShapeDtypeStruct((b, t, d), F32),
        scratch_shapes=[pltpu.VMEM((2, 2, tm * SLABS, LANES), F32), pltpu.SemaphoreType.DMA((2,))],
        input_output_aliases={2: 0},
        compiler_params=_cparams("arbitrary"),
        name="moe_combine",
    )(dest3, dest3, x, mod, mod, rec, ys)


def _dispatch_plan(rec, tm, n_tiles):
    experts = rec[:, ROUTE_E1:ROUTE_E2 + 1].astype(jnp.int32).reshape(-1)
    onehot = (experts[:, None] == jnp.arange(N_EXPERTS, dtype=jnp.int32)[None, :]).astype(jnp.int32)
    csum = jnp.cumsum(onehot, axis=0)
    counts = csum[-1]
    padded = (counts + tm - 1) // tm * tm
    ends = jnp.cumsum(padded)
    dest = jnp.sum(onehot * (csum - 1 + (ends - padded)[None, :]), axis=1)
    n_used = (ends[-1] // tm).astype(jnp.int32)
    tile_start = jnp.arange(n_tiles, dtype=jnp.int32) * tm
    tile_expert = jnp.sum((tile_start[:, None] >= ends[None, :]).astype(jnp.int32), axis=1)
    tile_expert = jnp.minimum(tile_expert, tile_expert[n_used - 1])
    return dest.astype(jnp.int32), tile_expert.astype(jnp.int32), n_used.reshape(1)


def _moe(x, mod, parts, out_weights, w_router, w1, w3, w2, first_expert, n_lat):
    b, t, d = x.shape
    n_tok = b * t
    tm = 512
    n_tiles = (2 * n_tok + N_EXPERTS * (tm - 1)) // tm
    x, h_slabs, rec = _moe_route(x, mod, parts, out_weights, w_router, n_lat)
    dest, tile_expert, n_used = _dispatch_plan(rec.reshape(n_tok, LANES), tm, n_tiles)
    xs = _moe_dispatch(h_slabs, dest, n_tiles * tm)
    ys = _moe_experts(xs, tile_expert + first_expert, n_used, w1, w3, w2, tm)
    return _moe_combine(x, mod, rec, ys, dest, n_lat)


def _rot_cols(w):
    half = HEAD_DIM // 2
    return jnp.concatenate([-w[..., half:], w[..., :half]], axis=-1)


def _with_rot(w_heads):
    d, n, _ = w_heads.shape
    return jnp.concatenate([w_heads, _rot_cols(w_heads)], axis=-1).reshape(d, n * LANES)


def _gain_lanes(gain):
    return jnp.tile(gain.astype(F32), LANES // HEAD_DIM).reshape(1, LANES)


def _head_sum_matrix():
    idx = jnp.arange(LANES, dtype=jnp.int32) // HEAD_DIM
    return (idx[:, None] == idx[None, :]).astype(BF16)


def _rope_table(n_lat, n_ctx):
    rows = n_lat // GRID_W
    row = jnp.repeat(jnp.arange(rows, dtype=F32), GRID_W)
    col = jnp.tile(jnp.arange(GRID_W, dtype=F32), rows)
    quarter = HEAD_DIM // 4
    inv_freq = ROPE_BASE ** (-jnp.arange(quarter, dtype=F32) / quarter)
    ang = jnp.concatenate([row[:, None] * inv_freq, col[:, None] * inv_freq], axis=-1)
    cos, sin = jnp.cos(ang), jnp.sin(ang)
    ones = jnp.ones((n_ctx, HEAD_DIM), F32)
    zeros = jnp.zeros((n_ctx, HEAD_DIM), F32)
    rows_of = lambda lat, ctx: jnp.concatenate([lat, ctx], axis=0)
    dup = rows_of(jnp.concatenate([cos, cos, sin, sin], axis=-1), jnp.concatenate([ones, zeros], axis=-1))
    cos_packed = rows_of(jnp.concatenate([cos] * 4, axis=-1), jnp.concatenate([ones, ones], axis=-1))
    sin_packed = rows_of(jnp.concatenate([-sin, sin, -sin, sin], axis=-1), jnp.concatenate([zeros, zeros], axis=-1))
    return dup, cos_packed, sin_packed


def _chan_tables():
    idx = jnp.arange(FNET_GROUP_DIM, dtype=jnp.int32)
    ang = ((idx[:, None] * idx[None, :]) % FNET_GROUP_DIM).astype(F32) * (2.0 * math.pi / FNET_GROUP_DIM)
    return jnp.cos(ang).astype(BF16), jnp.sin(ang).astype(BF16)


def kernel(x, c, ctx, c_ctx, w_mod, b_mod, w_in_even, w_out_even, ret_decay_fwd, ret_decay_bwd,
           ffn_w1, ffn_w3, ffn_w2, w_in_odd, w_out_odd, q_norm_gain, k_norm_gain,
           lambda_q1, lambda_k1, lambda_q2, lambda_k2, subln_gain, w_router, moe_w1, moe_w3, moe_w2):
    b, n_lat, d = x.shape
    n_ctx = ctx.shape[1]
    depth = w_mod.shape[0]
    assert d == D_MODEL and b < MOD_ROWS and n_lat % n_ctx == 0 and n_ctx % RET_CHUNK == 0

    stream = jnp.concatenate([x, ctx], axis=1)
    cond = jnp.zeros((MOD_ROWS, d), F32).at[:b].set(c).at[b].set(c_ctx)
    mods = _adaln(cond, w_mod, b_mod).reshape(depth, MOD_ROWS, N_MOD, d)
    tab, cos_packed, sin_packed = _rope_table(n_lat, n_ctx)
    chan_tabs = _chan_tables()
    stack_experts = lambda w: w.astype(BF16).reshape((-1,) + w.shape[2:])
    moe_w1_all, moe_w3_all, moe_w2_all = stack_experts(moe_w1), stack_experts(moe_w3), stack_experts(moe_w2)

    for layer in range(depth):
        i = layer // 2
        mod = mods[layer]
        if layer % 2 == 0:
            w = w_in_even[i]
            hq = RET_HEADS * HEAD_DIM
            wq = w[:, FNET_WIDTH:FNET_WIDTH + hq].reshape(d, RET_HEADS, HEAD_DIM)
            wk = w[:, FNET_WIDTH + hq:FNET_WIDTH + 2 * hq].reshape(d, RET_HEADS, HEAD_DIM)
            w_in = jnp.concatenate(
                [w[:, :FNET_WIDTH], _with_rot(wq), _with_rot(wk), w[:, FNET_WIDTH + 2 * hq:]], axis=1).astype(BF16)
            hin = _inproj(stream, mod, w_in, [tab], [], n_lat, w_in.shape[1], True)
            four = jnp.concatenate(
                [_fourier_seq(hin, 0, n_lat, chan_tabs), _fourier_seq(hin, n_lat, n_ctx, chan_tabs)], axis=1)
            dec_f = jnp.broadcast_to(ret_decay_fwd[i].astype(F32)[:, None, None], (RET_HEADS, 1, LANES))
            dec_b = jnp.broadcast_to(ret_decay_bwd[i].astype(F32)[:, None, None], (RET_HEADS, 1, LANES))
            ret = _retention(hin, dec_f, dec_b, n_lat)
            w_out = w_out_even[i].astype(BF16)
            stream = _ffn(stream, mod, [four, ret], [w_out[:FNET_WIDTH], w_out[FNET_WIDTH:]],
                          ffn_w1[i].astype(BF16), ffn_w3[i].astype(BF16), ffn_w2[i].astype(BF16), n_lat)
        else:
            consts = [_gain_lanes(q_norm_gain[i]), _gain_lanes(k_norm_gain[i]), _head_sum_matrix()]
            qkv = _inproj(stream, mod, w_in_odd[i].astype(BF16), [cos_packed, sin_packed], consts, n_lat,
                          3 * DIFF_HEADS * LANES, False)
            lam_init = 0.8 - 0.6 * float(np.exp(-0.3 * layer))
            lam_vecs = jnp.zeros((4, LANES), F32).at[:, :HEAD_DIM].set(
                jnp.stack([lambda_q1[i], lambda_k1[i], lambda_q2[i], lambda_k2[i]]).astype(F32))
            sub_gain = subln_gain[i].astype(F32).reshape(1, LANES)
            att = jnp.zeros((b, n_lat + n_ctx, DIFF_HEADS * LANES), BF16)
            att = _attention(att, qkv, lam_vecs, sub_gain, lam_init, n_lat, False)
            att = _attention(att, qkv, lam_vecs, sub_gain, lam_init, n_lat, True)
            wr = jnp.zeros((d, LANES), F32).at[:, :N_EXPERTS].set(w_router[i].astype(F32))
            stream = _moe(stream, mod, [att], [w_out_odd[i].astype(BF16)], wr,
                          moe_w1_all, moe_w3_all, moe_w2_all, i * N_EXPERTS, n_lat)
    return stream[:, :n_lat]
```

```python
import functools
import math

import jax
import jax.numpy as jnp
import numpy as np
from jax import lax
from jax.experimental import pallas as pl
from jax.experimental.pallas import tpu as pltpu

F32 = jnp.float32
BF16 = jnp.bfloat16

D_MODEL = 1024
GRID_W = 64
HEAD_DIM = 64
LANES = 128
FNET_GROUPS = 4
FNET_GROUP_DIM = 128
FNET_WIDTH = FNET_GROUPS * FNET_GROUP_DIM
RET_HEADS = 4
RET_CHUNK = 128
DIFF_HEADS = 8
N_EXPERTS = 8
N_MOD = 6
ROPE_BASE = 10000.0
EPS = 1e-6
LOG2_E = math.log2(math.e)
VMEM_LIMIT_BYTES = 56 * 1024 * 1024
MOD_ROWS = 8


def _cparams(*sem):
    return pltpu.CompilerParams(dimension_semantics=sem, vmem_limit_bytes=VMEM_LIMIT_BYTES)


def _pick_tile(n, candidates):
    for c in candidates:
        if n % c == 0:
            return c
    raise ValueError(f"no tile in {candidates} divides {n}")


def _const_spec(shape):
    nd = len(shape)
    return pl.BlockSpec(shape, lambda *_: (0,) * nd)


def _resident_spec(shape):
    nd = len(shape)
    return pl.BlockSpec(shape, lambda *_: (0,) * nd, pipeline_mode=pl.Buffered(1))


def _sigmoid(x):
    return 1.0 / (1.0 + jnp.exp(-x))


def _modulated(x, modl_ref, modc_ref, k_shift, row0, n_lat):
    tm = x.shape[0]
    ms = jnp.mean(x * x, axis=-1, keepdims=True)
    xn = x * lax.rsqrt(ms + EPS)
    rows = row0 + lax.broadcasted_iota(jnp.int32, (tm, 1), 0)
    is_ctx = rows >= n_lat
    shift = jnp.where(is_ctx, modc_ref[0, k_shift:k_shift + 1, :], modl_ref[0, k_shift:k_shift + 1, :])
    scale = jnp.where(is_ctx, modc_ref[0, k_shift + 1:k_shift + 2, :], modl_ref[0, k_shift + 1:k_shift + 2, :])
    return xn * (1.0 + scale) + shift


def _gate_rows(modl_ref, modc_ref, k_gate, row0, tm, n_lat):
    rows = row0 + lax.broadcasted_iota(jnp.int32, (tm, 1), 0)
    return jnp.where(rows >= n_lat, modc_ref[0, k_gate:k_gate + 1, :], modl_ref[0, k_gate:k_gate + 1, :])


def _adaln_kernel(c_ref, w_ref, b_ref, o_ref):
    c = c_ref[...]
    s = c * _sigmoid(c)
    o_ref[0] = jnp.dot(s, w_ref[0], preferred_element_type=F32, precision=lax.Precision.HIGHEST) + b_ref[0]


def _adaln(cond_rows, w_mod, b_mod):
    depth, d, n = w_mod.shape
    tn = _pick_tile(n, (1536, 1024, 512))
    return pl.pallas_call(
        _adaln_kernel,
        grid=(depth, n // tn),
        in_specs=[
            _const_spec((MOD_ROWS, d)),
            pl.BlockSpec((1, d, tn), lambda l, j: (l, 0, j)),
            pl.BlockSpec((1, 1, tn), lambda l, j: (l, 0, j)),
        ],
        out_specs=pl.BlockSpec((1, MOD_ROWS, tn), lambda l, j: (l, 0, j)),
        out_shape=jax.ShapeDtypeStruct((depth, MOD_ROWS, n), F32),
        compiler_params=_cparams("parallel", "parallel"),
        name="adaln",
    )(cond_rows, w_mod, b_mod.reshape(depth, 1, n))


def _rope_dup(blk, tab):
    t = blk * tab
    return t + pltpu.roll(t, HEAD_DIM, axis=1)


def _inproj_even_kernel(x_ref, modl_ref, modc_ref, w_ref, tab_ref, o_ref, *, n_lat, tm):
    row0 = pl.program_id(1) * tm
    h = _modulated(x_ref[0], modl_ref, modc_ref, 0, row0, n_lat).astype(BF16)
    tab = tab_ref[...]
    lane = lax.broadcasted_iota(jnp.int32, (tm, LANES), 1)
    n_out = o_ref.shape[2]
    for c in range(n_out // 256):
        y = jnp.dot(h, w_ref[:, c * 256:(c + 1) * 256], preferred_element_type=F32)
        for half in range(2):
            blk = y[:, half * LANES:(half + 1) * LANES]
            col = c * 2 + half
            if 4 <= col < 8:
                blk = _rope_dup(blk, tab)
            elif 8 <= col < 12:
                blk = jnp.where(lane < HEAD_DIM, _rope_dup(blk, tab) * (HEAD_DIM ** -0.5), 0.0)
            o_ref[0, :, col * LANES:(col + 1) * LANES] = blk.astype(BF16)


def _swap_halves(y, first_half):
    fwd = pltpu.roll(y, HEAD_DIM // 2, axis=1)
    bwd = pltpu.roll(y, LANES - HEAD_DIM // 2, axis=1)
    return jnp.where(first_half, bwd, fwd)


def _inproj_odd_kernel(x_ref, modl_ref, modc_ref, w_ref, cos_ref, sin_ref, gq_ref, gk_ref, seg_ref, o_ref,
                       *, n_lat, tm):
    row0 = pl.program_id(1) * tm
    h = _modulated(x_ref[0], modl_ref, modc_ref, 0, row0, n_lat).astype(BF16)
    cos = cos_ref[...]
    sin = sin_ref[...]
    seg = seg_ref[...]
    lane = lax.broadcasted_iota(jnp.int32, (tm, LANES), 1)
    first_half = lane % HEAD_DIM < HEAD_DIM // 2
    gains = (gq_ref[...] * (HEAD_DIM ** -0.5 * LOG2_E), gk_ref[...])
    for c in range(w_ref.shape[1] // 256):
        y = jnp.dot(h, w_ref[:, c * 256:(c + 1) * 256], preferred_element_type=F32)
        part = c // (DIFF_HEADS // 2)
        if part < 2:
            for half in range(2):
                blk = y[:, half * LANES:(half + 1) * LANES]
                ms = jnp.dot((blk * blk).astype(BF16), seg, preferred_element_type=F32) * (1.0 / HEAD_DIM)
                yn = blk * lax.rsqrt(ms + EPS) * gains[part]
                out = yn * cos + _swap_halves(yn, first_half) * sin
                o_ref[0, :, (2 * c + half) * LANES:(2 * c + half + 1) * LANES] = out.astype(BF16)
        else:
            o_ref[0, :, c * 256:(c + 1) * 256] = y.astype(BF16)


def _inproj(x, mod, w, row_tabs, consts, n_lat, n_out, even):
    b, t, d = x.shape
    tm = _pick_tile(t, (768, 256))
    kern = _inproj_even_kernel if even else _inproj_odd_kernel
    in_specs = [
        pl.BlockSpec((1, tm, d), lambda bi, i: (bi, i, 0)),
        pl.BlockSpec((1, N_MOD, d), lambda bi, i: (bi, 0, 0)),
        pl.BlockSpec((1, N_MOD, d), lambda bi, i: (b, 0, 0)),
        _resident_spec(w.shape),
    ]
    in_specs += [pl.BlockSpec((tm, LANES), lambda bi, i: (i, 0)) for _ in row_tabs]
    in_specs += [_const_spec(c.shape) for c in consts]
    return pl.pallas_call(
        functools.partial(kern, n_lat=n_lat, tm=tm),
        grid=(b, t // tm),
        in_specs=in_specs,
        out_specs=pl.BlockSpec((1, tm, n_out), lambda bi, i: (bi, i, 0)),
        out_shape=jax.ShapeDtypeStruct((b, t, n_out), BF16),
        compiler_params=_cparams("parallel", "parallel"),
        name="inproj_even" if even else "inproj_odd",
    )(x, mod, mod, w, *row_tabs, *consts)


def _fold_kernel(ud_ref, ua_ref, ub_ref, p_ref, m_ref, *, nb, tf):
    j = pl.program_id(0)
    row = lax.broadcasted_iota(jnp.int32, (tf, tf), 0)
    col = lax.broadcasted_iota(jnp.int32, (tf, tf), 1)
    flip = ((row + col == tf) & (row >= 1)).astype(BF16)
    corner = ((row == 0) & (col == 0) & (j > 0)).astype(BF16)
    for bi in range(nb):
        mirrored = (jnp.dot(flip, ua_ref[bi], preferred_element_type=F32)
                    + jnp.dot(corner, ub_ref[bi], preferred_element_type=F32))
        ud = ud_ref[bi].astype(F32)
        p_ref[bi] = (ud + mirrored).astype(BF16)
        m_ref[bi] = (ud - mirrored).astype(BF16)


def _dft_kernel(c_ref, s_ref, p_ref, m_ref, mid_ref, cc_ref, sc_ref, o_ref, acc_c, acc_s, *, nb, tk, scale):
    i = pl.program_id(0)
    j = pl.program_id(1)

    @pl.when(j == 0)
    def _():
        acc_c[...] = jnp.zeros_like(acc_c)
        acc_s[...] = jnp.zeros_like(acc_s)

    cm = c_ref[...]
    sm = s_ref[...]
    for bi in range(nb):
        acc_c[bi] += jnp.dot(cm, p_ref[bi], preferred_element_type=F32)
        acc_s[bi] += jnp.dot(sm, m_ref[bi], preferred_element_type=F32)

    @pl.when(j == pl.num_programs(1) - 1)
    def _():
        cc = cc_ref[...]
        sc = sc_ref[...]
        k = i * tk + lax.broadcasted_iota(jnp.int32, (tk, 1), 0)
        sign = (1 - 2 * (k & 1)).astype(F32)
        for bi in range(nb):
            mid = mid_ref[bi, 0:1, :].astype(F32)
            for g in range(FNET_GROUPS):
                sl = slice(g * FNET_GROUP_DIM, (g + 1) * FNET_GROUP_DIM)
                a = (acc_c[bi, :, sl] + sign * mid[:, sl]).astype(BF16)
                bm = acc_s[bi, :, sl].astype(BF16)
                y = jnp.dot(a, cc, preferred_element_type=F32) - jnp.dot(bm, sc, preferred_element_type=F32)
                o_ref[bi, :, sl] = (y * scale).astype(BF16)


def _angle_tables(n_rows, n_cols, stride, period):
    k = jnp.arange(n_rows, dtype=jnp.int32)[:, None]
    n = jnp.arange(n_cols, dtype=jnp.int32)[None, :]
    ang = ((k * n * stride) % period).astype(F32) * (2.0 * math.pi / period)
    return jnp.cos(ang), jnp.sin(ang)


def _dft_tables(n):
    ca, sa = _angle_tables(n // LANES, n // 2, LANES, n)
    cb, sb = _angle_tables(LANES, n // 2, 1, n)
    ca, sa, cb, sb = ca[:, None, :], sa[:, None, :], cb[None, :, :], sb[None, :, :]
    cm = (ca * cb - sa * sb).astype(BF16).reshape(n, n // 2)
    sm = (sa * cb + ca * sb).astype(BF16).reshape(n, n // 2)
    return cm, sm


def _fourier_seq(hin, row0, n, chan_tabs):
    b = hin.shape[0]
    half = n // 2
    tf = _pick_tile(half, (256, 128))
    off_f, last = row0 // tf, n // tf - 1
    seq = lambda index: pl.BlockSpec((b, tf, FNET_WIDTH), lambda j: (0, off_f + index(j), 0))
    folded = jax.ShapeDtypeStruct((b, half, FNET_WIDTH), BF16)
    u_plus, u_minus = pl.pallas_call(
        functools.partial(_fold_kernel, nb=b, tf=tf),
        grid=(half // tf,),
        in_specs=[seq(lambda j: j), seq(lambda j: last - j), seq(lambda j: jnp.minimum(last + 1 - j, last))],
        out_specs=[pl.BlockSpec((b, tf, FNET_WIDTH), lambda j: (0, j, 0))] * 2,
        out_shape=[folded, folded],
        compiler_params=_cparams("parallel"),
        name=f"fourier_fold_{n}",
    )(hin, hin, hin)

    cm, sm = _dft_tables(n)
    cc, sc = chan_tabs
    tk = _pick_tile(n, (1024, 512, 256))
    tn = _pick_tile(half, (1024, 512, 256, 128))
    mid_rows = 16
    folded_spec = pl.BlockSpec((b, tn, FNET_WIDTH), lambda i, j: (0, j, 0))
    return pl.pallas_call(
        functools.partial(_dft_kernel, nb=b, tk=tk, scale=1.0 / math.sqrt(n * FNET_GROUP_DIM)),
        grid=(n // tk, half // tn),
        in_specs=[
            pl.BlockSpec((tk, tn), lambda i, j: (i, j)),
            pl.BlockSpec((tk, tn), lambda i, j: (i, j)),
            folded_spec,
            folded_spec,
            pl.BlockSpec((b, mid_rows, FNET_WIDTH), lambda i, j: (0, (row0 + half) // mid_rows, 0)),
            _const_spec((FNET_GROUP_DIM, FNET_GROUP_DIM)),
            _const_spec((FNET_GROUP_DIM, FNET_GROUP_DIM)),
        ],
        out_specs=pl.BlockSpec((b, tk, FNET_WIDTH), lambda i, j: (0, i, 0)),
        out_shape=jax.ShapeDtypeStruct((b, n, FNET_WIDTH), BF16),
        scratch_shapes=[pltpu.VMEM((b, tk, FNET_WIDTH), F32), pltpu.VMEM((b, tk, FNET_WIDTH), F32)],
        compiler_params=_cparams("parallel", "arbitrary"),
        name=f"fourier_seq_{n}",
    )(cm, sm, u_plus, u_minus, hin, cc, sc)


def _log_sigmoid(x):
    return jnp.minimum(x, 0.0) - jnp.log(1.0 + jnp.exp(-jnp.abs(x)))


def _retention_kernel(q_ref, k_ref, v_ref, g_ref, decf_ref, decb_ref, o_ref,
                      of_scr, ob_scr, dec_scr, *, n_lat, n_ctx):
    c = RET_CHUNK
    pos_i = lax.broadcasted_iota(jnp.int32, (c, c), 0).astype(F32)
    pos_j = lax.broadcasted_iota(jnp.int32, (c, c), 1).astype(F32)
    lg_f = _log_sigmoid(decf_ref[0])
    lg_b = _log_sigmoid(decb_ref[0])
    rel = pos_i - pos_j
    mask_f = rel >= 0.0
    mask_b = rel < 0.0
    dec_scr[0] = jnp.where(mask_f, jnp.exp(lg_f * jnp.where(mask_f, rel, 0.0)), 0.0)
    dec_scr[1] = jnp.where(mask_b, jnp.exp(lg_b * jnp.where(mask_b, -rel, 0.0)), 0.0)
    dec_scr[2] = jnp.exp(lg_f * (pos_i + 1.0))
    dec_scr[3] = jnp.exp(lg_b * (c - pos_i))
    dec_scr[4] = jnp.exp(lg_f * (c - 1.0 - pos_i))
    dec_scr[5] = jnp.exp(lg_b * pos_i)
    dec_scr[6] = jnp.exp(lg_f * (c + 0.0 * pos_i))
    dec_scr[7] = jnp.exp(lg_b * (c + 0.0 * pos_i))
    def one_dir(d, start, out_scr, state):
        qc = q_ref[0, pl.ds(start, c), :]
        kc = k_ref[0, pl.ds(start, c), :]
        vc = v_ref[0, pl.ds(start, c), :]
        scores = lax.dot_general(qc, kc, (((1,), (1,)), ((), ())), preferred_element_type=F32)
        intra = jnp.dot((scores * dec_scr[d]).astype(BF16), vc, preferred_element_type=F32)
        inter = jnp.dot(qc, state.astype(BF16), preferred_element_type=F32)
        out_scr[pl.ds(start, c), :] = intra + dec_scr[2 + d] * inter
        vd = (vc.astype(F32) * dec_scr[4 + d]).astype(BF16)
        kv = lax.dot_general(kc, vd, (((0,), (0,)), ((), ())), preferred_element_type=F32)
        return dec_scr[6 + d] * state + kv

    s_f = jnp.zeros((LANES, LANES), F32)
    s_b = jnp.zeros((LANES, LANES), F32)
    n_ctx_chunks = n_ctx // c
    for ci in range(n_ctx_chunks):
        s_f = one_dir(0, n_lat + ci * c, of_scr, s_f)
        s_b = one_dir(1, n_lat + (n_ctx_chunks - 1 - ci) * c, ob_scr, s_b)

    n_lat_chunks = n_lat // c

    def body(i, states):
        s_f = one_dir(0, pl.multiple_of(i * c, c), of_scr, states[0])
        s_b = one_dir(1, pl.multiple_of((n_lat_chunks - 1 - i) * c, c), ob_scr, states[1])
        return s_f, s_b

    lax.fori_loop(0, n_lat_chunks, body, (s_f, s_b), unroll=_pick_tile(n_lat_chunks, (8, 4, 2, 1)))

    def finish(i, carry):
        r0 = pl.multiple_of(i * c, c)
        o = of_scr[pl.ds(r0, c), :] + ob_scr[pl.ds(r0, c), :]
        o = o * lax.rsqrt(jnp.mean(o * o, axis=-1, keepdims=True) + EPS)
        g = g_ref[0, pl.ds(r0, c), :].astype(F32)
        o_ref[0, pl.ds(r0, c), :] = (o * (g * _sigmoid(g))).astype(BF16)
        return carry

    n_chunks = (n_lat + n_ctx) // c
    lax.fori_loop(0, n_chunks, finish, 0, unroll=_pick_tile(n_chunks, (6, 3, 2, 1)))


def _retention(hin, dec_f, dec_b, n_lat):
    b, t, _ = hin.shape
    h = RET_HEADS
    seq = lambda col0: pl.BlockSpec((1, t, LANES), lambda bi, hi: (bi, 0, col0 + hi))
    dec = pl.BlockSpec((1, 1, LANES), lambda bi, hi: (hi, 0, 0))
    return pl.pallas_call(
        functools.partial(_retention_kernel, n_lat=n_lat, n_ctx=t - n_lat),
        grid=(b, h),
        in_specs=[seq(4), seq(8), seq(12), seq(16), dec, dec],
        out_specs=pl.BlockSpec((1, t, LANES), lambda bi, hi: (bi, 0, hi)),
        out_shape=jax.ShapeDtypeStruct((b, t, h * LANES), BF16),
        scratch_shapes=[
            pltpu.VMEM((t, LANES), F32),
            pltpu.VMEM((t, LANES), F32),
            pltpu.VMEM((8, RET_CHUNK, LANES), F32),
        ],
        compiler_params=_cparams("parallel", "parallel"),
        name="retention",
    )(hin, hin, hin, hin, dec_f, dec_b)


def _mixer_residual(x, a_refs, w_refs, modl_ref, modc_ref, row0, n_lat):
    y = jnp.dot(a_refs[0][0], w_refs[0][...], preferred_element_type=F32)
    for a_ref, w_ref in zip(a_refs[1:], w_refs[1:]):
        y = y + jnp.dot(a_ref[0], w_ref[...], preferred_element_type=F32)
    return x + _gate_rows(modl_ref, modc_ref, 2, row0, x.shape[0], n_lat) * y


def _mixer_specs(parts, weights, tm):
    specs = [pl.BlockSpec((1, tm, p.shape[2]), lambda bi, i: (bi, i, 0)) for p in parts]
    return specs + [_resident_spec(w.shape) for w in weights]


def _attn_kernel(prev_ref, q_ref, k_ref, v_ref, lam_ref, sg_ref, o_ref, *, tk, lam_init):
    del prev_ref
    tq = q_ref.shape[1]
    n_keys = k_ref.shape[1]
    q = q_ref[0]
    lane = lax.broadcasted_iota(jnp.int32, (tq, LANES), 1)
    zero = jnp.zeros_like(q)
    q_sub = (jnp.where(lane < HEAD_DIM, q, zero), jnp.where(lane >= HEAD_DIM, q, zero))
    ones = jnp.ones((tk, LANES), BF16)
    m = [None, None]
    acc = [None, None]
    for j in range(n_keys // tk):
        kc = k_ref[0, j * tk:(j + 1) * tk, :]
        v1 = jnp.concatenate([v_ref[0, j * tk:(j + 1) * tk, :], ones], axis=1)
        for sub in range(2):
            s = lax.dot_general(q_sub[sub], kc, (((1,), (1,)), ((), ())), preferred_element_type=F32)
            mx = jnp.max(s, axis=-1, keepdims=True)
            if j == 0:
                m[sub] = mx
                p = jnp.exp2(s - mx).astype(BF16)
                acc[sub] = jnp.dot(p, v1, preferred_element_type=F32)
            else:
                m_new = jnp.maximum(m[sub], mx)
                alpha = jnp.exp2(m[sub] - m_new)
                p = jnp.exp2(s - m_new).astype(BF16)
                acc[sub] = alpha * acc[sub] + jnp.dot(p, v1, preferred_element_type=F32)
                m[sub] = m_new

    lv = lam_ref[...]
    s1 = jnp.sum(lv[0:1] * lv[1:2], axis=-1, keepdims=True)
    s2 = jnp.sum(lv[2:3] * lv[3:4], axis=-1, keepdims=True)
    lam = jnp.exp(s1) - jnp.exp(s2) + lam_init
    o = acc[0][:, :LANES] / acc[0][:, LANES:] - lam * (acc[1][:, :LANES] / acc[1][:, LANES:])
    o = o * lax.rsqrt(jnp.mean(o * o, axis=-1, keepdims=True) + EPS)
    o_ref[0] = (o * sg_ref[...] * (1.0 - lam_init)).astype(BF16)


def _attention(out, qkv, lam_vecs, sub_gain, lam_init, n_lat, ctx_queries):
    b, t, _ = qkv.shape
    h = DIFF_HEADS
    n_ctx = t - n_lat
    if ctx_queries:
        tq, n_q_tiles, q_off = n_ctx, 1, n_lat // n_ctx
        keys, k_off = n_ctx, n_lat // n_ctx
        tk = n_ctx
    else:
        tq = _pick_tile(n_lat, (1024, 512, 256))
        n_q_tiles, q_off = n_lat // tq, 0
        keys, k_off = t, 0
        tk = _pick_tile(t, (768, 256))
    return pl.pallas_call(
        functools.partial(_attn_kernel, tk=tk, lam_init=lam_init),
        grid=(b, h, n_q_tiles),
        in_specs=[
            pl.BlockSpec(memory_space=pl.ANY),
            pl.BlockSpec((1, tq, LANES), lambda bi, hi, i: (bi, q_off + i, hi)),
            pl.BlockSpec((1, keys, LANES), lambda bi, hi, i: (bi, k_off, h + hi)),
            pl.BlockSpec((1, keys, LANES), lambda bi, hi, i: (bi, k_off, 2 * h + hi)),
            _const_spec((4, LANES)),
            _const_spec((1, LANES)),
        ],
        out_specs=pl.BlockSpec((1, tq, LANES), lambda bi, hi, i: (bi, q_off + i, hi)),
        out_shape=jax.ShapeDtypeStruct((b, t, h * LANES), BF16),
        input_output_aliases={0: 0},
        compiler_params=_cparams("parallel", "parallel", "arbitrary"),
        name="diff_attn_ctx" if ctx_queries else "diff_attn_lat",
    )(out, qkv, qkv, qkv, lam_vecs, sub_gain)


def _ffn_kernel(*refs, n_parts, n_lat, tm, tf):
    x_ref, modl_ref, modc_ref = refs[:3]
    a_refs, w_refs = refs[3:3 + n_parts], refs[3 + n_parts:3 + 2 * n_parts]
    w1_ref, w3_ref, w2_ref, o_ref = refs[3 + 2 * n_parts:]
    row0 = pl.program_id(1) * tm
    x = _mixer_residual(x_ref[0], a_refs, w_refs, modl_ref, modc_ref, row0, n_lat)
    h = _modulated(x, modl_ref, modc_ref, 3, row0, n_lat).astype(BF16)
    f_dim = w1_ref.shape[1]
    y = jnp.zeros((tm, x.shape[1]), F32)
    for c in range(f_dim // tf):
        sl = slice(c * tf, (c + 1) * tf)
        a = jnp.dot(h, w1_ref[:, sl], preferred_element_type=F32)
        g = jnp.dot(h, w3_ref[:, sl], preferred_element_type=F32)
        u = (a * _sigmoid(a) * g).astype(BF16)
        y = y + jnp.dot(u, w2_ref[sl, :], preferred_element_type=F32)
    gate = _gate_rows(modl_ref, modc_ref, 5, row0, tm, n_lat)
    o_ref[0] = x + gate * y


def _ffn(x, mod, parts, out_weights, w1, w3, w2, n_lat):
    b, t, d = x.shape
    tm = _pick_tile(t, (768, 256))
    tf = _pick_tile(w1.shape[1], (256, 128))
    return pl.pallas_call(
        functools.partial(_ffn_kernel, n_parts=len(parts), n_lat=n_lat, tm=tm, tf=tf),
        grid=(b, t // tm),
        in_specs=[
            pl.BlockSpec((1, tm, d), lambda bi, i: (bi, i, 0)),
            pl.BlockSpec((1, N_MOD, d), lambda bi, i: (bi, 0, 0)),
            pl.BlockSpec((1, N_MOD, d), lambda bi, i: (b, 0, 0)),
            *_mixer_specs(parts, out_weights, tm),
            _resident_spec(w1.shape),
            _resident_spec(w3.shape),
            _resident_spec(w2.shape),
        ],
        out_specs=pl.BlockSpec((1, tm, d), lambda bi, i: (bi, i, 0)),
        out_shape=jax.ShapeDtypeStruct((b, t, d), F32),
        input_output_aliases={0: 0},
        compiler_params=_cparams("parallel", "parallel"),
        name="mixer_out_ffn",
    )(x, mod, mod, *parts, *out_weights, w1, w3, w2)


ROUTE_E1, ROUTE_E2, ROUTE_G1, ROUTE_G2 = 0, 1, 2, 3
SLABS = D_MODEL // LANES
DMA_LOOP_UNROLL = 8


def _top2_route(logits):
    tm = logits.shape[0]
    lane = lax.broadcasted_iota(jnp.int32, (tm, LANES), 1)
    neg = jnp.float32(-jnp.inf)
    lg = jnp.where(lane < N_EXPERTS, logits, neg)
    m1 = jnp.max(lg, axis=-1, keepdims=True)
    i1 = jnp.min(jnp.where(lg == m1, lane, LANES), axis=-1, keepdims=True)
    lg2 = jnp.where(lane == i1, neg, lg)
    m2 = jnp.max(lg2, axis=-1, keepdims=True)
    i2 = jnp.min(jnp.where(lg2 == m2, lane, LANES), axis=-1, keepdims=True)
    e = jnp.exp(m2 - m1)
    g1 = 1.0 / (1.0 + e)
    g2 = e / (1.0 + e)
    rec = jnp.where(lane == ROUTE_E1, i1.astype(F32), 0.0) + jnp.where(lane == ROUTE_E2, i2.astype(F32), 0.0)
    return rec + jnp.where(lane == ROUTE_G1, g1, 0.0) + jnp.where(lane == ROUTE_G2, g2, 0.0)


def _slab_store(ref, val, n_rows):
    for s in range(SLABS):
        ref[pl.ds(s, n_rows, stride=SLABS), :] = val[:, s * LANES:(s + 1) * LANES]


def _slab_load(ref, n_rows):
    return jnp.concatenate([ref[pl.ds(s, n_rows, stride=SLABS), :] for s in range(SLABS)], axis=1)


def _mixer_out_kernel(*refs, n_parts, n_lat, tm):
    x_ref, modl_ref, modc_ref = refs[:3]
    a_refs, w_refs = refs[3:3 + n_parts], refs[3 + n_parts:3 + 2 * n_parts]
    o_ref = refs[3 + 2 * n_parts]
    row0 = pl.program_id(1) * tm
    o_ref[0] = _mixer_residual(x_ref[0], a_refs, w_refs, modl_ref, modc_ref, row0, n_lat)


def _mixer_out(x, mod, parts, out_weights, n_lat):
    b, t, d = x.shape
    tm = _pick_tile(t, (768, 256))
    return pl.pallas_call(
        functools.partial(_mixer_out_kernel, n_parts=len(parts), n_lat=n_lat, tm=tm),
        grid=(b, t // tm),
        in_specs=[
            pl.BlockSpec((1, tm, d), lambda bi, i: (bi, i, 0)),
            pl.BlockSpec((1, N_MOD, d), lambda bi, i: (bi, 0, 0)),
            pl.BlockSpec((1, N_MOD, d), lambda bi, i: (b, 0, 0)),
            *_mixer_specs(parts, out_weights, tm),
        ],
        out_specs=pl.BlockSpec((1, tm, d), lambda bi, i: (bi, i, 0)),
        out_shape=jax.ShapeDtypeStruct((b, t, d), F32),
        input_output_aliases={0: 0},
        compiler_params=_cparams("parallel", "parallel"),
        name="mixer_out",
    )(x, mod, mod, *parts, *out_weights)


def _route_kernel(x_ref, modl_ref, modc_ref, wr_ref, h_ref, rec_ref, *, n_lat, tm):
    row0 = pl.program_id(1) * tm
    hf = _modulated(x_ref[0], modl_ref, modc_ref, 3, row0, n_lat)
    logits = jnp.dot(hf, wr_ref[...], preferred_element_type=F32, precision=lax.Precision.HIGHEST)
    rec_ref[0] = _top2_route(logits)
    _slab_store(h_ref, hf, tm)


def _moe_route(x, mod, w_router, n_lat):
    b, t, d = x.shape
    tm = _pick_tile(t, (768, 256))
    tiles = t // tm
    return pl.pallas_call(
        functools.partial(_route_kernel, n_lat=n_lat, tm=tm),
        grid=(b, tiles),
        in_specs=[
            pl.BlockSpec((1, tm, d), lambda bi, i: (bi, i, 0)),
            pl.BlockSpec((1, N_MOD, d), lambda bi, i: (bi, 0, 0)),
            pl.BlockSpec((1, N_MOD, d), lambda bi, i: (b, 0, 0)),
            _const_spec((d, LANES)),
        ],
        out_specs=[
            pl.BlockSpec((tm * SLABS, LANES), lambda bi, i: (bi * tiles + i, 0)),
            pl.BlockSpec((1, tm, LANES), lambda bi, i: (bi, i, 0)),
        ],
        out_shape=[
            jax.ShapeDtypeStruct((b * t * SLABS, LANES), F32),
            jax.ShapeDtypeStruct((b, t, LANES), F32),
        ],
        compiler_params=_cparams("parallel", "parallel"),
        name="moe_route",
    )(x, mod, mod, w_router)


def _row_copy(src_ref, src_row, dst_ref, dst_row, sem):
    return pltpu.make_async_copy(
        src_ref.at[pl.ds(pl.multiple_of(src_row * SLABS, SLABS), SLABS), :],
        dst_ref.at[pl.ds(pl.multiple_of(dst_row * SLABS, SLABS), SLABS), :],
        sem)


def _dispatch_kernel(dest_ref, h_ref, init_ref, xs_ref, sem, *, tm):
    del init_ref

    def start(r, carry):
        for slot in range(2):
            _row_copy(h_ref, r, xs_ref, dest_ref[0, 0, 2 * r + slot], sem).start()
        return carry

    def wait(r, carry):
        for slot in range(2):
            _row_copy(h_ref, r, xs_ref, dest_ref[0, 0, 2 * r + slot], sem).wait()
        return carry

    lax.fori_loop(0, tm, start, 0, unroll=DMA_LOOP_UNROLL)
    lax.fori_loop(0, tm, wait, 0, unroll=DMA_LOOP_UNROLL)


def _moe_dispatch(h_slabs, dest, n_sorted_rows):
    n_tok = h_slabs.shape[0] // SLABS
    tm = _pick_tile(n_tok, (512, 256))
    steps = n_tok // tm
    return pl.pallas_call(
        functools.partial(_dispatch_kernel, tm=tm),
        grid=(steps,),
        in_specs=[
            pl.BlockSpec((1, 1, 2 * tm), lambda i: (i, 0, 0), memory_space=pltpu.SMEM),
            pl.BlockSpec((tm * SLABS, LANES), lambda i: (i, 0)),
            pl.BlockSpec(memory_space=pl.ANY),
        ],
        out_specs=pl.BlockSpec(memory_space=pl.ANY),
        out_shape=jax.ShapeDtypeStruct((n_sorted_rows * SLABS, LANES), F32),
        scratch_shapes=[pltpu.SemaphoreType.DMA(())],
        input_output_aliases={2: 0},
        compiler_params=_cparams("arbitrary"),
        name="moe_dispatch",
    )(dest.reshape(steps, 1, 2 * tm), h_slabs, jnp.zeros((n_sorted_rows * SLABS, LANES), F32))


def _experts_kernel(te_ref, used_ref, x_ref, w1_ref, w3_ref, w2_ref, o_ref, acc_scr, *, tm, tf, n_steps_f):
    del te_ref
    i = pl.program_id(0)
    f = pl.program_id(1)
    valid = i < used_ref[0]
    n_f = pl.num_programs(1)

    def step(first, last):
        h = _slab_load(x_ref, tm).astype(BF16)
        f_blk = w1_ref.shape[2]
        y = None if first else acc_scr[...]
        for c in range(f_blk // tf):
            sl = slice(c * tf, (c + 1) * tf)
            a = jnp.dot(h, w1_ref[0, :, sl], preferred_element_type=F32)
            g = jnp.dot(h, w3_ref[0, :, sl], preferred_element_type=F32)
            u = (a * _sigmoid(a) * g).astype(BF16)
            part = jnp.dot(u, w2_ref[0, sl, :], preferred_element_type=F32)
            y = part if y is None else y + part
        if last:
            _slab_store(o_ref, y, tm)
        else:
            acc_scr[...] = y

    if n_steps_f == 1:
        pl.when(valid)(lambda: step(True, True))
    else:
        pl.when(valid & (f == 0))(lambda: step(True, False))
        pl.when(valid & (f == n_f - 1))(lambda: step(False, True))
        if n_steps_f > 2:
            pl.when(valid & (f > 0) & (f < n_f - 1))(lambda: step(False, False))

    @pl.when(jnp.logical_not(valid) & (f == n_f - 1))
    def _():
        _slab_store(o_ref, jnp.zeros(acc_scr.shape, F32), tm)


def _moe_experts(xs, tile_expert, n_used, w1, w3, w2, tm):
    n_exp, d, f_dim = w1.shape
    n_tiles = xs.shape[0] // (tm * SLABS)
    f_blk = _pick_tile(f_dim, (1792, 512, 256))
    tf = _pick_tile(f_blk, (256, 128))
    n_f = f_dim // f_blk

    def f_eff(i, f, used):
        return jnp.where(i < used[0], f, n_f - 1)

    grid_spec = pltpu.PrefetchScalarGridSpec(
        num_scalar_prefetch=2,
        grid=(n_tiles, n_f),
        in_specs=[
            pl.BlockSpec((tm * SLABS, LANES), lambda i, f, te, used: (jnp.minimum(i, used[0] - 1), 0)),
            pl.BlockSpec((1, d, f_blk), lambda i, f, te, used: (te[i], 0, f_eff(i, f, used))),
            pl.BlockSpec((1, d, f_blk), lambda i, f, te, used: (te[i], 0, f_eff(i, f, used))),
            pl.BlockSpec((1, f_blk, d), lambda i, f, te, used: (te[i], f_eff(i, f, used), 0)),
        ],
        out_specs=pl.BlockSpec((tm * SLABS, LANES), lambda i, f, te, used: (i, 0)),
        scratch_shapes=[pltpu.VMEM((tm, d), F32)],
    )
    return pl.pallas_call(
        functools.partial(_experts_kernel, tm=tm, tf=tf, n_steps_f=n_f),
        grid_spec=grid_spec,
        out_shape=jax.ShapeDtypeStruct(xs.shape, F32),
        compiler_params=_cparams("arbitrary", "arbitrary"),
        name="moe_experts",
    )(tile_expert, n_used, xs, w1, w3, w2)


def _combine_kernel(dcur_ref, dnext_ref, x_ref, modl_ref, modc_ref, rec_ref, ys_ref, o_ref, buf, sems,
                    *, n_lat, tm, tiles_per_batch):
    i = pl.program_id(0)
    n = pl.num_programs(0)

    def gather(dest_ref, slot_buf, op):
        def body(r, carry):
            for slot in range(2):
                cp = _row_copy(ys_ref, dest_ref[0, 0, 2 * r + slot], buf.at[slot_buf, slot], r, sems.at[slot_buf])
                cp.start() if op == "start" else cp.wait()
            return carry
        lax.fori_loop(0, tm, body, 0, unroll=DMA_LOOP_UNROLL)

    @pl.when(i == 0)
    def _():
        gather(dcur_ref, 0, "start")

    for parity in range(2):
        @pl.when((i + 1 < n) & ((i + 1) % 2 == parity))
        def _():
            gather(dnext_ref, parity, "start")

    for parity in range(2):
        @pl.when(i % 2 == parity)
        def _():
            gather(dcur_ref, parity, "wait")
            rec = rec_ref[0]
            y = (rec[:, ROUTE_G1:ROUTE_G1 + 1] * _slab_load(buf.at[parity, 0], tm)
                 + rec[:, ROUTE_G2:ROUTE_G2 + 1] * _slab_load(buf.at[parity, 1], tm))
            row0 = (i % tiles_per_batch) * tm
            gate = _gate_rows(modl_ref, modc_ref, 5, row0, tm, n_lat)
            o_ref[0] = x_ref[0] + gate * y


def _moe_combine(x, mod, rec, ys, dest, n_lat):
    b, t, d = x.shape
    tm = _pick_tile(t, (256,))
    tpb = t // tm
    n_tiles = b * tpb
    dest3 = dest.reshape(n_tiles, 1, 2 * tm)
    tok = lambda i: (i // tpb, i % tpb, 0)
    return pl.pallas_call(
        functools.partial(_combine_kernel, n_lat=n_lat, tm=tm, tiles_per_batch=tpb),
        grid=(n_tiles,),
        in_specs=[
            pl.BlockSpec((1, 1, 2 * tm), lambda i: (i, 0, 0), memory_space=pltpu.SMEM),
            pl.BlockSpec((1, 1, 2 * tm), lambda i: (jnp.minimum(i + 1, n_tiles - 1), 0, 0),
                         memory_space=pltpu.SMEM),
            pl.BlockSpec((1, tm, d), tok),
            pl.BlockSpec((1, N_MOD, d), lambda i: (i // tpb, 0, 0)),
            pl.BlockSpec((1, N_MOD, d), lambda i: (b, 0, 0)),
            pl.BlockSpec((1, tm, LANES), tok),
            pl.BlockSpec(memory_space=pl.ANY),
        ],
        out_specs=pl.BlockSpec((1, tm, d), tok),
        out_shape=jax.ShapeDtypeStruct((b, t, d), F32),
        scratch_shapes=[pltpu.VMEM((2, 2, tm * SLABS, LANES), F32), pltpu.SemaphoreType.DMA((2,))],
        input_output_aliases={2: 0},
        compiler_params=_cparams("arbitrary"),
        name="moe_combine",
    )(dest3, dest3, x, mod, mod, rec, ys)


def _dispatch_plan(rec, tm, n_tiles):
    experts = rec[:, ROUTE_E1:ROUTE_E2 + 1].astype(jnp.int32).reshape(-1)
    onehot = (experts[:, None] == jnp.arange(N_EXPERTS, dtype=jnp.int32)[None, :]).astype(jnp.int32)
    csum = jnp.cumsum(onehot, axis=0)
    counts = csum[-1]
    padded = (counts + tm - 1) // tm * tm
    ends = jnp.cumsum(padded)
    dest = jnp.sum(onehot * (csum - 1 + (ends - padded)[None, :]), axis=1)
    n_used = (ends[-1] // tm).astype(jnp.int32)
    tile_start = jnp.arange(n_tiles, dtype=jnp.int32) * tm
    tile_expert = jnp.sum((tile_start[:, None] >= ends[None, :]).astype(jnp.int32), axis=1)
    tile_expert = jnp.minimum(tile_expert, tile_expert[n_used - 1])
    return dest.astype(jnp.int32), tile_expert.astype(jnp.int32), n_used.reshape(1)


def _moe(x, mod, w_router, w1, w3, w2, first_expert, n_lat):
    b, t, d = x.shape
    n_tok = b * t
    tm = 512
    n_tiles = (2 * n_tok + N_EXPERTS * (tm - 1)) // tm
    h_slabs, rec = _moe_route(x, mod, w_router, n_lat)
    dest, tile_expert, n_used = _dispatch_plan(rec.reshape(n_tok, LANES), tm, n_tiles)
    xs = _moe_dispatch(h_slabs, dest, n_tiles * tm)
    ys = _moe_experts(xs, tile_expert + first_expert, n_used, w1, w3, w2, tm)
    return _moe_combine(x, mod, rec, ys, dest, n_lat)


def _rot_cols(w):
    half = HEAD_DIM // 2
    return jnp.concatenate([-w[..., half:], w[..., :half]], axis=-1)


def _with_rot(w_heads):
    d, n, _ = w_heads.shape
    return jnp.concatenate([w_heads, _rot_cols(w_heads)], axis=-1).reshape(d, n * LANES)


def _gain_lanes(gain):
    return jnp.tile(gain.astype(F32), LANES // HEAD_DIM).reshape(1, LANES)


def _head_sum_matrix():
    idx = jnp.arange(LANES, dtype=jnp.int32) // HEAD_DIM
    return (idx[:, None] == idx[None, :]).astype(BF16)


def _rope_table(n_lat, n_ctx):
    rows = n_lat // GRID_W
    row = jnp.repeat(jnp.arange(rows, dtype=F32), GRID_W)
    col = jnp.tile(jnp.arange(GRID_W, dtype=F32), rows)
    quarter = HEAD_DIM // 4
    inv_freq = ROPE_BASE ** (-jnp.arange(quarter, dtype=F32) / quarter)
    ang = jnp.concatenate([row[:, None] * inv_freq, col[:, None] * inv_freq], axis=-1)
    cos, sin = jnp.cos(ang), jnp.sin(ang)
    ones = jnp.ones((n_ctx, HEAD_DIM), F32)
    zeros = jnp.zeros((n_ctx, HEAD_DIM), F32)
    rows_of = lambda lat, ctx: jnp.concatenate([lat, ctx], axis=0)
    dup = rows_of(jnp.concatenate([cos, cos, sin, sin], axis=-1), jnp.concatenate([ones, zeros], axis=-1))
    cos_packed = rows_of(jnp.concatenate([cos] * 4, axis=-1), jnp.concatenate([ones, ones], axis=-1))
    sin_packed = rows_of(jnp.concatenate([-sin, sin, -sin, sin], axis=-1), jnp.concatenate([zeros, zeros], axis=-1))
    return dup, cos_packed, sin_packed


def _chan_tables():
    idx = jnp.arange(FNET_GROUP_DIM, dtype=jnp.int32)
    ang = ((idx[:, None] * idx[None, :]) % FNET_GROUP_DIM).astype(F32) * (2.0 * math.pi / FNET_GROUP_DIM)
    return jnp.cos(ang).astype(BF16), jnp.sin(ang).astype(BF16)


def kernel(x, c, ctx, c_ctx, w_mod, b_mod, w_in_even, w_out_even, ret_decay_fwd, ret_decay_bwd,
           ffn_w1, ffn_w3, ffn_w2, w_in_odd, w_out_odd, q_norm_gain, k_norm_gain,
           lambda_q1, lambda_k1, lambda_q2, lambda_k2, subln_gain, w_router, moe_w1, moe_w3, moe_w2):
    b, n_lat, d = x.shape
    n_ctx = ctx.shape[1]
    depth = w_mod.shape[0]
    assert d == D_MODEL and b < MOD_ROWS and n_lat % n_ctx == 0 and n_ctx % RET_CHUNK == 0

    stream = jnp.concatenate([x, ctx], axis=1)
    cond = jnp.zeros((MOD_ROWS, d), F32).at[:b].set(c).at[b].set(c_ctx)
    mods = _adaln(cond, w_mod, b_mod).reshape(depth, MOD_ROWS, N_MOD, d)
    tab, cos_packed, sin_packed = _rope_table(n_lat, n_ctx)
    chan_tabs = _chan_tables()
    stack_experts = lambda w: w.astype(BF16).reshape((-1,) + w.shape[2:])
    moe_w1_all, moe_w3_all, moe_w2_all = stack_experts(moe_w1), stack_experts(moe_w3), stack_experts(moe_w2)

    for layer in range(depth):
        i = layer // 2
        mod = mods[layer]
        if layer % 2 == 0:
            w = w_in_even[i]
            hq = RET_HEADS * HEAD_DIM
            wq = w[:, FNET_WIDTH:FNET_WIDTH + hq].reshape(d, RET_HEADS, HEAD_DIM)
            wk = w[:, FNET_WIDTH + hq:FNET_WIDTH + 2 * hq].reshape(d, RET_HEADS, HEAD_DIM)
            w_in = jnp.concatenate(
                [w[:, :FNET_WIDTH], _with_rot(wq), _with_rot(wk), w[:, FNET_WIDTH + 2 * hq:]], axis=1).astype(BF16)
            hin = _inproj(stream, mod, w_in, [tab], [], n_lat, w_in.shape[1], True)
            four = jnp.concatenate(
                [_fourier_seq(hin, 0, n_lat, chan_tabs), _fourier_seq(hin, n_lat, n_ctx, chan_tabs)], axis=1)
            dec_f = jnp.broadcast_to(ret_decay_fwd[i].astype(F32)[:, None, None], (RET_HEADS, 1, LANES))
            dec_b = jnp.broadcast_to(ret_decay_bwd[i].astype(F32)[:, None, None], (RET_HEADS, 1, LANES))
            ret = _retention(hin, dec_f, dec_b, n_lat)
            w_out = w_out_even[i].astype(BF16)
            stream = _ffn(stream, mod, [four, ret], [w_out[:FNET_WIDTH], w_out[FNET_WIDTH:]],
                          ffn_w1[i].astype(BF16), ffn_w3[i].astype(BF16), ffn_w2[i].astype(BF16), n_lat)
        else:
            consts = [_gain_lanes(q_norm_gain[i]), _gain_lanes(k_norm_gain[i]), _head_sum_matrix()]
            qkv = _inproj(stream, mod, w_in_odd[i].astype(BF16), [cos_packed, sin_packed], consts, n_lat,
                          3 * DIFF_HEADS * LANES, False)
            lam_init = 0.8 - 0.6 * float(np.exp(-0.3 * layer))
            lam_vecs = jnp.zeros((4, LANES), F32).at[:, :HEAD_DIM].set(
                jnp.stack([lambda_q1[i], lambda_k1[i], lambda_q2[i], lambda_k2[i]]).astype(F32))
            sub_gain = subln_gain[i].astype(F32).reshape(1, LANES)
            att = jnp.zeros((b, n_lat + n_ctx, DIFF_HEADS * LANES), BF16)
            att = _attention(att, qkv, lam_vecs, sub_gain, lam_init, n_lat, False)
            att = _attention(att, qkv, lam_vecs, sub_gain, lam_init, n_lat, True)
            wr = jnp.zeros((d, LANES), F32).at[:, :N_EXPERTS].set(w_router[i].astype(F32))
            stream = _mixer_out(stream, mod, [att], [w_out_odd[i].astype(BF16)], n_lat)
            stream = _moe(stream, mod, wr, moe_w1_all, moe_w3_all, moe_w2_all, i * N_EXPERTS, n_lat)
    return stream[:, :n_lat]
```

```python
import functools
import math

import jax
import jax.numpy as jnp
import numpy as np
from jax import lax
from jax.experimental import pallas as pl
from jax.experimental.pallas import tpu as pltpu

F32 = jnp.float32
BF16 = jnp.bfloat16

D_MODEL = 1024
GRID_W = 64
HEAD_DIM = 64
LANES = 128
FNET_GROUPS = 4
FNET_GROUP_DIM = 128
FNET_WIDTH = FNET_GROUPS * FNET_GROUP_DIM
RET_HEADS = 4
RET_CHUNK = 128
DIFF_HEADS = 8
N_EXPERTS = 8
N_MOD = 6
ROPE_BASE = 10000.0
EPS = 1e-6
LOG2_E = math.log2(math.e)
VMEM_LIMIT_BYTES = 56 * 1024 * 1024
MOD_ROWS = 8


def _cparams(*sem):
    return pltpu.CompilerParams(dimension_semantics=sem, vmem_limit_bytes=VMEM_LIMIT_BYTES)


def _pick_tile(n, candidates):
    for c in candidates:
        if n % c == 0:
            return c
    raise ValueError(f"no tile in {candidates} divides {n}")


def _const_spec(shape):
    nd = len(shape)
    return pl.BlockSpec(shape, lambda *_: (0,) * nd)


def _resident_spec(shape):
    nd = len(shape)
    return pl.BlockSpec(shape, lambda *_: (0,) * nd, pipeline_mode=pl.Buffered(1))


def _sigmoid(x):
    return 1.0 / (1.0 + jnp.exp(-x))


def _modulated(x, modl_ref, modc_ref, k_shift, row0, n_lat):
    tm = x.shape[0]
    ms = jnp.mean(x * x, axis=-1, keepdims=True)
    xn = x * lax.rsqrt(ms + EPS)
    rows = row0 + lax.broadcasted_iota(jnp.int32, (tm, 1), 0)
    is_ctx = rows >= n_lat
    shift = jnp.where(is_ctx, modc_ref[0, k_shift:k_shift + 1, :], modl_ref[0, k_shift:k_shift + 1, :])
    scale = jnp.where(is_ctx, modc_ref[0, k_shift + 1:k_shift + 2, :], modl_ref[0, k_shift + 1:k_shift + 2, :])
    return xn * (1.0 + scale) + shift


def _gate_rows(modl_ref, modc_ref, k_gate, row0, tm, n_lat):
    rows = row0 + lax.broadcasted_iota(jnp.int32, (tm, 1), 0)
    return jnp.where(rows >= n_lat, modc_ref[0, k_gate:k_gate + 1, :], modl_ref[0, k_gate:k_gate + 1, :])


def _adaln_kernel(c_ref, w_ref, b_ref, o_ref):
    c = c_ref[...]
    s = c * _sigmoid(c)
    o_ref[0] = jnp.dot(s, w_ref[0], preferred_element_type=F32, precision=lax.Precision.HIGHEST) + b_ref[0]


def _adaln(cond_rows, w_mod, b_mod):
    depth, d, n = w_mod.shape
    tn = _pick_tile(n, (1536, 1024, 512))
    return pl.pallas_call(
        _adaln_kernel,
        grid=(depth, n // tn),
        in_specs=[
            _const_spec((MOD_ROWS, d)),
            pl.BlockSpec((1, d, tn), lambda l, j: (l, 0, j)),
            pl.BlockSpec((1, 1, tn), lambda l, j: (l, 0, j)),
        ],
        out_specs=pl.BlockSpec((1, MOD_ROWS, tn), lambda l, j: (l, 0, j)),
        out_shape=jax.ShapeDtypeStruct((depth, MOD_ROWS, n), F32),
        compiler_params=_cparams("parallel", "parallel"),
        name="adaln",
    )(cond_rows, w_mod, b_mod.reshape(depth, 1, n))


def _rope_dup(blk, tab):
    t = blk * tab
    return t + pltpu.roll(t, HEAD_DIM, axis=1)


def _inproj_even_kernel(x_ref, modl_ref, modc_ref, w_ref, tab_ref, o_ref, *, n_lat, tm):
    row0 = pl.program_id(1) * tm
    h = _modulated(x_ref[0], modl_ref, modc_ref, 0, row0, n_lat).astype(BF16)
    tab = tab_ref[...]
    lane = lax.broadcasted_iota(jnp.int32, (tm, LANES), 1)
    n_out = o_ref.shape[2]
    for c in range(n_out // 256):
        y = jnp.dot(h, w_ref[:, c * 256:(c + 1) * 256], preferred_element_type=F32)
        for half in range(2):
            blk = y[:, half * LANES:(half + 1) * LANES]
            col = c * 2 + half
            if 4 <= col < 8:
                blk = _rope_dup(blk, tab)
            elif 8 <= col < 12:
                blk = jnp.where(lane < HEAD_DIM, _rope_dup(blk, tab) * (HEAD_DIM ** -0.5), 0.0)
            o_ref[0, :, col * LANES:(col + 1) * LANES] = blk.astype(BF16)


def _swap_halves(y, first_half):
    fwd = pltpu.roll(y, HEAD_DIM // 2, axis=1)
    bwd = pltpu.roll(y, LANES - HEAD_DIM // 2, axis=1)
    return jnp.where(first_half, bwd, fwd)


def _inproj_odd_kernel(x_ref, modl_ref, modc_ref, w_ref, cos_ref, sin_ref, gq_ref, gk_ref, seg_ref, o_ref,
                       *, n_lat, tm):
    row0 = pl.program_id(1) * tm
    h = _modulated(x_ref[0], modl_ref, modc_ref, 0, row0, n_lat).astype(BF16)
    cos = cos_ref[...]
    sin = sin_ref[...]
    seg = seg_ref[...]
    lane = lax.broadcasted_iota(jnp.int32, (tm, LANES), 1)
    first_half = lane % HEAD_DIM < HEAD_DIM // 2
    gains = (gq_ref[...] * (HEAD_DIM ** -0.5 * LOG2_E), gk_ref[...])
    for c in range(w_ref.shape[1] // 256):
        y = jnp.dot(h, w_ref[:, c * 256:(c + 1) * 256], preferred_element_type=F32)
        part = c // (DIFF_HEADS // 2)
        if part < 2:
            for half in range(2):
                blk = y[:, half * LANES:(half + 1) * LANES]
                ms = jnp.dot((blk * blk).astype(BF16), seg, preferred_element_type=F32) * (1.0 / HEAD_DIM)
                yn = blk * lax.rsqrt(ms + EPS) * gains[part]
                out = yn * cos + _swap_halves(yn, first_half) * sin
                o_ref[0, :, (2 * c + half) * LANES:(2 * c + half + 1) * LANES] = out.astype(BF16)
        else:
            o_ref[0, :, c * 256:(c + 1) * 256] = y.astype(BF16)


def _inproj(x, mod, w, row_tabs, consts, n_lat, n_out, even):
    b, t, d = x.shape
    tm = _pick_tile(t, (768, 256))
    kern = _inproj_even_kernel if even else _inproj_odd_kernel
    in_specs = [
        pl.BlockSpec((1, tm, d), lambda bi, i: (bi, i, 0)),
        pl.BlockSpec((1, N_MOD, d), lambda bi, i: (bi, 0, 0)),
        pl.BlockSpec((1, N_MOD, d), lambda bi, i: (b, 0, 0)),
        _resident_spec(w.shape),
    ]
    in_specs += [pl.BlockSpec((tm, LANES), lambda bi, i: (i, 0)) for _ in row_tabs]
    in_specs += [_const_spec(c.shape) for c in consts]
    return pl.pallas_call(
        functools.partial(kern, n_lat=n_lat, tm=tm),
        grid=(b, t // tm),
        in_specs=in_specs,
        out_specs=pl.BlockSpec((1, tm, n_out), lambda bi, i: (bi, i, 0)),
        out_shape=jax.ShapeDtypeStruct((b, t, n_out), BF16),
        compiler_params=_cparams("parallel", "parallel"),
        name="inproj_even" if even else "inproj_odd",
    )(x, mod, mod, w, *row_tabs, *consts)


def _fold_kernel(ud_ref, ua_ref, ub_ref, p_ref, m_ref, *, nb, tf):
    j = pl.program_id(0)
    row = lax.broadcasted_iota(jnp.int32, (tf, tf), 0)
    col = lax.broadcasted_iota(jnp.int32, (tf, tf), 1)
    flip = ((row + col == tf) & (row >= 1)).astype(BF16)
    corner = ((row == 0) & (col == 0) & (j > 0)).astype(BF16)
    for bi in range(nb):
        mirrored = (jnp.dot(flip, ua_ref[bi], preferred_element_type=F32)
                    + jnp.dot(corner, ub_ref[bi], preferred_element_type=F32))
        ud = ud_ref[bi].astype(F32)
        p_ref[bi] = (ud + mirrored).astype(BF16)
        m_ref[bi] = (ud - mirrored).astype(BF16)


def _dft_kernel(c_ref, s_ref, p_ref, m_ref, mid_ref, cc_ref, sc_ref, o_ref, acc_c, acc_s, *, nb, tk, scale):
    i = pl.program_id(0)
    j = pl.program_id(1)

    @pl.when(j == 0)
    def _():
        acc_c[...] = jnp.zeros_like(acc_c)
        acc_s[...] = jnp.zeros_like(acc_s)

    cm = c_ref[...]
    sm = s_ref[...]
    for bi in range(nb):
        acc_c[bi] += jnp.dot(cm, p_ref[bi], preferred_element_type=F32)
        acc_s[bi] += jnp.dot(sm, m_ref[bi], preferred_element_type=F32)

    @pl.when(j == pl.num_programs(1) - 1)
    def _():
        cc = cc_ref[...]
        sc = sc_ref[...]
        k = i * tk + lax.broadcasted_iota(jnp.int32, (tk, 1), 0)
        sign = (1 - 2 * (k & 1)).astype(F32)
        for bi in range(nb):
            mid = mid_ref[bi, 0:1, :].astype(F32)
            for g in range(FNET_GROUPS):
                sl = slice(g * FNET_GROUP_DIM, (g + 1) * FNET_GROUP_DIM)
                a = (acc_c[bi, :, sl] + sign * mid[:, sl]).astype(BF16)
                bm = acc_s[bi, :, sl].astype(BF16)
                y = jnp.dot(a, cc, preferred_element_type=F32) - jnp.dot(bm, sc, preferred_element_type=F32)
                o_ref[bi, :, sl] = (y * scale).astype(BF16)


def _angle_tables(n_rows, n_cols, stride, period):
    k = jnp.arange(n_rows, dtype=jnp.int32)[:, None]
    n = jnp.arange(n_cols, dtype=jnp.int32)[None, :]
    ang = ((k * n * stride) % period).astype(F32) * (2.0 * math.pi / period)
    return jnp.cos(ang), jnp.sin(ang)


def _dft_tables(n):
    ca, sa = _angle_tables(n // LANES, n // 2, LANES, n)
    cb, sb = _angle_tables(LANES, n // 2, 1, n)
    ca, sa, cb, sb = ca[:, None, :], sa[:, None, :], cb[None, :, :], sb[None, :, :]
    cm = (ca * cb - sa * sb).astype(BF16).reshape(n, n // 2)
    sm = (sa * cb + ca * sb).astype(BF16).reshape(n, n // 2)
    return cm, sm


def _fourier_seq(hin, row0, n, chan_tabs):
    b = hin.shape[0]
    half = n // 2
    tf = _pick_tile(half, (256, 128))
    off_f, last = row0 // tf, n // tf - 1
    seq = lambda index: pl.BlockSpec((b, tf, FNET_WIDTH), lambda j: (0, off_f + index(j), 0))
    folded = jax.ShapeDtypeStruct((b, half, FNET_WIDTH), BF16)
    u_plus, u_minus = pl.pallas_call(
        functools.partial(_fold_kernel, nb=b, tf=tf),
        grid=(half // tf,),
        in_specs=[seq(lambda j: j), seq(lambda j: last - j), seq(lambda j: jnp.minimum(last + 1 - j, last))],
        out_specs=[pl.BlockSpec((b, tf, FNET_WIDTH), lambda j: (0, j, 0))] * 2,
        out_shape=[folded, folded],
        compiler_params=_cparams("parallel"),
        name=f"fourier_fold_{n}",
    )(hin, hin, hin)

    cm, sm = _dft_tables(n)
    cc, sc = chan_tabs
    tk = _pick_tile(n, (1024, 512, 256))
    tn = _pick_tile(half, (1024, 512, 256, 128))
    mid_rows = 16
    folded_spec = pl.BlockSpec((b, tn, FNET_WIDTH), lambda i, j: (0, j, 0))
    return pl.pallas_call(
        functools.partial(_dft_kernel, nb=b, tk=tk, scale=1.0 / math.sqrt(n * FNET_GROUP_DIM)),
        grid=(n // tk, half // tn),
        in_specs=[
            pl.BlockSpec((tk, tn), lambda i, j: (i, j)),
            pl.BlockSpec((tk, tn), lambda i, j: (i, j)),
            folded_spec,
            folded_spec,
            pl.BlockSpec((b, mid_rows, FNET_WIDTH), lambda i, j: (0, (row0 + half) // mid_rows, 0)),
            _const_spec((FNET_GROUP_DIM, FNET_GROUP_DIM)),
            _const_spec((FNET_GROUP_DIM, FNET_GROUP_DIM)),
        ],
        out_specs=pl.BlockSpec((b, tk, FNET_WIDTH), lambda i, j: (0, i, 0)),
        out_shape=jax.ShapeDtypeStruct((b, n, FNET_WIDTH), BF16),
        scratch_shapes=[pltpu.VMEM((b, tk, FNET_WIDTH), F32), pltpu.VMEM((b, tk, FNET_WIDTH), F32)],
        compiler_params=_cparams("parallel", "arbitrary"),
        name=f"fourier_seq_{n}",
    )(cm, sm, u_plus, u_minus, hin, cc, sc)


def _log_sigmoid(x):
    return jnp.minimum(x, 0.0) - jnp.log(1.0 + jnp.exp(-jnp.abs(x)))


def _retention_kernel(q_ref, k_ref, v_ref, g_ref, decf_ref, decb_ref, o_ref,
                      of_scr, ob_scr, dec_scr, *, n_lat, n_ctx):
    c = RET_CHUNK
    pos_i = lax.broadcasted_iota(jnp.int32, (c, c), 0).astype(F32)
    pos_j = lax.broadcasted_iota(jnp.int32, (c, c), 1).astype(F32)
    lg_f = _log_sigmoid(decf_ref[0])
    lg_b = _log_sigmoid(decb_ref[0])
    rel = pos_i - pos_j
    mask_f = rel >= 0.0
    mask_b = rel < 0.0
    dec_scr[0] = jnp.where(mask_f, jnp.exp(lg_f * jnp.where(mask_f, rel, 0.0)), 0.0)
    dec_scr[1] = jnp.where(mask_b, jnp.exp(lg_b * jnp.where(mask_b, -rel, 0.0)), 0.0)
    dec_scr[2] = jnp.exp(lg_f * (pos_i + 1.0))
    dec_scr[3] = jnp.exp(lg_b * (c - pos_i))
    dec_scr[4] = jnp.exp(lg_f * (c - 1.0 - pos_i))
    dec_scr[5] = jnp.exp(lg_b * pos_i)
    dec_scr[6] = jnp.exp(lg_f * (c + 0.0 * pos_i))
    dec_scr[7] = jnp.exp(lg_b * (c + 0.0 * pos_i))
    def one_dir(d, start, out_scr, state):
        qc = q_ref[0, pl.ds(start, c), :]
        kc = k_ref[0, pl.ds(start, c), :]
        vc = v_ref[0, pl.ds(start, c), :]
        scores = lax.dot_general(qc, kc, (((1,), (1,)), ((), ())), preferred_element_type=F32)
        intra = jnp.dot((scores * dec_scr[d]).astype(BF16), vc, preferred_element_type=F32)
        inter = jnp.dot(qc, state.astype(BF16), preferred_element_type=F32)
        out_scr[pl.ds(start, c), :] = intra + dec_scr[2 + d] * inter
        vd = (vc.astype(F32) * dec_scr[4 + d]).astype(BF16)
        kv = lax.dot_general(kc, vd, (((0,), (0,)), ((), ())), preferred_element_type=F32)
        return dec_scr[6 + d] * state + kv

    s_f = jnp.zeros((LANES, LANES), F32)
    s_b = jnp.zeros((LANES, LANES), F32)
    n_ctx_chunks = n_ctx // c
    for ci in range(n_ctx_chunks):
        s_f = one_dir(0, n_lat + ci * c, of_scr, s_f)
        s_b = one_dir(1, n_lat + (n_ctx_chunks - 1 - ci) * c, ob_scr, s_b)

    n_lat_chunks = n_lat // c

    def body(i, states):
        s_f = one_dir(0, pl.multiple_of(i * c, c), of_scr, states[0])
        s_b = one_dir(1, pl.multiple_of((n_lat_chunks - 1 - i) * c, c), ob_scr, states[1])
        return s_f, s_b

    lax.fori_loop(0, n_lat_chunks, body, (s_f, s_b), unroll=_pick_tile(n_lat_chunks, (8, 4, 2, 1)))

    def finish(i, carry):
        r0 = pl.multiple_of(i * c, c)
        o = of_scr[pl.ds(r0, c), :] + ob_scr[pl.ds(r0, c), :]
        o = o * lax.rsqrt(jnp.mean(o * o, axis=-1, keepdims=True) + EPS)
        g = g_ref[0, pl.ds(r0, c), :].astype(F32)
        o_ref[0, pl.ds(r0, c), :] = (o * (g * _sigmoid(g))).astype(BF16)
        return carry

    n_chunks = (n_lat + n_ctx) // c
    lax.fori_loop(0, n_chunks, finish, 0, unroll=_pick_tile(n_chunks, (6, 3, 2, 1)))


def _retention(hin, dec_f, dec_b, n_lat):
    b, t, _ = hin.shape
    h = RET_HEADS
    seq = lambda col0: pl.BlockSpec((1, t, LANES), lambda bi, hi: (bi, 0, col0 + hi))
    dec = pl.BlockSpec((1, 1, LANES), lambda bi, hi: (hi, 0, 0))
    return pl.pallas_call(
        functools.partial(_retention_kernel, n_lat=n_lat, n_ctx=t - n_lat),
        grid=(b, h),
        in_specs=[seq(4), seq(8), seq(12), seq(16), dec, dec],
        out_specs=pl.BlockSpec((1, t, LANES), lambda bi, hi: (bi, 0, hi)),
        out_shape=jax.ShapeDtypeStruct((b, t, h * LANES), BF16),
        scratch_shapes=[
            pltpu.VMEM((t, LANES), F32),
            pltpu.VMEM((t, LANES), F32),
            pltpu.VMEM((8, RET_CHUNK, LANES), F32),
        ],
        compiler_params=_cparams("parallel", "parallel"),
        name="retention",
    )(hin, hin, hin, hin, dec_f, dec_b)


def _mixer_residual(x, a_refs, w_refs, modl_ref, modc_ref, row0, n_lat):
    y = jnp.dot(a_refs[0][0], w_refs[0][...], preferred_element_type=F32)
    for a_ref, w_ref in zip(a_refs[1:], w_refs[1:]):
        y = y + jnp.dot(a_ref[0], w_ref[...], preferred_element_type=F32)
    return x + _gate_rows(modl_ref, modc_ref, 2, row0, x.shape[0], n_lat) * y


def _mixer_specs(parts, weights, tm):
    specs = [pl.BlockSpec((1, tm, p.shape[2]), lambda bi, i: (bi, i, 0)) for p in parts]
    return specs + [_resident_spec(w.shape) for w in weights]


def _attn_kernel(prev_ref, q_ref, k_ref, v_ref, lam_ref, sg_ref, o_ref, *, tk, lam_init):
    del prev_ref
    tq = q_ref.shape[1]
    n_keys = k_ref.shape[1]
    q = q_ref[0]
    lane = lax.broadcasted_iota(jnp.int32, (tq, LANES), 1)
    zero = jnp.zeros_like(q)
    q_sub = (jnp.where(lane < HEAD_DIM, q, zero), jnp.where(lane >= HEAD_DIM, q, zero))
    ones = jnp.ones((tk, LANES), BF16)
    m = [None, None]
    acc = [None, None]
    for j in range(n_keys // tk):
        kc = k_ref[0, j * tk:(j + 1) * tk, :]
        v1 = jnp.concatenate([v_ref[0, j * tk:(j + 1) * tk, :], ones], axis=1)
        for sub in range(2):
            s = lax.dot_general(q_sub[sub], kc, (((1,), (1,)), ((), ())), preferred_element_type=F32)
            mx = jnp.max(s, axis=-1, keepdims=True)
            if j == 0:
                m[sub] = mx
                p = jnp.exp2(s - mx).astype(BF16)
                acc[sub] = jnp.dot(p, v1, preferred_element_type=F32)
            else:
                m_new = jnp.maximum(m[sub], mx)
                alpha = jnp.exp2(m[sub] - m_new)
                p = jnp.exp2(s - m_new).astype(BF16)
                acc[sub] = alpha * acc[sub] + jnp.dot(p, v1, preferred_element_type=F32)
                m[sub] = m_new

    lv = lam_ref[...]
    s1 = jnp.sum(lv[0:1] * lv[1:2], axis=-1, keepdims=True)
    s2 = jnp.sum(lv[2:3] * lv[3:4], axis=-1, keepdims=True)
    lam = jnp.exp(s1) - jnp.exp(s2) + lam_init
    o = acc[0][:, :LANES] / acc[0][:, LANES:] - lam * (acc[1][:, :LANES] / acc[1][:, LANES:])
    o = o * lax.rsqrt(jnp.mean(o * o, axis=-1, keepdims=True) + EPS)
    o_ref[0] = (o * sg_ref[...] * (1.0 - lam_init)).astype(BF16)


def _attention(out, qkv, lam_vecs, sub_gain, lam_init, n_lat, ctx_queries):
    b, t, _ = qkv.shape
    h = DIFF_HEADS
    n_ctx = t - n_lat
    if ctx_queries:
        tq, n_q_tiles, q_off = n_ctx, 1, n_lat // n_ctx
        keys, k_off = n_ctx, n_lat // n_ctx
        tk = n_ctx
    else:
        tq = _pick_tile(n_lat, (1024, 512, 256))
        n_q_tiles, q_off = n_lat // tq, 0
        keys, k_off = t, 0
        tk = _pick_tile(t, (768, 256))
    return pl.pallas_call(
        functools.partial(_attn_kernel, tk=tk, lam_init=lam_init),
        grid=(b, h, n_q_tiles),
        in_specs=[
            pl.BlockSpec(memory_space=pl.ANY),
            pl.BlockSpec((1, tq, LANES), lambda bi, hi, i: (bi, q_off + i, hi)),
            pl.BlockSpec((1, keys, LANES), lambda bi, hi, i: (bi, k_off, h + hi)),
            pl.BlockSpec((1, keys, LANES), lambda bi, hi, i: (bi, k_off, 2 * h + hi)),
            _const_spec((4, LANES)),
            _const_spec((1, LANES)),
        ],
        out_specs=pl.BlockSpec((1, tq, LANES), lambda bi, hi, i: (bi, q_off + i, hi)),
        out_shape=jax.ShapeDtypeStruct((b, t, h * LANES), BF16),
        input_output_aliases={0: 0},
        compiler_params=_cparams("parallel", "parallel", "arbitrary"),
        name="diff_attn_ctx" if ctx_queries else "diff_attn_lat",
    )(out, qkv, qkv, qkv, lam_vecs, sub_gain)


def _ffn_kernel(*refs, n_parts, n_lat, tm, tf):
    x_ref, modl_ref, modc_ref = refs[:3]
    a_refs, w_refs = refs[3:3 + n_parts], refs[3 + n_parts:3 + 2 * n_parts]
    w1_ref, w3_ref, w2_ref, o_ref = refs[3 + 2 * n_parts:]
    row0 = pl.program_id(1) * tm
    x = _mixer_residual(x_ref[0], a_refs, w_refs, modl_ref, modc_ref, row0, n_lat)
    h = _modulated(x, modl_ref, modc_ref, 3, row0, n_lat).astype(BF16)
    f_dim = w1_ref.shape[1]
    y = jnp.zeros((tm, x.shape[1]), F32)
    for c in range(f_dim // tf):
        sl = slice(c * tf, (c + 1) * tf)
        a = jnp.dot(h, w1_ref[:, sl], preferred_element_type=F32)
        g = jnp.dot(h, w3_ref[:, sl], preferred_element_type=F32)
        u = (a * _sigmoid(a) * g).astype(BF16)
        y = y + jnp.dot(u, w2_ref[sl, :], preferred_element_type=F32)
    gate = _gate_rows(modl_ref, modc_ref, 5, row0, tm, n_lat)
    o_ref[0] = x + gate * y


def _ffn(x, mod, parts, out_weights, w1, w3, w2, n_lat):
    b, t, d = x.shape
    tm = _pick_tile(t, (768, 256))
    tf = _pick_tile(w1.shape[1], (256, 128))
    return pl.pallas_call(
        functools.partial(_ffn_kernel, n_parts=len(parts), n_lat=n_lat, tm=tm, tf=tf),
        grid=(b, t // tm),
        in_specs=[
            pl.BlockSpec((1, tm, d), lambda bi, i: (bi, i, 0)),
            pl.BlockSpec((1, N_MOD, d), lambda bi, i: (bi, 0, 0)),
            pl.BlockSpec((1, N_MOD, d), lambda bi, i: (b, 0, 0)),
            *_mixer_specs(parts, out_weights, tm),
            _resident_spec(w1.shape),
            _resident_spec(w3.shape),
            _resident_spec(w2.shape),
        ],
        out_specs=pl.BlockSpec((1, tm, d), lambda bi, i: (bi, i, 0)),
        out_shape=jax.ShapeDtypeStruct((b, t, d), F32),
        input_output_aliases={0: 0},
        compiler_params=_cparams("parallel", "parallel"),
        name="mixer_out_ffn",
    )(x, mod, mod, *parts, *out_weights, w1, w3, w2)


ROUTE_E1, ROUTE_E2, ROUTE_G1, ROUTE_G2 = 0, 1, 2, 3
SLABS = D_MODEL // LANES
DMA_LOOP_UNROLL = 8


def _top2_route(logits):
    tm = logits.shape[0]
    lane = lax.broadcasted_iota(jnp.int32, (tm, LANES), 1)
    neg = jnp.float32(-jnp.inf)
    lg = jnp.where(lane < N_EXPERTS, logits, neg)
    m1 = jnp.max(lg, axis=-1, keepdims=True)
    i1 = jnp.min(jnp.where(lg == m1, lane, LANES), axis=-1, keepdims=True)
    lg2 = jnp.where(lane == i1, neg, lg)
    m2 = jnp.max(lg2, axis=-1, keepdims=True)
    i2 = jnp.min(jnp.where(lg2 == m2, lane, LANES), axis=-1, keepdims=True)
    e = jnp.exp(m2 - m1)
    g1 = 1.0 / (1.0 + e)
    g2 = e / (1.0 + e)
    rec = jnp.where(lane == ROUTE_E1, i1.astype(F32), 0.0) + jnp.where(lane == ROUTE_E2, i2.astype(F32), 0.0)
    return rec + jnp.where(lane == ROUTE_G1, g1, 0.0) + jnp.where(lane == ROUTE_G2, g2, 0.0)


def _slab_store(ref, val, n_rows):
    for s in range(SLABS):
        ref[pl.ds(s, n_rows, stride=SLABS), :] = val[:, s * LANES:(s + 1) * LANES]


def _slab_load(ref, n_rows):
    return jnp.concatenate([ref[pl.ds(s, n_rows, stride=SLABS), :] for s in range(SLABS)], axis=1)


def _mixer_out_kernel(*refs, n_parts, n_lat, tm):
    x_ref, modl_ref, modc_ref = refs[:3]
    a_refs, w_refs = refs[3:3 + n_parts], refs[3 + n_parts:3 + 2 * n_parts]
    o_ref = refs[3 + 2 * n_parts]
    row0 = pl.program_id(1) * tm
    o_ref[0] = _mixer_residual(x_ref[0], a_refs, w_refs, modl_ref, modc_ref, row0, n_lat)


def _mixer_out(x, mod, parts, out_weights, n_lat, n_rows):
    b, t, d = x.shape
    tm = _pick_tile(math.gcd(n_rows, x.shape[1]), (768, 512, 256))
    return pl.pallas_call(
        functools.partial(_mixer_out_kernel, n_parts=len(parts), n_lat=n_lat, tm=tm),
        grid=(b, n_rows // tm),
        in_specs=[
            pl.BlockSpec((1, tm, d), lambda bi, i: (bi, i, 0)),
            pl.BlockSpec((1, N_MOD, d), lambda bi, i: (bi, 0, 0)),
            pl.BlockSpec((1, N_MOD, d), lambda bi, i: (b, 0, 0)),
            *_mixer_specs(parts, out_weights, tm),
        ],
        out_specs=pl.BlockSpec((1, tm, d), lambda bi, i: (bi, i, 0)),
        out_shape=jax.ShapeDtypeStruct((b, t, d), F32),
        input_output_aliases={0: 0},
        compiler_params=_cparams("parallel", "parallel"),
        name="mixer_out",
    )(x, mod, mod, *parts, *out_weights)


def _route_kernel(x_ref, modl_ref, modc_ref, wr_ref, h_ref, rec_ref, *, n_lat, tm):
    row0 = pl.program_id(1) * tm
    hf = _modulated(x_ref[0], modl_ref, modc_ref, 3, row0, n_lat)
    logits = jnp.dot(hf, wr_ref[...], preferred_element_type=F32, precision=lax.Precision.HIGHEST)
    rec_ref[0] = _top2_route(logits)
    _slab_store(h_ref, hf, tm)


def _moe_route(x, mod, w_router, n_lat, n_rows):
    b, _, d = x.shape
    tm = _pick_tile(math.gcd(n_rows, x.shape[1]), (768, 512, 256))
    tiles = n_rows // tm
    return pl.pallas_call(
        functools.partial(_route_kernel, n_lat=n_lat, tm=tm),
        grid=(b, tiles),
        in_specs=[
            pl.BlockSpec((1, tm, d), lambda bi, i: (bi, i, 0)),
            pl.BlockSpec((1, N_MOD, d), lambda bi, i: (bi, 0, 0)),
            pl.BlockSpec((1, N_MOD, d), lambda bi, i: (b, 0, 0)),
            _const_spec((d, LANES)),
        ],
        out_specs=[
            pl.BlockSpec((tm * SLABS, LANES), lambda bi, i: (bi * tiles + i, 0)),
            pl.BlockSpec((1, tm, LANES), lambda bi, i: (bi, i, 0)),
        ],
        out_shape=[
            jax.ShapeDtypeStruct((b * n_rows * SLABS, LANES), F32),
            jax.ShapeDtypeStruct((b, n_rows, LANES), F32),
        ],
        compiler_params=_cparams("parallel", "parallel"),
        name="moe_route",
    )(x, mod, mod, w_router)


def _row_copy(src_ref, src_row, dst_ref, dst_row, sem):
    return pltpu.make_async_copy(
        src_ref.at[pl.ds(pl.multiple_of(src_row * SLABS, SLABS), SLABS), :],
        dst_ref.at[pl.ds(pl.multiple_of(dst_row * SLABS, SLABS), SLABS), :],
        sem)


def _dispatch_kernel(dest_ref, h_ref, init_ref, xs_ref, sem, *, tm):
    del init_ref

    def start(r, carry):
        for slot in range(2):
            _row_copy(h_ref, r, xs_ref, dest_ref[0, 0, 2 * r + slot], sem).start()
        return carry

    lax.fori_loop(0, tm, start, 0, unroll=DMA_LOOP_UNROLL)
    for _ in range(2):
        pltpu.make_async_copy(h_ref, xs_ref.at[pl.ds(0, tm * SLABS), :], sem).wait()


def _moe_dispatch(h_slabs, dest, n_sorted_rows):
    n_tok = h_slabs.shape[0] // SLABS
    tm = _pick_tile(n_tok, (512, 256))
    steps = n_tok // tm
    return pl.pallas_call(
        functools.partial(_dispatch_kernel, tm=tm),
        grid=(steps,),
        in_specs=[
            pl.BlockSpec((1, 1, 2 * tm), lambda i: (i, 0, 0), memory_space=pltpu.SMEM),
            pl.BlockSpec((tm * SLABS, LANES), lambda i: (i, 0)),
            pl.BlockSpec(memory_space=pl.ANY),
        ],
        out_specs=pl.BlockSpec(memory_space=pl.ANY),
        out_shape=jax.ShapeDtypeStruct((n_sorted_rows * SLABS, LANES), F32),
        scratch_shapes=[pltpu.SemaphoreType.DMA(())],
        input_output_aliases={2: 0},
        compiler_params=_cparams("arbitrary"),
        name="moe_dispatch",
    )(dest.reshape(steps, 1, 2 * tm), h_slabs, jnp.zeros((n_sorted_rows * SLABS, LANES), F32))


def _experts_kernel(te_ref, used_ref, x_ref, w1_ref, w3_ref, w2_ref, o_ref, acc_scr, *, tm, tf, n_steps_f):
    del te_ref
    i = pl.program_id(0)
    f = pl.program_id(1)
    valid = i < used_ref[0]
    n_f = pl.num_programs(1)

    def step(first, last):
        h = _slab_load(x_ref, tm).astype(BF16)
        f_blk = w1_ref.shape[2]
        y = None if first else acc_scr[...]
        for c in range(f_blk // tf):
            sl = slice(c * tf, (c + 1) * tf)
            a = jnp.dot(h, w1_ref[0, :, sl], preferred_element_type=F32)
            g = jnp.dot(h, w3_ref[0, :, sl], preferred_element_type=F32)
            u = (a * _sigmoid(a) * g).astype(BF16)
            part = jnp.dot(u, w2_ref[0, sl, :], preferred_element_type=F32)
            y = part if y is None else y + part
        if last:
            _slab_store(o_ref, y, tm)
        else:
            acc_scr[...] = y

    if n_steps_f == 1:
        pl.when(valid)(lambda: step(True, True))
    else:
        pl.when(valid & (f == 0))(lambda: step(True, False))
        pl.when(valid & (f == n_f - 1))(lambda: step(False, True))
        if n_steps_f > 2:
            pl.when(valid & (f > 0) & (f < n_f - 1))(lambda: step(False, False))

    @pl.when(jnp.logical_not(valid) & (f == n_f - 1))
    def _():
        _slab_store(o_ref, jnp.zeros(acc_scr.shape, F32), tm)


def _moe_experts(xs, tile_expert, n_used, w1, w3, w2, tm):
    n_exp, d, f_dim = w1.shape
    n_tiles = xs.shape[0] // (tm * SLABS)
    f_blk = _pick_tile(f_dim, (1792, 512, 256))
    tf = _pick_tile(f_blk, (256, 128))
    n_f = f_dim // f_blk

    def f_eff(i, f, used):
        return jnp.where(i < used[0], f, n_f - 1)

    grid_spec = pltpu.PrefetchScalarGridSpec(
        num_scalar_prefetch=2,
        grid=(n_tiles, n_f),
        in_specs=[
            pl.BlockSpec((tm * SLABS, LANES), lambda i, f, te, used: (jnp.minimum(i, used[0] - 1), 0)),
            pl.BlockSpec((1, d, f_blk), lambda i, f, te, used: (te[i], 0, f_eff(i, f, used))),
            pl.BlockSpec((1, d, f_blk), lambda i, f, te, used: (te[i], 0, f_eff(i, f, used))),
            pl.BlockSpec((1, f_blk, d), lambda i, f, te, used: (te[i], f_eff(i, f, used), 0)),
        ],
        out_specs=pl.BlockSpec((tm * SLABS, LANES), lambda i, f, te, used: (i, 0)),
        scratch_shapes=[pltpu.VMEM((tm, d), F32)],
    )
    return pl.pallas_call(
        functools.partial(_experts_kernel, tm=tm, tf=tf, n_steps_f=n_f),
        grid_spec=grid_spec,
        out_shape=jax.ShapeDtypeStruct(xs.shape, F32),
        compiler_params=_cparams("arbitrary", "arbitrary"),
        name="moe_experts",
    )(tile_expert, n_used, xs, w1, w3, w2)


def _combine_kernel(dcur_ref, dnext_ref, x_ref, modl_ref, modc_ref, rec_ref, ys_ref, o_ref, buf, sems,
                    *, n_lat, tm, tiles_per_batch):
    i = pl.program_id(0)
    n = pl.num_programs(0)

    def start_gather(dest_ref, slot_buf):
        def body(r, carry):
            for slot in range(2):
                _row_copy(ys_ref, dest_ref[0, 0, 2 * r + slot], buf.at[slot_buf, slot], r, sems.at[slot_buf]).start()
            return carry
        lax.fori_loop(0, tm, body, 0, unroll=DMA_LOOP_UNROLL)

    def wait_gather(slot_buf):
        for slot in range(2):
            pltpu.make_async_copy(ys_ref.at[pl.ds(0, tm * SLABS), :], buf.at[slot_buf, slot], sems.at[slot_buf]).wait()

    @pl.when(i == 0)
    def _():
        start_gather(dcur_ref, 0)

    for parity in range(2):
        @pl.when((i + 1 < n) & ((i + 1) % 2 == parity))
        def _():
            start_gather(dnext_ref, parity)

    for parity in range(2):
        @pl.when(i % 2 == parity)
        def _():
            wait_gather(parity)
            rec = rec_ref[0]
            y = (rec[:, ROUTE_G1:ROUTE_G1 + 1] * _slab_load(buf.at[parity, 0], tm)
                 + rec[:, ROUTE_G2:ROUTE_G2 + 1] * _slab_load(buf.at[parity, 1], tm))
            row0 = (i % tiles_per_batch) * tm
            gate = _gate_rows(modl_ref, modc_ref, 5, row0, tm, n_lat)
            o_ref[0] = x_ref[0] + gate * y


def _moe_combine(x, mod, rec, ys, dest, n_lat, n_rows):
    b, t, d = x.shape
    tm = _pick_tile(n_rows, (256,))
    tpb = n_rows // tm
    n_tiles = b * tpb
    dest3 = dest.reshape(n_tiles, 1, 2 * tm)
    tok = lambda i: (i // tpb, i % tpb, 0)
    return pl.pallas_call(
        functools.partial(_combine_kernel, n_lat=n_lat, tm=tm, tiles_per_batch=tpb),
        grid=(n_tiles,),
        in_specs=[
            pl.BlockSpec((1, 1, 2 * tm), lambda i: (i, 0, 0), memory_space=pltpu.SMEM),
            pl.BlockSpec((1, 1, 2 * tm), lambda i: (jnp.minimum(i + 1, n_tiles - 1), 0, 0),
                         memory_space=pltpu.SMEM),
            pl.BlockSpec((1, tm, d), tok),
            pl.BlockSpec((1, N_MOD, d), lambda i: (i // tpb, 0, 0)),
            pl.BlockSpec((1, N_MOD, d), lambda i: (b, 0, 0)),
            pl.BlockSpec((1, tm, LANES), tok),
            pl.BlockSpec(memory_space=pl.ANY),
        ],
        out_specs=pl.BlockSpec((1, tm, d), tok),
        out_shape=jax.ShapeDtypeStruct((b, n_rows, d), F32),
        scratch_shapes=[pltpu.VMEM((2, 2, tm * SLABS, LANES), F32), pltpu.SemaphoreType.DMA((2,))],
        input_output_aliases={2: 0} if n_rows == t else {},
        compiler_params=_cparams("arbitrary"),
        name="moe_combine",
    )(dest3, dest3, x, mod, mod, rec, ys)


def _dispatch_plan(rec, tm, n_tiles):
    experts = rec[:, ROUTE_E1:ROUTE_E2 + 1].astype(jnp.int32).reshape(-1)
    onehot = (experts[:, None] == jnp.arange(N_EXPERTS, dtype=jnp.int32)[None, :]).astype(jnp.int32)
    csum = jnp.cumsum(onehot, axis=0)
    counts = csum[-1]
    padded = (counts + tm - 1) // tm * tm
    ends = jnp.cumsum(padded)
    dest = jnp.sum(onehot * (csum - 1 + (ends - padded)[None, :]), axis=1)
    n_used = (ends[-1] // tm).astype(jnp.int32)
    tile_start = jnp.arange(n_tiles, dtype=jnp.int32) * tm
    tile_expert = jnp.sum((tile_start[:, None] >= ends[None, :]).astype(jnp.int32), axis=1)
    tile_expert = jnp.minimum(tile_expert, tile_expert[n_used - 1])
    return dest.astype(jnp.int32), tile_expert.astype(jnp.int32), n_used.reshape(1)


def _moe(x, mod, w_router, w1, w3, w2, first_expert, n_lat, n_rows):
    b, _, d = x.shape
    n_tok = b * n_rows
    tm = 512
    n_tiles = (2 * n_tok + N_EXPERTS * (tm - 1)) // tm
    h_slabs, rec = _moe_route(x, mod, w_router, n_lat, n_rows)
    dest, tile_expert, n_used = _dispatch_plan(rec.reshape(n_tok, LANES), tm, n_tiles)
    xs = _moe_dispatch(h_slabs, dest, n_tiles * tm)
    ys = _moe_experts(xs, tile_expert + first_expert, n_used, w1, w3, w2, tm)
    return _moe_combine(x, mod, rec, ys, dest, n_lat, n_rows)


def _rot_cols(w):
    half = HEAD_DIM // 2
    return jnp.concatenate([-w[..., half:], w[..., :half]], axis=-1)


def _with_rot(w_heads):
    d, n, _ = w_heads.shape
    return jnp.concatenate([w_heads, _rot_cols(w_heads)], axis=-1).reshape(d, n * LANES)


def _gain_lanes(gain):
    return jnp.tile(gain.astype(F32), LANES // HEAD_DIM).reshape(1, LANES)


def _head_sum_matrix():
    idx = jnp.arange(LANES, dtype=jnp.int32) // HEAD_DIM
    return (idx[:, None] == idx[None, :]).astype(BF16)


def _rope_table(n_lat, n_ctx):
    rows = n_lat // GRID_W
    row = jnp.repeat(jnp.arange(rows, dtype=F32), GRID_W)
    col = jnp.tile(jnp.arange(GRID_W, dtype=F32), rows)
    quarter = HEAD_DIM // 4
    inv_freq = ROPE_BASE ** (-jnp.arange(quarter, dtype=F32) / quarter)
    ang = jnp.concatenate([row[:, None] * inv_freq, col[:, None] * inv_freq], axis=-1)
    cos, sin = jnp.cos(ang), jnp.sin(ang)
    ones = jnp.ones((n_ctx, HEAD_DIM), F32)
    zeros = jnp.zeros((n_ctx, HEAD_DIM), F32)
    rows_of = lambda lat, ctx: jnp.concatenate([lat, ctx], axis=0)
    dup = rows_of(jnp.concatenate([cos, cos, sin, sin], axis=-1), jnp.concatenate([ones, zeros], axis=-1))
    cos_packed = rows_of(jnp.concatenate([cos] * 4, axis=-1), jnp.concatenate([ones, ones], axis=-1))
    sin_packed = rows_of(jnp.concatenate([-sin, sin, -sin, sin], axis=-1), jnp.concatenate([zeros, zeros], axis=-1))
    return dup, cos_packed, sin_packed


def _chan_tables():
    idx = jnp.arange(FNET_GROUP_DIM, dtype=jnp.int32)
    ang = ((idx[:, None] * idx[None, :]) % FNET_GROUP_DIM).astype(F32) * (2.0 * math.pi / FNET_GROUP_DIM)
    return jnp.cos(ang).astype(BF16), jnp.sin(ang).astype(BF16)


def kernel(x, c, ctx, c_ctx, w_mod, b_mod, w_in_even, w_out_even, ret_decay_fwd, ret_decay_bwd,
           ffn_w1, ffn_w3, ffn_w2, w_in_odd, w_out_odd, q_norm_gain, k_norm_gain,
           lambda_q1, lambda_k1, lambda_q2, lambda_k2, subln_gain, w_router, moe_w1, moe_w3, moe_w2):
    b, n_lat, d = x.shape
    n_ctx = ctx.shape[1]
    depth = w_mod.shape[0]
    assert d == D_MODEL and b < MOD_ROWS and n_lat % n_ctx == 0 and n_ctx % RET_CHUNK == 0

    stream = jnp.concatenate([x, ctx], axis=1)
    cond = jnp.zeros((MOD_ROWS, d), F32).at[:b].set(c).at[b].set(c_ctx)
    mods = _adaln(cond, w_mod, b_mod).reshape(depth, MOD_ROWS, N_MOD, d)
    tab, cos_packed, sin_packed = _rope_table(n_lat, n_ctx)
    chan_tabs = _chan_tables()
    stack_experts = lambda w: w.astype(BF16).reshape((-1,) + w.shape[2:])
    moe_w1_all, moe_w3_all, moe_w2_all = stack_experts(moe_w1), stack_experts(moe_w3), stack_experts(moe_w2)

    for layer in range(depth):
        i = layer // 2
        mod = mods[layer]
        if layer % 2 == 0:
            w = w_in_even[i]
            hq = RET_HEADS * HEAD_DIM
            wq = w[:, FNET_WIDTH:FNET_WIDTH + hq].reshape(d, RET_HEADS, HEAD_DIM)
            wk = w[:, FNET_WIDTH + hq:FNET_WIDTH + 2 * hq].reshape(d, RET_HEADS, HEAD_DIM)
            w_in = jnp.concatenate(
                [w[:, :FNET_WIDTH], _with_rot(wq), _with_rot(wk), w[:, FNET_WIDTH + 2 * hq:]], axis=1).astype(BF16)
            hin = _inproj(stream, mod, w_in, [tab], [], n_lat, w_in.shape[1], True)
            four = jnp.concatenate(
                [_fourier_seq(hin, 0, n_lat, chan_tabs), _fourier_seq(hin, n_lat, n_ctx, chan_tabs)], axis=1)
            dec_f = jnp.broadcast_to(ret_decay_fwd[i].astype(F32)[:, None, None], (RET_HEADS, 1, LANES))
            dec_b = jnp.broadcast_to(ret_decay_bwd[i].astype(F32)[:, None, None], (RET_HEADS, 1, LANES))
            ret = _retention(hin, dec_f, dec_b, n_lat)
            w_out = w_out_even[i].astype(BF16)
            stream = _ffn(stream, mod, [four, ret], [w_out[:FNET_WIDTH], w_out[FNET_WIDTH:]],
                          ffn_w1[i].astype(BF16), ffn_w3[i].astype(BF16), ffn_w2[i].astype(BF16), n_lat)
        else:
            consts = [_gain_lanes(q_norm_gain[i]), _gain_lanes(k_norm_gain[i]), _head_sum_matrix()]
            qkv = _inproj(stream, mod, w_in_odd[i].astype(BF16), [cos_packed, sin_packed], consts, n_lat,
                          3 * DIFF_HEADS * LANES, False)
            lam_init = 0.8 - 0.6 * float(np.exp(-0.3 * layer))
            lam_vecs = jnp.zeros((4, LANES), F32).at[:, :HEAD_DIM].set(
                jnp.stack([lambda_q1[i], lambda_k1[i], lambda_q2[i], lambda_k2[i]]).astype(F32))
            sub_gain = subln_gain[i].astype(F32).reshape(1, LANES)
            att = jnp.zeros((b, n_lat + n_ctx, DIFF_HEADS * LANES), BF16)
            last = layer == depth - 1
            n_rows = n_lat if last else n_lat + n_ctx
            att = _attention(att, qkv, lam_vecs, sub_gain, lam_init, n_lat, False)
            if not last:
                att = _attention(att, qkv, lam_vecs, sub_gain, lam_init, n_lat, True)
            wr = jnp.zeros((d, LANES), F32).at[:, :N_EXPERTS].set(w_router[i].astype(F32))
            stream = _mixer_out(stream, mod, [att], [w_out_odd[i].astype(BF16)], n_lat, n_rows)
            stream = _moe(stream, mod, wr, moe_w1_all, moe_w3_all, moe_w2_all, i * N_EXPERTS, n_lat, n_rows)
    return stream[:, :n_lat]
```

```python
import functools
import math

import jax
import jax.numpy as jnp
import numpy as np
from jax import lax
from jax.experimental import pallas as pl
from jax.experimental.pallas import tpu as pltpu

F32 = jnp.float32
BF16 = jnp.bfloat16

D_MODEL = 1024
GRID_W = 64
HEAD_DIM = 64
LANES = 128
MXU_WIDTH = 256
FNET_GROUPS = 4
FNET_GROUP_DIM = 128
FNET_WIDTH = FNET_GROUPS * FNET_GROUP_DIM
RET_HEADS = 4
RET_CHUNK = 128
DIFF_HEADS = 8
N_EXPERTS = 8
N_MOD = 6
ROPE_BASE = 10000.0
EPS = 1e-6
LOG2_E = math.log2(math.e)
VMEM_LIMIT_BYTES = 56 * 1024 * 1024
MOD_ROWS = 8


def _cparams(*sem):
    return pltpu.CompilerParams(dimension_semantics=sem, vmem_limit_bytes=VMEM_LIMIT_BYTES)


def _pick_tile(n, candidates):
    for c in candidates:
        if n % c == 0:
            return c
    raise ValueError(f"no tile in {candidates} divides {n}")


def _const_spec(shape):
    nd = len(shape)
    return pl.BlockSpec(shape, lambda *_: (0,) * nd)


def _resident_spec(shape):
    nd = len(shape)
    return pl.BlockSpec(shape, lambda *_: (0,) * nd, pipeline_mode=pl.Buffered(1))


def _sigmoid(x):
    return 1.0 / (1.0 + jnp.exp(-x))


def _modulated(x, modl_ref, modc_ref, k_shift, row0, n_lat):
    tm = x.shape[0]
    ms = jnp.mean(x * x, axis=-1, keepdims=True)
    xn = x * lax.rsqrt(ms + EPS)
    rows = row0 + lax.broadcasted_iota(jnp.int32, (tm, 1), 0)
    is_ctx = rows >= n_lat
    shift = jnp.where(is_ctx, modc_ref[0, k_shift:k_shift + 1, :], modl_ref[0, k_shift:k_shift + 1, :])
    scale = jnp.where(is_ctx, modc_ref[0, k_shift + 1:k_shift + 2, :], modl_ref[0, k_shift + 1:k_shift + 2, :])
    return xn * (1.0 + scale) + shift


def _gate_rows(modl_ref, modc_ref, k_gate, row0, tm, n_lat):
    rows = row0 + lax.broadcasted_iota(jnp.int32, (tm, 1), 0)
    return jnp.where(rows >= n_lat, modc_ref[0, k_gate:k_gate + 1, :], modl_ref[0, k_gate:k_gate + 1, :])


def _adaln_kernel(c_ref, w_ref, b_ref, o_ref):
    c = c_ref[...]
    s = c * _sigmoid(c)
    o_ref[0] = jnp.dot(s, w_ref[0], preferred_element_type=F32, precision=lax.Precision.HIGHEST) + b_ref[0]


def _adaln(cond_rows, w_mod, b_mod):
    depth, d, n = w_mod.shape
    tn = _pick_tile(n, (1536, 1024, 512))
    return pl.pallas_call(
        _adaln_kernel,
        grid=(depth, n // tn),
        in_specs=[
            _const_spec((MOD_ROWS, d)),
            pl.BlockSpec((1, d, tn), lambda l, j: (l, 0, j)),
            pl.BlockSpec((1, 1, tn), lambda l, j: (l, 0, j)),
        ],
        out_specs=pl.BlockSpec((1, MOD_ROWS, tn), lambda l, j: (l, 0, j)),
        out_shape=jax.ShapeDtypeStruct((depth, MOD_ROWS, n), F32),
        compiler_params=_cparams("parallel", "parallel"),
        name="adaln",
    )(cond_rows, w_mod, b_mod.reshape(depth, 1, n))


def _rope_dup(blk, tab):
    t = blk * tab
    return t + pltpu.roll(t, HEAD_DIM, axis=1)


def _inproj_even_kernel(x_ref, modl_ref, modc_ref, w_ref, tab_ref, o_ref, *, n_lat, tm):
    row0 = pl.program_id(1) * tm
    h = _modulated(x_ref[0], modl_ref, modc_ref, 0, row0, n_lat).astype(BF16)
    tab = tab_ref[...]
    lane = lax.broadcasted_iota(jnp.int32, (tm, LANES), 1)
    n_out = o_ref.shape[2]
    for c in range(n_out // MXU_WIDTH):
        y = jnp.dot(h, w_ref[:, c * MXU_WIDTH:(c + 1) * MXU_WIDTH], preferred_element_type=F32)
        for half in range(2):
            blk = y[:, half * LANES:(half + 1) * LANES]
            col = c * 2 + half
            if 4 <= col < 8:
                blk = _rope_dup(blk, tab)
            elif 8 <= col < 12:
                blk = jnp.where(lane < HEAD_DIM, _rope_dup(blk, tab) * (HEAD_DIM ** -0.5), 0.0)
            o_ref[0, :, col * LANES:(col + 1) * LANES] = blk.astype(BF16)


def _swap_halves(y, first_half):
    fwd = pltpu.roll(y, HEAD_DIM // 2, axis=1)
    bwd = pltpu.roll(y, LANES - HEAD_DIM // 2, axis=1)
    return jnp.where(first_half, bwd, fwd)


def _inproj_odd_kernel(x_ref, modl_ref, modc_ref, w_ref, cos_ref, sin_ref, gq_ref, gk_ref, seg_ref, o_ref,
                       *, n_lat, tm):
    row0 = pl.program_id(1) * tm
    h = _modulated(x_ref[0], modl_ref, modc_ref, 0, row0, n_lat).astype(BF16)
    cos = cos_ref[...]
    sin = sin_ref[...]
    seg = seg_ref[...]
    lane = lax.broadcasted_iota(jnp.int32, (tm, LANES), 1)
    first_half = lane % HEAD_DIM < HEAD_DIM // 2
    gains = (gq_ref[...] * (HEAD_DIM ** -0.5 * LOG2_E), gk_ref[...])
    for c in range(w_ref.shape[1] // MXU_WIDTH):
        y = jnp.dot(h, w_ref[:, c * MXU_WIDTH:(c + 1) * MXU_WIDTH], preferred_element_type=F32)
        part = c // (DIFF_HEADS // 2)
        if part < 2:
            for half in range(2):
                blk = y[:, half * LANES:(half + 1) * LANES]
                ms = jnp.dot((blk * blk).astype(BF16), seg, preferred_element_type=F32) * (1.0 / HEAD_DIM)
                yn = blk * lax.rsqrt(ms + EPS) * gains[part]
                out = yn * cos + _swap_halves(yn, first_half) * sin
                o_ref[0, :, (2 * c + half) * LANES:(2 * c + half + 1) * LANES] = out.astype(BF16)
        else:
            o_ref[0, :, c * MXU_WIDTH:(c + 1) * MXU_WIDTH] = y.astype(BF16)


def _inproj(x, mod, w, row_tabs, consts, n_lat, n_out, even):
    b, t, d = x.shape
    tm = _pick_tile(t, (768, 256))
    kern = _inproj_even_kernel if even else _inproj_odd_kernel
    in_specs = [
        pl.BlockSpec((1, tm, d), lambda bi, i: (bi, i, 0)),
        pl.BlockSpec((1, N_MOD, d), lambda bi, i: (bi, 0, 0)),
        pl.BlockSpec((1, N_MOD, d), lambda bi, i: (b, 0, 0)),
        _resident_spec(w.shape),
    ]
    in_specs += [pl.BlockSpec((tm, LANES), lambda bi, i: (i, 0)) for _ in row_tabs]
    in_specs += [_const_spec(c.shape) for c in consts]
    return pl.pallas_call(
        functools.partial(kern, n_lat=n_lat, tm=tm),
        grid=(b, t // tm),
        in_specs=in_specs,
        out_specs=pl.BlockSpec((1, tm, n_out), lambda bi, i: (bi, i, 0)),
        out_shape=jax.ShapeDtypeStruct((b, t, n_out), BF16),
        compiler_params=_cparams("parallel", "parallel"),
        name="inproj_even" if even else "inproj_odd",
    )(x, mod, mod, w, *row_tabs, *consts)


def _fold_kernel(ud_ref, ua_ref, ub_ref, p_ref, m_ref, *, nb, tf):
    j = pl.program_id(0)
    row = lax.broadcasted_iota(jnp.int32, (tf, tf), 0)
    col = lax.broadcasted_iota(jnp.int32, (tf, tf), 1)
    flip = ((row + col == tf) & (row >= 1)).astype(BF16)
    corner = ((row == 0) & (col == 0) & (j > 0)).astype(BF16)
    for bi in range(nb):
        mirrored = (jnp.dot(flip, ua_ref[bi], preferred_element_type=F32)
                    + jnp.dot(corner, ub_ref[bi], preferred_element_type=F32))
        ud = ud_ref[bi].astype(F32)
        p_ref[bi] = (ud + mirrored).astype(BF16)
        m_ref[bi] = (ud - mirrored).astype(BF16)


def _dft_kernel(c_ref, s_ref, p_ref, m_ref, mid_ref, cc_ref, sc_ref, o_ref, acc_c, acc_s, *, nb, tk, scale):
    i = pl.program_id(0)
    j = pl.program_id(1)

    @pl.when(j == 0)
    def _():
        acc_c[...] = jnp.zeros_like(acc_c)
        acc_s[...] = jnp.zeros_like(acc_s)

    cm = c_ref[...]
    sm = s_ref[...]
    for bi in range(nb):
        acc_c[bi] += jnp.dot(cm, p_ref[bi], preferred_element_type=F32)
        acc_s[bi] += jnp.dot(sm, m_ref[bi], preferred_element_type=F32)

    @pl.when(j == pl.num_programs(1) - 1)
    def _():
        cc = cc_ref[...]
        sc = sc_ref[...]
        k = i * tk + lax.broadcasted_iota(jnp.int32, (tk, 1), 0)
        sign = (1 - 2 * (k & 1)).astype(F32)
        for bi in range(nb):
            mid = mid_ref[bi, 0:1, :].astype(F32)
            for g in range(FNET_GROUPS):
                sl = slice(g * FNET_GROUP_DIM, (g + 1) * FNET_GROUP_DIM)
                a = (acc_c[bi, :, sl] + sign * mid[:, sl]).astype(BF16)
                bm = acc_s[bi, :, sl].astype(BF16)
                y = jnp.dot(a, cc, preferred_element_type=F32) - jnp.dot(bm, sc, preferred_element_type=F32)
                o_ref[bi, :, sl] = (y * scale).astype(BF16)


def _angle_tables(n_rows, n_cols, stride, period):
    k = jnp.arange(n_rows, dtype=jnp.int32)[:, None]
    n = jnp.arange(n_cols, dtype=jnp.int32)[None, :]
    ang = ((k * n * stride) % period).astype(F32) * (2.0 * math.pi / period)
    return jnp.cos(ang), jnp.sin(ang)


def _dft_tables(n):
    ca, sa = _angle_tables(n // LANES, n // 2, LANES, n)
    cb, sb = _angle_tables(LANES, n // 2, 1, n)
    ca, sa, cb, sb = ca[:, None, :], sa[:, None, :], cb[None, :, :], sb[None, :, :]
    cm = (ca * cb - sa * sb).astype(BF16).reshape(n, n // 2)
    sm = (sa * cb + ca * sb).astype(BF16).reshape(n, n // 2)
    return cm, sm


def _fourier_seq(hin, row0, n, chan_tabs):
    b = hin.shape[0]
    half = n // 2
    tf = _pick_tile(half, (256, 128))
    off_f, last = row0 // tf, n // tf - 1
    seq = lambda index: pl.BlockSpec((b, tf, FNET_WIDTH), lambda j: (0, off_f + index(j), 0))
    folded = jax.ShapeDtypeStruct((b, half, FNET_WIDTH), BF16)
    u_plus, u_minus = pl.pallas_call(
        functools.partial(_fold_kernel, nb=b, tf=tf),
        grid=(half // tf,),
        in_specs=[seq(lambda j: j), seq(lambda j: last - j), seq(lambda j: jnp.minimum(last + 1 - j, last))],
        out_specs=[pl.BlockSpec((b, tf, FNET_WIDTH), lambda j: (0, j, 0))] * 2,
        out_shape=[folded, folded],
        compiler_params=_cparams("parallel"),
        name=f"fourier_fold_{n}",
    )(hin, hin, hin)

    cm, sm = _dft_tables(n)
    cc, sc = chan_tabs
    tk = _pick_tile(n, (1024, 512, 256))
    tn = _pick_tile(half, (1024, 512, 256, 128))
    mid_rows = 16
    folded_spec = pl.BlockSpec((b, tn, FNET_WIDTH), lambda i, j: (0, j, 0))
    return pl.pallas_call(
        functools.partial(_dft_kernel, nb=b, tk=tk, scale=1.0 / math.sqrt(n * FNET_GROUP_DIM)),
        grid=(n // tk, half // tn),
        in_specs=[
            pl.BlockSpec((tk, tn), lambda i, j: (i, j)),
            pl.BlockSpec((tk, tn), lambda i, j: (i, j)),
            folded_spec,
            folded_spec,
            pl.BlockSpec((b, mid_rows, FNET_WIDTH), lambda i, j: (0, (row0 + half) // mid_rows, 0)),
            _const_spec((FNET_GROUP_DIM, FNET_GROUP_DIM)),
            _const_spec((FNET_GROUP_DIM, FNET_GROUP_DIM)),
        ],
        out_specs=pl.BlockSpec((b, tk, FNET_WIDTH), lambda i, j: (0, i, 0)),
        out_shape=jax.ShapeDtypeStruct((b, n, FNET_WIDTH), BF16),
        scratch_shapes=[pltpu.VMEM((b, tk, FNET_WIDTH), F32), pltpu.VMEM((b, tk, FNET_WIDTH), F32)],
        compiler_params=_cparams("parallel", "arbitrary"),
        name=f"fourier_seq_{n}",
    )(cm, sm, u_plus, u_minus, hin, cc, sc)


def _log_sigmoid(x):
    return jnp.minimum(x, 0.0) - jnp.log(1.0 + jnp.exp(-jnp.abs(x)))


def _retention_kernel(q_ref, k_ref, v_ref, g_ref, decf_ref, decb_ref, o_ref,
                      of_scr, ob_scr, dec_scr, *, n_lat, n_ctx):
    c = RET_CHUNK
    pos_i = lax.broadcasted_iota(jnp.int32, (c, c), 0).astype(F32)
    pos_j = lax.broadcasted_iota(jnp.int32, (c, c), 1).astype(F32)
    lg_f = _log_sigmoid(decf_ref[0])
    lg_b = _log_sigmoid(decb_ref[0])
    rel = pos_i - pos_j
    mask_f = rel >= 0.0
    mask_b = rel < 0.0
    dec_scr[0] = jnp.where(mask_f, jnp.exp(lg_f * jnp.where(mask_f, rel, 0.0)), 0.0)
    dec_scr[1] = jnp.where(mask_b, jnp.exp(lg_b * jnp.where(mask_b, -rel, 0.0)), 0.0)
    dec_scr[2] = jnp.exp(lg_f * (pos_i + 1.0))
    dec_scr[3] = jnp.exp(lg_b * (c - pos_i))
    dec_scr[4] = jnp.exp(lg_f * (c - 1.0 - pos_i))
    dec_scr[5] = jnp.exp(lg_b * pos_i)
    dec_scr[6] = jnp.broadcast_to(jnp.exp(lg_f * c), (c, c))
    dec_scr[7] = jnp.broadcast_to(jnp.exp(lg_b * c), (c, c))

    def one_dir(d, start, out_scr, state):
        qc = q_ref[0, pl.ds(start, c), :]
        kc = k_ref[0, pl.ds(start, c), :]
        vc = v_ref[0, pl.ds(start, c), :]
        scores = lax.dot_general(qc, kc, (((1,), (1,)), ((), ())), preferred_element_type=F32)
        intra = jnp.dot((scores * dec_scr[d]).astype(BF16), vc, preferred_element_type=F32)
        inter = jnp.dot(qc, state.astype(BF16), preferred_element_type=F32)
        out_scr[pl.ds(start, c), :] = intra + dec_scr[2 + d] * inter
        vd = (vc.astype(F32) * dec_scr[4 + d]).astype(BF16)
        kv = lax.dot_general(kc, vd, (((0,), (0,)), ((), ())), preferred_element_type=F32)
        return dec_scr[6 + d] * state + kv

    s_f = jnp.zeros((LANES, LANES), F32)
    s_b = jnp.zeros((LANES, LANES), F32)
    n_ctx_chunks = n_ctx // c
    for ci in range(n_ctx_chunks):
        s_f = one_dir(0, n_lat + ci * c, of_scr, s_f)
        s_b = one_dir(1, n_lat + (n_ctx_chunks - 1 - ci) * c, ob_scr, s_b)

    n_lat_chunks = n_lat // c

    def body(i, states):
        s_f = one_dir(0, pl.multiple_of(i * c, c), of_scr, states[0])
        s_b = one_dir(1, pl.multiple_of((n_lat_chunks - 1 - i) * c, c), ob_scr, states[1])
        return s_f, s_b

    lax.fori_loop(0, n_lat_chunks, body, (s_f, s_b), unroll=_pick_tile(n_lat_chunks, (8, 4, 2, 1)))

    def finish(i, carry):
        r0 = pl.multiple_of(i * c, c)
        o = of_scr[pl.ds(r0, c), :] + ob_scr[pl.ds(r0, c), :]
        o = o * lax.rsqrt(jnp.mean(o * o, axis=-1, keepdims=True) + EPS)
        g = g_ref[0, pl.ds(r0, c), :].astype(F32)
        o_ref[0, pl.ds(r0, c), :] = (o * (g * _sigmoid(g))).astype(BF16)
        return carry

    n_chunks = (n_lat + n_ctx) // c
    lax.fori_loop(0, n_chunks, finish, 0, unroll=_pick_tile(n_chunks, (6, 3, 2, 1)))


def _retention(hin, dec_f, dec_b, n_lat):
    b, t, _ = hin.shape
    h = RET_HEADS
    seq = lambda col0: pl.BlockSpec((1, t, LANES), lambda bi, hi: (bi, 0, col0 + hi))
    dec = pl.BlockSpec((1, 1, LANES), lambda bi, hi: (hi, 0, 0))
    return pl.pallas_call(
        functools.partial(_retention_kernel, n_lat=n_lat, n_ctx=t - n_lat),
        grid=(b, h),
        in_specs=[seq(4), seq(8), seq(12), seq(16), dec, dec],
        out_specs=pl.BlockSpec((1, t, LANES), lambda bi, hi: (bi, 0, hi)),
        out_shape=jax.ShapeDtypeStruct((b, t, h * LANES), BF16),
        scratch_shapes=[
            pltpu.VMEM((t, LANES), F32),
            pltpu.VMEM((t, LANES), F32),
            pltpu.VMEM((8, RET_CHUNK, LANES), F32),
        ],
        compiler_params=_cparams("parallel", "parallel"),
        name="retention",
    )(hin, hin, hin, hin, dec_f, dec_b)


def _mixer_residual(x, a_refs, w_refs, modl_ref, modc_ref, row0, n_lat):
    y = jnp.dot(a_refs[0][0], w_refs[0][...], preferred_element_type=F32)
    for a_ref, w_ref in zip(a_refs[1:], w_refs[1:]):
        y = y + jnp.dot(a_ref[0], w_ref[...], preferred_element_type=F32)
    return x + _gate_rows(modl_ref, modc_ref, 2, row0, x.shape[0], n_lat) * y


def _mixer_specs(parts, weights, tm):
    specs = [pl.BlockSpec((1, tm, p.shape[2]), lambda bi, i: (bi, i, 0)) for p in parts]
    return specs + [_resident_spec(w.shape) for w in weights]


def _attn_kernel(prev_ref, q_ref, k_ref, v_ref, lam_ref, sg_ref, o_ref, *, tk, lam_init):
    del prev_ref
    tq = q_ref.shape[1]
    n_keys = k_ref.shape[1]
    q = q_ref[0]
    lane = lax.broadcasted_iota(jnp.int32, (tq, LANES), 1)
    zero = jnp.zeros_like(q)
    q_sub = (jnp.where(lane < HEAD_DIM, q, zero), jnp.where(lane >= HEAD_DIM, q, zero))
    ones = jnp.ones((tk, LANES), BF16)
    m = [None, None]
    acc = [None, None]
    for j in range(n_keys // tk):
        kc = k_ref[0, j * tk:(j + 1) * tk, :]
        v1 = jnp.concatenate([v_ref[0, j * tk:(j + 1) * tk, :], ones], axis=1)
        for sub in range(2):
            s = lax.dot_general(q_sub[sub], kc, (((1,), (1,)), ((), ())), preferred_element_type=F32)
            mx = jnp.max(s, axis=-1, keepdims=True)
            if j == 0:
                m[sub] = mx
                p = jnp.exp2(s - mx).astype(BF16)
                acc[sub] = jnp.dot(p, v1, preferred_element_type=F32)
            else:
                m_new = jnp.maximum(m[sub], mx)
                alpha = jnp.exp2(m[sub] - m_new)
                p = jnp.exp2(s - m_new).astype(BF16)
                acc[sub] = alpha * acc[sub] + jnp.dot(p, v1, preferred_element_type=F32)
                m[sub] = m_new

    lv = lam_ref[...]
    s1 = jnp.sum(lv[0:1] * lv[1:2], axis=-1, keepdims=True)
    s2 = jnp.sum(lv[2:3] * lv[3:4], axis=-1, keepdims=True)
    lam = jnp.exp(s1) - jnp.exp(s2) + lam_init
    o = acc[0][:, :LANES] / acc[0][:, LANES:] - lam * (acc[1][:, :LANES] / acc[1][:, LANES:])
    o = o * lax.rsqrt(jnp.mean(o * o, axis=-1, keepdims=True) + EPS)
    o_ref[0] = (o * sg_ref[...] * (1.0 - lam_init)).astype(BF16)


def _attention(out, qkv, lam_vecs, sub_gain, lam_init, n_lat, ctx_queries):
    b, t, _ = qkv.shape
    h = DIFF_HEADS
    n_ctx = t - n_lat
    if ctx_queries:
        tq, n_q_tiles, q_off = n_ctx, 1, n_lat // n_ctx
        keys, k_off = n_ctx, n_lat // n_ctx
        tk = n_ctx
    else:
        tq = _pick_tile(n_lat, (1024, 512, 256))
        n_q_tiles, q_off = n_lat // tq, 0
        keys, k_off = t, 0
        tk = _pick_tile(t, (768, 256))
    return pl.pallas_call(
        functools.partial(_attn_kernel, tk=tk, lam_init=lam_init),
        grid=(b, h, n_q_tiles),
        in_specs=[
            pl.BlockSpec(memory_space=pl.ANY),
            pl.BlockSpec((1, tq, LANES), lambda bi, hi, i: (bi, q_off + i, hi)),
            pl.BlockSpec((1, keys, LANES), lambda bi, hi, i: (bi, k_off, h + hi)),
            pl.BlockSpec((1, keys, LANES), lambda bi, hi, i: (bi, k_off, 2 * h + hi)),
            _const_spec((4, LANES)),
            _const_spec((1, LANES)),
        ],
        out_specs=pl.BlockSpec((1, tq, LANES), lambda bi, hi, i: (bi, q_off + i, hi)),
        out_shape=jax.ShapeDtypeStruct((b, t, h * LANES), BF16),
        input_output_aliases={0: 0},
        compiler_params=_cparams("parallel", "parallel", "arbitrary"),
        name="diff_attn_ctx" if ctx_queries else "diff_attn_lat",
    )(out, qkv, qkv, qkv, lam_vecs, sub_gain)


def _ffn_kernel(*refs, n_parts, n_lat, tm, tf):
    x_ref, modl_ref, modc_ref = refs[:3]
    a_refs, w_refs = refs[3:3 + n_parts], refs[3 + n_parts:3 + 2 * n_parts]
    w1_ref, w3_ref, w2_ref, o_ref = refs[3 + 2 * n_parts:]
    row0 = pl.program_id(1) * tm
    x = _mixer_residual(x_ref[0], a_refs, w_refs, modl_ref, modc_ref, row0, n_lat)
    h = _modulated(x, modl_ref, modc_ref, 3, row0, n_lat).astype(BF16)
    f_dim = w1_ref.shape[1]
    y = jnp.zeros((tm, x.shape[1]), F32)
    for c in range(f_dim // tf):
        sl = slice(c * tf, (c + 1) * tf)
        a = jnp.dot(h, w1_ref[:, sl], preferred_element_type=F32)
        g = jnp.dot(h, w3_ref[:, sl], preferred_element_type=F32)
        u = (a * _sigmoid(a) * g).astype(BF16)
        y = y + jnp.dot(u, w2_ref[sl, :], preferred_element_type=F32)
    gate = _gate_rows(modl_ref, modc_ref, 5, row0, tm, n_lat)
    o_ref[0] = x + gate * y


def _ffn(x, mod, parts, out_weights, w1, w3, w2, n_lat):
    b, t, d = x.shape
    tm = _pick_tile(t, (768, 256))
    tf = _pick_tile(w1.shape[1], (256, 128))
    return pl.pallas_call(
        functools.partial(_ffn_kernel, n_parts=len(parts), n_lat=n_lat, tm=tm, tf=tf),
        grid=(b, t // tm),
        in_specs=[
            pl.BlockSpec((1, tm, d), lambda bi, i: (bi, i, 0)),
            pl.BlockSpec((1, N_MOD, d), lambda bi, i: (bi, 0, 0)),
            pl.BlockSpec((1, N_MOD, d), lambda bi, i: (b, 0, 0)),
            *_mixer_specs(parts, out_weights, tm),
            _resident_spec(w1.shape),
            _resident_spec(w3.shape),
            _resident_spec(w2.shape),
        ],
        out_specs=pl.BlockSpec((1, tm, d), lambda bi, i: (bi, i, 0)),
        out_shape=jax.ShapeDtypeStruct((b, t, d), F32),
        input_output_aliases={0: 0},
        compiler_params=_cparams("parallel", "parallel"),
        name="mixer_out_ffn",
    )(x, mod, mod, *parts, *out_weights, w1, w3, w2)


ROUTE_E1, ROUTE_E2, ROUTE_G1, ROUTE_G2 = 0, 1, 2, 3
SLABS = D_MODEL // LANES
DMA_LOOP_UNROLL = 8


def _top2_route(logits):
    tm = logits.shape[0]
    lane = lax.broadcasted_iota(jnp.int32, (tm, LANES), 1)
    neg = jnp.float32(-jnp.inf)
    lg = jnp.where(lane < N_EXPERTS, logits, neg)
    m1 = jnp.max(lg, axis=-1, keepdims=True)
    i1 = jnp.min(jnp.where(lg == m1, lane, LANES), axis=-1, keepdims=True)
    lg2 = jnp.where(lane == i1, neg, lg)
    m2 = jnp.max(lg2, axis=-1, keepdims=True)
    i2 = jnp.min(jnp.where(lg2 == m2, lane, LANES), axis=-1, keepdims=True)
    e = jnp.exp(m2 - m1)
    g1 = 1.0 / (1.0 + e)
    g2 = e / (1.0 + e)
    rec = jnp.where(lane == ROUTE_E1, i1.astype(F32), 0.0) + jnp.where(lane == ROUTE_E2, i2.astype(F32), 0.0)
    return rec + jnp.where(lane == ROUTE_G1, g1, 0.0) + jnp.where(lane == ROUTE_G2, g2, 0.0)


def _slab_store(ref, val, n_rows):
    for s in range(SLABS):
        ref[pl.ds(s, n_rows, stride=SLABS), :] = val[:, s * LANES:(s + 1) * LANES]


def _slab_load(ref, n_rows):
    return jnp.concatenate([ref[pl.ds(s, n_rows, stride=SLABS), :] for s in range(SLABS)], axis=1)


def _mixer_out_kernel(*refs, n_parts, n_lat, tm):
    x_ref, modl_ref, modc_ref = refs[:3]
    a_refs, w_refs = refs[3:3 + n_parts], refs[3 + n_parts:3 + 2 * n_parts]
    o_ref = refs[3 + 2 * n_parts]
    row0 = pl.program_id(1) * tm
    o_ref[0] = _mixer_residual(x_ref[0], a_refs, w_refs, modl_ref, modc_ref, row0, n_lat)


def _mixer_out(x, mod, parts, out_weights, n_lat, n_rows):
    b, t, d = x.shape
    tm = _pick_tile(math.gcd(n_rows, x.shape[1]), (768, 512, 256))
    return pl.pallas_call(
        functools.partial(_mixer_out_kernel, n_parts=len(parts), n_lat=n_lat, tm=tm),
        grid=(b, n_rows // tm),
        in_specs=[
            pl.BlockSpec((1, tm, d), lambda bi, i: (bi, i, 0)),
            pl.BlockSpec((1, N_MOD, d), lambda bi, i: (bi, 0, 0)),
            pl.BlockSpec((1, N_MOD, d), lambda bi, i: (b, 0, 0)),
            *_mixer_specs(parts, out_weights, tm),
        ],
        out_specs=pl.BlockSpec((1, tm, d), lambda bi, i: (bi, i, 0)),
        out_shape=jax.ShapeDtypeStruct((b, t, d), F32),
        input_output_aliases={0: 0},
        compiler_params=_cparams("parallel", "parallel"),
        name="mixer_out",
    )(x, mod, mod, *parts, *out_weights)


def _route_kernel(x_ref, modl_ref, modc_ref, wr_ref, h_ref, rec_ref, *, n_lat, tm):
    row0 = pl.program_id(1) * tm
    hf = _modulated(x_ref[0], modl_ref, modc_ref, 3, row0, n_lat)
    logits = jnp.dot(hf, wr_ref[...], preferred_element_type=F32, precision=lax.Precision.HIGHEST)
    rec_ref[0] = _top2_route(logits)
    _slab_store(h_ref, hf, tm)


def _moe_route(x, mod, w_router, n_lat, n_rows):
    b, _, d = x.shape
    tm = _pick_tile(math.gcd(n_rows, x.shape[1]), (768, 512, 256))
    tiles = n_rows // tm
    return pl.pallas_call(
        functools.partial(_route_kernel, n_lat=n_lat, tm=tm),
        grid=(b, tiles),
        in_specs=[
            pl.BlockSpec((1, tm, d), lambda bi, i: (bi, i, 0)),
            pl.BlockSpec((1, N_MOD, d), lambda bi, i: (bi, 0, 0)),
            pl.BlockSpec((1, N_MOD, d), lambda bi, i: (b, 0, 0)),
            _const_spec((d, LANES)),
        ],
        out_specs=[
            pl.BlockSpec((tm * SLABS, LANES), lambda bi, i: (bi * tiles + i, 0)),
            pl.BlockSpec((1, tm, LANES), lambda bi, i: (bi, i, 0)),
        ],
        out_shape=[
            jax.ShapeDtypeStruct((b * n_rows * SLABS, LANES), F32),
            jax.ShapeDtypeStruct((b, n_rows, LANES), F32),
        ],
        compiler_params=_cparams("parallel", "parallel"),
        name="moe_route",
    )(x, mod, mod, w_router)


def _row_copy(src_ref, src_row, dst_ref, dst_row, sem):
    return pltpu.make_async_copy(
        src_ref.at[pl.ds(pl.multiple_of(src_row * SLABS, SLABS), SLABS), :],
        dst_ref.at[pl.ds(pl.multiple_of(dst_row * SLABS, SLABS), SLABS), :],
        sem)


def _dispatch_kernel(dest_ref, h_ref, init_ref, xs_ref, sem, *, tm):
    del init_ref

    def start(r, carry):
        for slot in range(2):
            _row_copy(h_ref, r, xs_ref, dest_ref[0, 0, 2 * r + slot], sem).start(priority=slot)
        return carry

    lax.fori_loop(0, tm, start, 0, unroll=DMA_LOOP_UNROLL)
    for _ in range(2):
        pltpu.make_async_copy(h_ref, xs_ref.at[pl.ds(0, tm * SLABS), :], sem).wait()


def _moe_dispatch(h_slabs, dest, n_sorted_rows):
    n_tok = h_slabs.shape[0] // SLABS
    tm = _pick_tile(n_tok, (512, 256))
    steps = n_tok // tm
    return pl.pallas_call(
        functools.partial(_dispatch_kernel, tm=tm),
        grid=(steps,),
        in_specs=[
            pl.BlockSpec((1, 1, 2 * tm), lambda i: (i, 0, 0), memory_space=pltpu.SMEM),
            pl.BlockSpec((tm * SLABS, LANES), lambda i: (i, 0)),
            pl.BlockSpec(memory_space=pl.ANY),
        ],
        out_specs=pl.BlockSpec(memory_space=pl.ANY),
        out_shape=jax.ShapeDtypeStruct((n_sorted_rows * SLABS, LANES), F32),
        scratch_shapes=[pltpu.SemaphoreType.DMA(())],
        input_output_aliases={2: 0},
        compiler_params=_cparams("arbitrary"),
        name="moe_dispatch",
    )(dest.reshape(steps, 1, 2 * tm), h_slabs, jnp.zeros((n_sorted_rows * SLABS, LANES), F32))


def _experts_kernel(te_ref, used_ref, x_ref, w1_ref, w3_ref, w2_ref, o_ref, acc_scr, *, tm, tf, n_steps_f):
    del te_ref
    i = pl.program_id(0)
    f = pl.program_id(1)
    valid = i < used_ref[0]
    n_f = pl.num_programs(1)

    def step(first, last):
        h = _slab_load(x_ref, tm).astype(BF16)
        f_blk = w1_ref.shape[2]
        y = None if first else acc_scr[...]
        for c in range(f_blk // tf):
            sl = slice(c * tf, (c + 1) * tf)
            a = jnp.dot(h, w1_ref[0, :, sl], preferred_element_type=F32)
            g = jnp.dot(h, w3_ref[0, :, sl], preferred_element_type=F32)
            u = (a * _sigmoid(a) * g).astype(BF16)
            part = jnp.dot(u, w2_ref[0, sl, :], preferred_element_type=F32)
            y = part if y is None else y + part
        if last:
            _slab_store(o_ref, y, tm)
        else:
            acc_scr[...] = y

    if n_steps_f == 1:
        pl.when(valid)(lambda: step(True, True))
    else:
        pl.when(valid & (f == 0))(lambda: step(True, False))
        pl.when(valid & (f == n_f - 1))(lambda: step(False, True))
        if n_steps_f > 2:
            pl.when(valid & (f > 0) & (f < n_f - 1))(lambda: step(False, False))

    @pl.when(jnp.logical_not(valid) & (f == n_f - 1))
    def _():
        _slab_store(o_ref, jnp.zeros(acc_scr.shape, F32), tm)


def _moe_experts(xs, tile_expert, n_used, w1, w3, w2, tm):
    n_exp, d, f_dim = w1.shape
    n_tiles = xs.shape[0] // (tm * SLABS)
    f_blk = _pick_tile(f_dim, (1792, 512, 256))
    tf = f_blk
    n_f = f_dim // f_blk

    def f_eff(i, f, used):
        return jnp.where(i < used[0], f, n_f - 1)

    grid_spec = pltpu.PrefetchScalarGridSpec(
        num_scalar_prefetch=2,
        grid=(n_tiles, n_f),
        in_specs=[
            pl.BlockSpec((tm * SLABS, LANES), lambda i, f, te, used: (jnp.minimum(i, used[0] - 1), 0)),
            pl.BlockSpec((1, d, f_blk), lambda i, f, te, used: (te[i], 0, f_eff(i, f, used))),
            pl.BlockSpec((1, d, f_blk), lambda i, f, te, used: (te[i], 0, f_eff(i, f, used))),
            pl.BlockSpec((1, f_blk, d), lambda i, f, te, used: (te[i], f_eff(i, f, used), 0)),
        ],
        out_specs=pl.BlockSpec((tm * SLABS, LANES), lambda i, f, te, used: (i, 0)),
        scratch_shapes=[pltpu.VMEM((tm, d), F32)],
    )
    return pl.pallas_call(
        functools.partial(_experts_kernel, tm=tm, tf=tf, n_steps_f=n_f),
        grid_spec=grid_spec,
        out_shape=jax.ShapeDtypeStruct(xs.shape, F32),
        compiler_params=_cparams("arbitrary", "arbitrary"),
        name="moe_experts",
    )(tile_expert, n_used, xs, w1, w3, w2)


def _combine_kernel(dcur_ref, dnext_ref, x_ref, modl_ref, modc_ref, rec_ref, ys_ref, o_ref, buf, sems,
                    *, n_lat, tm, tiles_per_batch):
    i = pl.program_id(0)
    n = pl.num_programs(0)

    def start_gather(dest_ref, slot_buf):
        def body(r, carry):
            for slot in range(2):
                _row_copy(ys_ref, dest_ref[0, 0, 2 * r + slot], buf.at[slot_buf, slot], r,
                          sems.at[slot_buf]).start(priority=slot)
            return carry
        lax.fori_loop(0, tm, body, 0, unroll=DMA_LOOP_UNROLL)

    def wait_gather(slot_buf):
        for slot in range(2):
            pltpu.make_async_copy(ys_ref.at[pl.ds(0, tm * SLABS), :], buf.at[slot_buf, slot], sems.at[slot_buf]).wait()

    @pl.when(i == 0)
    def _():
        start_gather(dcur_ref, 0)

    for parity in range(2):
        @pl.when((i + 1 < n) & ((i + 1) % 2 == parity))
        def _():
            start_gather(dnext_ref, parity)

    for parity in range(2):
        @pl.when(i % 2 == parity)
        def _():
            wait_gather(parity)
            rec = rec_ref[0]
            y = (rec[:, ROUTE_G1:ROUTE_G1 + 1] * _slab_load(buf.at[parity, 0], tm)
                 + rec[:, ROUTE_G2:ROUTE_G2 + 1] * _slab_load(buf.at[parity, 1], tm))
            row0 = (i % tiles_per_batch) * tm
            gate = _gate_rows(modl_ref, modc_ref, 5, row0, tm, n_lat)
            o_ref[0] = x_ref[0] + gate * y


def _moe_combine(x, mod, rec, ys, dest, n_lat, n_rows):
    b, t, d = x.shape
    tm = _pick_tile(n_rows, (256,))
    tpb = n_rows // tm
    n_tiles = b * tpb
    dest3 = dest.reshape(n_tiles, 1, 2 * tm)
    tok = lambda i: (i // tpb, i % tpb, 0)
    return pl.pallas_call(
        functools.partial(_combine_kernel, n_lat=n_lat, tm=tm, tiles_per_batch=tpb),
        grid=(n_tiles,),
        in_specs=[
            pl.BlockSpec((1, 1, 2 * tm), lambda i: (i, 0, 0), memory_space=pltpu.SMEM),
            pl.BlockSpec((1, 1, 2 * tm), lambda i: (jnp.minimum(i + 1, n_tiles - 1), 0, 0),
                         memory_space=pltpu.SMEM),
            pl.BlockSpec((1, tm, d), tok),
            pl.BlockSpec((1, N_MOD, d), lambda i: (i // tpb, 0, 0)),
            pl.BlockSpec((1, N_MOD, d), lambda i: (b, 0, 0)),
            pl.BlockSpec((1, tm, LANES), tok),
            pl.BlockSpec(memory_space=pl.ANY),
        ],
        out_specs=pl.BlockSpec((1, tm, d), tok),
        out_shape=jax.ShapeDtypeStruct((b, n_rows, d), F32),
        scratch_shapes=[pltpu.VMEM((2, 2, tm * SLABS, LANES), F32), pltpu.SemaphoreType.DMA((2,))],
        input_output_aliases={2: 0} if n_rows == t else {},
        compiler_params=_cparams("arbitrary"),
        name="moe_combine",
    )(dest3, dest3, x, mod, mod, rec, ys)


def _dispatch_plan(rec, tm, n_tiles):
    experts = rec[:, ROUTE_E1:ROUTE_E2 + 1].astype(jnp.int32).reshape(-1)
    onehot = (experts[:, None] == jnp.arange(N_EXPERTS, dtype=jnp.int32)[None, :]).astype(jnp.int32)
    csum = jnp.cumsum(onehot, axis=0)
    counts = csum[-1]
    padded = (counts + tm - 1) // tm * tm
    ends = jnp.cumsum(padded)
    dest = jnp.sum(onehot * (csum - 1 + (ends - padded)[None, :]), axis=1)
    n_used = (ends[-1] // tm).astype(jnp.int32)
    tile_start = jnp.arange(n_tiles, dtype=jnp.int32) * tm
    tile_expert = jnp.sum((tile_start[:, None] >= ends[None, :]).astype(jnp.int32), axis=1)
    tile_expert = jnp.minimum(tile_expert, tile_expert[n_used - 1])
    return dest.astype(jnp.int32), tile_expert.astype(jnp.int32), n_used.reshape(1)


def _moe(x, mod, w_router, w1, w3, w2, first_expert, n_lat, n_rows):
    b, _, d = x.shape
    n_tok = b * n_rows
    tm = 512
    n_tiles = (2 * n_tok + N_EXPERTS * (tm - 1)) // tm
    h_slabs, rec = _moe_route(x, mod, w_router, n_lat, n_rows)
    dest, tile_expert, n_used = _dispatch_plan(rec.reshape(n_tok, LANES), tm, n_tiles)
    xs = _moe_dispatch(h_slabs, dest, n_tiles * tm)
    ys = _moe_experts(xs, tile_expert + first_expert, n_used, w1, w3, w2, tm)
    return _moe_combine(x, mod, rec, ys, dest, n_lat, n_rows)


def _rot_cols(w):
    half = HEAD_DIM // 2
    return jnp.concatenate([-w[..., half:], w[..., :half]], axis=-1)


def _with_rot(w_heads):
    d, n, _ = w_heads.shape
    return jnp.concatenate([w_heads, _rot_cols(w_heads)], axis=-1).reshape(d, n * LANES)


def _gain_lanes(gain):
    return jnp.tile(gain.astype(F32), LANES // HEAD_DIM).reshape(1, LANES)


def _head_sum_matrix():
    idx = jnp.arange(LANES, dtype=jnp.int32) // HEAD_DIM
    return (idx[:, None] == idx[None, :]).astype(BF16)


def _rope_table(n_lat, n_ctx):
    rows = n_lat // GRID_W
    row = jnp.repeat(jnp.arange(rows, dtype=F32), GRID_W)
    col = jnp.tile(jnp.arange(GRID_W, dtype=F32), rows)
    quarter = HEAD_DIM // 4
    inv_freq = ROPE_BASE ** (-jnp.arange(quarter, dtype=F32) / quarter)
    ang = jnp.concatenate([row[:, None] * inv_freq, col[:, None] * inv_freq], axis=-1)
    cos, sin = jnp.cos(ang), jnp.sin(ang)
    ones = jnp.ones((n_ctx, HEAD_DIM), F32)
    zeros = jnp.zeros((n_ctx, HEAD_DIM), F32)
    rows_of = lambda lat, ctx: jnp.concatenate([lat, ctx], axis=0)
    dup = rows_of(jnp.concatenate([cos, cos, sin, sin], axis=-1), jnp.concatenate([ones, zeros], axis=-1))
    cos_packed = rows_of(jnp.concatenate([cos] * 4, axis=-1), jnp.concatenate([ones, ones], axis=-1))
    sin_packed = rows_of(jnp.concatenate([-sin, sin, -sin, sin], axis=-1), jnp.concatenate([zeros, zeros], axis=-1))
    return dup, cos_packed, sin_packed


def _chan_tables():
    idx = jnp.arange(FNET_GROUP_DIM, dtype=jnp.int32)
    ang = ((idx[:, None] * idx[None, :]) % FNET_GROUP_DIM).astype(F32) * (2.0 * math.pi / FNET_GROUP_DIM)
    return jnp.cos(ang).astype(BF16), jnp.sin(ang).astype(BF16)


def kernel(x, c, ctx, c_ctx, w_mod, b_mod, w_in_even, w_out_even, ret_decay_fwd, ret_decay_bwd,
           ffn_w1, ffn_w3, ffn_w2, w_in_odd, w_out_odd, q_norm_gain, k_norm_gain,
           lambda_q1, lambda_k1, lambda_q2, lambda_k2, subln_gain, w_router, moe_w1, moe_w3, moe_w2):
    b, n_lat, d = x.shape
    n_ctx = ctx.shape[1]
    depth = w_mod.shape[0]
    assert d == D_MODEL and b < MOD_ROWS and n_lat % n_ctx == 0 and n_ctx % RET_CHUNK == 0

    stream = jnp.concatenate([x, ctx], axis=1)
    cond = jnp.zeros((MOD_ROWS, d), F32).at[:b].set(c).at[b].set(c_ctx)
    mods = _adaln(cond, w_mod, b_mod).reshape(depth, MOD_ROWS, N_MOD, d)
    tab, cos_packed, sin_packed = _rope_table(n_lat, n_ctx)
    chan_tabs = _chan_tables()
    stack_experts = lambda w: w.astype(BF16).reshape((-1,) + w.shape[2:])
    moe_w1_all, moe_w3_all, moe_w2_all = stack_experts(moe_w1), stack_experts(moe_w3), stack_experts(moe_w2)

    for layer in range(depth):
        i = layer // 2
        mod = mods[layer]
        if layer % 2 == 0:
            w = w_in_even[i]
            hq = RET_HEADS * HEAD_DIM
            wq = w[:, FNET_WIDTH:FNET_WIDTH + hq].reshape(d, RET_HEADS, HEAD_DIM)
            wk = w[:, FNET_WIDTH + hq:FNET_WIDTH + 2 * hq].reshape(d, RET_HEADS, HEAD_DIM)
            w_in = jnp.concatenate(
                [w[:, :FNET_WIDTH], _with_rot(wq), _with_rot(wk), w[:, FNET_WIDTH + 2 * hq:]], axis=1).astype(BF16)
            hin = _inproj(stream, mod, w_in, [tab], [], n_lat, w_in.shape[1], True)
            four = jnp.concatenate(
                [_fourier_seq(hin, 0, n_lat, chan_tabs), _fourier_seq(hin, n_lat, n_ctx, chan_tabs)], axis=1)
            dec_f = jnp.broadcast_to(ret_decay_fwd[i].astype(F32)[:, None, None], (RET_HEADS, 1, LANES))
            dec_b = jnp.broadcast_to(ret_decay_bwd[i].astype(F32)[:, None, None], (RET_HEADS, 1, LANES))
            ret = _retention(hin, dec_f, dec_b, n_lat)
            w_out = w_out_even[i].astype(BF16)
            stream = _ffn(stream, mod, [four, ret], [w_out[:FNET_WIDTH], w_out[FNET_WIDTH:]],
                          ffn_w1[i].astype(BF16), ffn_w3[i].astype(BF16), ffn_w2[i].astype(BF16), n_lat)
        else:
            consts = [_gain_lanes(q_norm_gain[i]), _gain_lanes(k_norm_gain[i]), _head_sum_matrix()]
            qkv = _inproj(stream, mod, w_in_odd[i].astype(BF16), [cos_packed, sin_packed], consts, n_lat,
                          3 * DIFF_HEADS * LANES, False)
            lam_init = 0.8 - 0.6 * float(np.exp(-0.3 * layer))
            lam_vecs = jnp.zeros((4, LANES), F32).at[:, :HEAD_DIM].set(
                jnp.stack([lambda_q1[i], lambda_k1[i], lambda_q2[i], lambda_k2[i]]).astype(F32))
            sub_gain = subln_gain[i].astype(F32).reshape(1, LANES)
            att = jnp.zeros((b, n_lat + n_ctx, DIFF_HEADS * LANES), BF16)
            last = layer == depth - 1
            n_rows = n_lat if last else n_lat + n_ctx
            att = _attention(att, qkv, lam_vecs, sub_gain, lam_init, n_lat, False)
            if not last:
                att = _attention(att, qkv, lam_vecs, sub_gain, lam_init, n_lat, True)
            wr = jnp.zeros((d, LANES), F32).at[:, :N_EXPERTS].set(w_router[i].astype(F32))
            stream = _mixer_out(stream, mod, [att], [w_out_odd[i].astype(BF16)], n_lat, n_rows)
            stream = _moe(stream, mod, wr, moe_w1_all, moe_w3_all, moe_w2_all, i * N_EXPERTS, n_lat, n_rows)
    return stream[:, :n_lat]
```

```python
import functools
import math

import jax
import jax.numpy as jnp
import numpy as np
from jax import lax
from jax.experimental import pallas as pl
from jax.experimental.pallas import tpu as pltpu

F32 = jnp.float32
BF16 = jnp.bfloat16

D_MODEL = 1024
GRID_W = 64
HEAD_DIM = 64
LANES = 128
MXU_WIDTH = 256
FNET_GROUPS = 4
FNET_GROUP_DIM = 128
FNET_WIDTH = FNET_GROUPS * FNET_GROUP_DIM
RET_HEADS = 4
RET_CHUNK = 128
DIFF_HEADS = 8
N_EXPERTS = 8
N_MOD = 6
ROPE_BASE = 10000.0
EPS = 1e-6
LOG2_E = math.log2(math.e)
VMEM_LIMIT_BYTES = 56 * 1024 * 1024
MOD_ROWS = 8


def _cparams(*sem):
    return pltpu.CompilerParams(dimension_semantics=sem, vmem_limit_bytes=VMEM_LIMIT_BYTES)


def _pick_tile(n, candidates):
    for c in candidates:
        if n % c == 0:
            return c
    raise ValueError(f"no tile in {candidates} divides {n}")


def _const_spec(shape):
    nd = len(shape)
    return pl.BlockSpec(shape, lambda *_: (0,) * nd)


def _resident_spec(shape):
    nd = len(shape)
    return pl.BlockSpec(shape, lambda *_: (0,) * nd, pipeline_mode=pl.Buffered(1))


def _sigmoid(x):
    return 1.0 / (1.0 + jnp.exp(-x))


def _modulated(x, modl_ref, modc_ref, k_shift, row0, n_lat):
    tm = x.shape[0]
    ms = jnp.mean(x * x, axis=-1, keepdims=True)
    xn = x * lax.rsqrt(ms + EPS)
    rows = row0 + lax.broadcasted_iota(jnp.int32, (tm, 1), 0)
    is_ctx = rows >= n_lat
    shift = jnp.where(is_ctx, modc_ref[0, k_shift:k_shift + 1, :], modl_ref[0, k_shift:k_shift + 1, :])
    scale = jnp.where(is_ctx, modc_ref[0, k_shift + 1:k_shift + 2, :], modl_ref[0, k_shift + 1:k_shift + 2, :])
    return xn * (1.0 + scale) + shift


def _gate_rows(modl_ref, modc_ref, k_gate, row0, tm, n_lat):
    rows = row0 + lax.broadcasted_iota(jnp.int32, (tm, 1), 0)
    return jnp.where(rows >= n_lat, modc_ref[0, k_gate:k_gate + 1, :], modl_ref[0, k_gate:k_gate + 1, :])


def _adaln_kernel(c_ref, w_ref, b_ref, o_ref):
    c = c_ref[...]
    s = c * _sigmoid(c)
    o_ref[0] = jnp.dot(s, w_ref[0], preferred_element_type=F32, precision=lax.Precision.HIGHEST) + b_ref[0]


def _adaln(cond_rows, w_mod, b_mod):
    depth, d, n = w_mod.shape
    tn = _pick_tile(n, (1536, 1024, 512))
    return pl.pallas_call(
        _adaln_kernel,
        grid=(depth, n // tn),
        in_specs=[
            _const_spec((MOD_ROWS, d)),
            pl.BlockSpec((1, d, tn), lambda l, j: (l, 0, j)),
            pl.BlockSpec((1, 1, tn), lambda l, j: (l, 0, j)),
        ],
        out_specs=pl.BlockSpec((1, MOD_ROWS, tn), lambda l, j: (l, 0, j)),
        out_shape=jax.ShapeDtypeStruct((depth, MOD_ROWS, n), F32),
        compiler_params=_cparams("parallel", "parallel"),
        name="adaln",
    )(cond_rows, w_mod, b_mod.reshape(depth, 1, n))


def _rope_dup(blk, tab):
    t = blk * tab
    return t + pltpu.roll(t, HEAD_DIM, axis=1)


def _inproj_even_kernel(x_ref, modl_ref, modc_ref, w_ref, tab_ref, o_ref, *, n_lat, tm):
    row0 = pl.program_id(1) * tm
    h = _modulated(x_ref[0], modl_ref, modc_ref, 0, row0, n_lat).astype(BF16)
    tab = tab_ref[...]
    lane = lax.broadcasted_iota(jnp.int32, (tm, LANES), 1)
    n_out = o_ref.shape[2]
    for c in range(n_out // MXU_WIDTH):
        y = jnp.dot(h, w_ref[:, c * MXU_WIDTH:(c + 1) * MXU_WIDTH], preferred_element_type=F32)
        for half in range(2):
            blk = y[:, half * LANES:(half + 1) * LANES]
            col = c * 2 + half
            if 4 <= col < 8:
                blk = _rope_dup(blk, tab)
            elif 8 <= col < 12:
                blk = jnp.where(lane < HEAD_DIM, _rope_dup(blk, tab) * (HEAD_DIM ** -0.5), 0.0)
            o_ref[0, :, col * LANES:(col + 1) * LANES] = blk.astype(BF16)


def _swap_halves(y, first_half):
    fwd = pltpu.roll(y, HEAD_DIM // 2, axis=1)
    bwd = pltpu.roll(y, LANES - HEAD_DIM // 2, axis=1)
    return jnp.where(first_half, bwd, fwd)


def _inproj_odd_kernel(x_ref, modl_ref, modc_ref, w_ref, cos_ref, sin_ref, gq_ref, gk_ref, seg_ref, o_ref,
                       *, n_lat, tm):
    row0 = pl.program_id(1) * tm
    h = _modulated(x_ref[0], modl_ref, modc_ref, 0, row0, n_lat).astype(BF16)
    cos = cos_ref[...]
    sin = sin_ref[...]
    seg = seg_ref[...]
    lane = lax.broadcasted_iota(jnp.int32, (tm, LANES), 1)
    first_half = lane % HEAD_DIM < HEAD_DIM // 2
    gains = (gq_ref[...] * (HEAD_DIM ** -0.5 * LOG2_E), gk_ref[...])
    for c in range(w_ref.shape[1] // MXU_WIDTH):
        y = jnp.dot(h, w_ref[:, c * MXU_WIDTH:(c + 1) * MXU_WIDTH], preferred_element_type=F32)
        part = c // (DIFF_HEADS // 2)
        if part < 2:
            for half in range(2):
                blk = y[:, half * LANES:(half + 1) * LANES]
                ms = jnp.dot((blk * blk).astype(BF16), seg, preferred_element_type=F32) * (1.0 / HEAD_DIM)
                yn = blk * lax.rsqrt(ms + EPS) * gains[part]
                out = yn * cos + _swap_halves(yn, first_half) * sin
                o_ref[0, :, (2 * c + half) * LANES:(2 * c + half + 1) * LANES] = out.astype(BF16)
        else:
            o_ref[0, :, c * MXU_WIDTH:(c + 1) * MXU_WIDTH] = y.astype(BF16)


def _inproj(x, mod, w, row_tabs, consts, n_lat, n_out, even):
    b, t, d = x.shape
    tm = _pick_tile(t, (768, 256))
    kern = _inproj_even_kernel if even else _inproj_odd_kernel
    in_specs = [
        pl.BlockSpec((1, tm, d), lambda bi, i: (bi, i, 0)),
        pl.BlockSpec((1, N_MOD, d), lambda bi, i: (bi, 0, 0)),
        pl.BlockSpec((1, N_MOD, d), lambda bi, i: (b, 0, 0)),
        _resident_spec(w.shape),
    ]
    in_specs += [pl.BlockSpec((tm, LANES), lambda bi, i: (i, 0)) for _ in row_tabs]
    in_specs += [_const_spec(c.shape) for c in consts]
    return pl.pallas_call(
        functools.partial(kern, n_lat=n_lat, tm=tm),
        grid=(b, t // tm),
        in_specs=in_specs,
        out_specs=pl.BlockSpec((1, tm, n_out), lambda bi, i: (bi, i, 0)),
        out_shape=jax.ShapeDtypeStruct((b, t, n_out), BF16),
        compiler_params=_cparams("parallel", "parallel"),
        name="inproj_even" if even else "inproj_odd",
    )(x, mod, mod, w, *row_tabs, *consts)


def _fold_kernel(ud_ref, ua_ref, ub_ref, p_ref, m_ref, *, nb, tf):
    j = pl.program_id(0)
    row = lax.broadcasted_iota(jnp.int32, (tf, tf), 0)
    col = lax.broadcasted_iota(jnp.int32, (tf, tf), 1)
    flip = ((row + col == tf) & (row >= 1)).astype(BF16)
    corner = ((row == 0) & (col == 0) & (j > 0)).astype(BF16)
    for bi in range(nb):
        mirrored = (jnp.dot(flip, ua_ref[bi], preferred_element_type=F32)
                    + jnp.dot(corner, ub_ref[bi], preferred_element_type=F32))
        ud = ud_ref[bi].astype(F32)
        p_ref[bi] = (ud + mirrored).astype(BF16)
        m_ref[bi] = (ud - mirrored).astype(BF16)


def _dft_kernel(c_ref, s_ref, p_ref, m_ref, mid_ref, cc_ref, sc_ref, o_ref, acc_c, acc_s, *, nb, tk, scale):
    i = pl.program_id(0)
    j = pl.program_id(1)

    @pl.when(j == 0)
    def _():
        acc_c[...] = jnp.zeros_like(acc_c)
        acc_s[...] = jnp.zeros_like(acc_s)

    cm = c_ref[...]
    sm = s_ref[...]
    for bi in range(nb):
        acc_c[bi] += jnp.dot(cm, p_ref[bi], preferred_element_type=F32)
        acc_s[bi] += jnp.dot(sm, m_ref[bi], preferred_element_type=F32)

    @pl.when(j == pl.num_programs(1) - 1)
    def _():
        cc = cc_ref[...]
        sc = sc_ref[...]
        k = i * tk + lax.broadcasted_iota(jnp.int32, (tk, 1), 0)
        sign = (1 - 2 * (k & 1)).astype(F32)
        for bi in range(nb):
            mid = mid_ref[bi, 0:1, :].astype(F32)
            for g in range(FNET_GROUPS):
                sl = slice(g * FNET_GROUP_DIM, (g + 1) * FNET_GROUP_DIM)
                a = (acc_c[bi, :, sl] + sign * mid[:, sl]).astype(BF16)
                bm = acc_s[bi, :, sl].astype(BF16)
                y = jnp.dot(a, cc, preferred_element_type=F32) - jnp.dot(bm, sc, preferred_element_type=F32)
                o_ref[bi, :, sl] = (y * scale).astype(BF16)


def _angle_tables(n_rows, n_cols, stride, period):
    k = jnp.arange(n_rows, dtype=jnp.int32)[:, None]
    n = jnp.arange(n_cols, dtype=jnp.int32)[None, :]
    ang = ((k * n * stride) % period).astype(F32) * (2.0 * math.pi / period)
    return jnp.cos(ang), jnp.sin(ang)


def _dft_tables(n):
    ca, sa = _angle_tables(n // LANES, n // 2, LANES, n)
    cb, sb = _angle_tables(LANES, n // 2, 1, n)
    ca, sa, cb, sb = ca[:, None, :], sa[:, None, :], cb[None, :, :], sb[None, :, :]
    cm = (ca * cb - sa * sb).astype(BF16).reshape(n, n // 2)
    sm = (sa * cb + ca * sb).astype(BF16).reshape(n, n // 2)
    return cm, sm


def _fourier_seq(hin, row0, n, chan_tabs):
    b = hin.shape[0]
    half = n // 2
    tf = _pick_tile(half, (256, 128))
    off_f, last = row0 // tf, n // tf - 1
    seq = lambda index: pl.BlockSpec((b, tf, FNET_WIDTH), lambda j: (0, off_f + index(j), 0))
    folded = jax.ShapeDtypeStruct((b, half, FNET_WIDTH), BF16)
    u_plus, u_minus = pl.pallas_call(
        functools.partial(_fold_kernel, nb=b, tf=tf),
        grid=(half // tf,),
        in_specs=[seq(lambda j: j), seq(lambda j: last - j), seq(lambda j: jnp.minimum(last + 1 - j, last))],
        out_specs=[pl.BlockSpec((b, tf, FNET_WIDTH), lambda j: (0, j, 0))] * 2,
        out_shape=[folded, folded],
        compiler_params=_cparams("parallel"),
        name=f"fourier_fold_{n}",
    )(hin, hin, hin)

    cm, sm = _dft_tables(n)
    cc, sc = chan_tabs
    tk = _pick_tile(n, (1024, 512, 256))
    tn = _pick_tile(half, (1024, 512, 256, 128))
    mid_rows = 16
    folded_spec = pl.BlockSpec((b, tn, FNET_WIDTH), lambda i, j: (0, j, 0))
    return pl.pallas_call(
        functools.partial(_dft_kernel, nb=b, tk=tk, scale=1.0 / math.sqrt(n * FNET_GROUP_DIM)),
        grid=(n // tk, half // tn),
        in_specs=[
            pl.BlockSpec((tk, tn), lambda i, j: (i, j)),
            pl.BlockSpec((tk, tn), lambda i, j: (i, j)),
            folded_spec,
            folded_spec,
            pl.BlockSpec((b, mid_rows, FNET_WIDTH), lambda i, j: (0, (row0 + half) // mid_rows, 0)),
            _const_spec((FNET_GROUP_DIM, FNET_GROUP_DIM)),
            _const_spec((FNET_GROUP_DIM, FNET_GROUP_DIM)),
        ],
        out_specs=pl.BlockSpec((b, tk, FNET_WIDTH), lambda i, j: (0, i, 0)),
        out_shape=jax.ShapeDtypeStruct((b, n, FNET_WIDTH), BF16),
        scratch_shapes=[pltpu.VMEM((b, tk, FNET_WIDTH), F32), pltpu.VMEM((b, tk, FNET_WIDTH), F32)],
        compiler_params=_cparams("parallel", "arbitrary"),
        name=f"fourier_seq_{n}",
    )(cm, sm, u_plus, u_minus, hin, cc, sc)


def _log_sigmoid(x):
    return jnp.minimum(x, 0.0) - jnp.log(1.0 + jnp.exp(-jnp.abs(x)))


def _retention_kernel(q_ref, k_ref, v_ref, g_ref, decf_ref, decb_ref, o_ref,
                      of_scr, ob_scr, dec_scr, *, n_lat, n_ctx):
    c = RET_CHUNK
    pos_i = lax.broadcasted_iota(jnp.int32, (c, c), 0).astype(F32)
    pos_j = lax.broadcasted_iota(jnp.int32, (c, c), 1).astype(F32)
    lg_f = _log_sigmoid(decf_ref[0])
    lg_b = _log_sigmoid(decb_ref[0])
    rel = pos_i - pos_j
    mask_f = rel >= 0.0
    mask_b = rel < 0.0
    dec_scr[0] = jnp.where(mask_f, jnp.exp(lg_f * jnp.where(mask_f, rel, 0.0)), 0.0)
    dec_scr[1] = jnp.where(mask_b, jnp.exp(lg_b * jnp.where(mask_b, -rel, 0.0)), 0.0)
    dec_scr[2] = jnp.exp(lg_f * (pos_i + 1.0))
    dec_scr[3] = jnp.exp(lg_b * (c - pos_i))
    dec_scr[4] = jnp.exp(lg_f * (c - 1.0 - pos_i))
    dec_scr[5] = jnp.exp(lg_b * pos_i)
    dec_scr[6] = jnp.broadcast_to(jnp.exp(lg_f * c), (c, c))
    dec_scr[7] = jnp.broadcast_to(jnp.exp(lg_b * c), (c, c))

    def one_dir(d, start, out_scr, state):
        qc = q_ref[0, pl.ds(start, c), :]
        kc = k_ref[0, pl.ds(start, c), :]
        vc = v_ref[0, pl.ds(start, c), :]
        scores = lax.dot_general(qc, kc, (((1,), (1,)), ((), ())), preferred_element_type=F32)
        lhs = jnp.concatenate([(scores * dec_scr[d]).astype(BF16),
                               (qc.astype(F32) * dec_scr[2 + d]).astype(BF16)], axis=1)
        rhs = jnp.concatenate([vc, state.astype(BF16)], axis=0)
        out_scr[pl.ds(start, c), :] = jnp.dot(lhs, rhs, preferred_element_type=F32)
        vd = (vc.astype(F32) * dec_scr[4 + d]).astype(BF16)
        kv = lax.dot_general(kc, vd, (((0,), (0,)), ((), ())), preferred_element_type=F32)
        return dec_scr[6 + d] * state + kv

    s_f = jnp.zeros((LANES, LANES), F32)
    s_b = jnp.zeros((LANES, LANES), F32)
    n_ctx_chunks = n_ctx // c
    for ci in range(n_ctx_chunks):
        s_f = one_dir(0, n_lat + ci * c, of_scr, s_f)
        s_b = one_dir(1, n_lat + (n_ctx_chunks - 1 - ci) * c, ob_scr, s_b)

    n_lat_chunks = n_lat // c

    def body(i, states):
        s_f = one_dir(0, pl.multiple_of(i * c, c), of_scr, states[0])
        s_b = one_dir(1, pl.multiple_of((n_lat_chunks - 1 - i) * c, c), ob_scr, states[1])
        return s_f, s_b

    lax.fori_loop(0, n_lat_chunks, body, (s_f, s_b), unroll=_pick_tile(n_lat_chunks, (8, 4, 2, 1)))

    def finish(i, carry):
        r0 = pl.multiple_of(i * c, c)
        o = of_scr[pl.ds(r0, c), :] + ob_scr[pl.ds(r0, c), :]
        o = o * lax.rsqrt(jnp.mean(o * o, axis=-1, keepdims=True) + EPS)
        g = g_ref[0, pl.ds(r0, c), :].astype(F32)
        o_ref[0, pl.ds(r0, c), :] = (o * (g * _sigmoid(g))).astype(BF16)
        return carry

    n_chunks = (n_lat + n_ctx) // c
    lax.fori_loop(0, n_chunks, finish, 0, unroll=_pick_tile(n_chunks, (6, 3, 2, 1)))


def _retention(hin, dec_f, dec_b, n_lat):
    b, t, _ = hin.shape
    h = RET_HEADS
    seq = lambda col0: pl.BlockSpec((1, t, LANES), lambda bi, hi: (bi, 0, col0 + hi))
    dec = pl.BlockSpec((1, 1, LANES), lambda bi, hi: (hi, 0, 0))
    return pl.pallas_call(
        functools.partial(_retention_kernel, n_lat=n_lat, n_ctx=t - n_lat),
        grid=(b, h),
        in_specs=[seq(4), seq(8), seq(12), seq(16), dec, dec],
        out_specs=pl.BlockSpec((1, t, LANES), lambda bi, hi: (bi, 0, hi)),
        out_shape=jax.ShapeDtypeStruct((b, t, h * LANES), BF16),
        scratch_shapes=[
            pltpu.VMEM((t, LANES), F32),
            pltpu.VMEM((t, LANES), F32),
            pltpu.VMEM((8, RET_CHUNK, LANES), F32),
        ],
        compiler_params=_cparams("parallel", "parallel"),
        name="retention",
    )(hin, hin, hin, hin, dec_f, dec_b)


def _mixer_residual(x, a_refs, w_refs, modl_ref, modc_ref, row0, n_lat):
    y = jnp.dot(a_refs[0][0], w_refs[0][...], preferred_element_type=F32)
    for a_ref, w_ref in zip(a_refs[1:], w_refs[1:]):
        y = y + jnp.dot(a_ref[0], w_ref[...], preferred_element_type=F32)
    return x + _gate_rows(modl_ref, modc_ref, 2, row0, x.shape[0], n_lat) * y


def _mixer_specs(parts, weights, tm):
    specs = [pl.BlockSpec((1, tm, p.shape[2]), lambda bi, i: (bi, i, 0)) for p in parts]
    return specs + [_resident_spec(w.shape) for w in weights]


def _attn_kernel(prev_ref, q_ref, k_ref, v_ref, lam_ref, sg_ref, o_ref, *, tk, lam_init):
    del prev_ref
    tq = q_ref.shape[1]
    n_keys = k_ref.shape[1]
    q = q_ref[0]
    lane = lax.broadcasted_iota(jnp.int32, (tq, LANES), 1)
    zero = jnp.zeros_like(q)
    q_sub = (jnp.where(lane < HEAD_DIM, q, zero), jnp.where(lane >= HEAD_DIM, q, zero))
    ones = jnp.ones((tk, LANES), BF16)
    m = [None, None]
    acc = [None, None]
    for j in range(n_keys // tk):
        kc = k_ref[0, j * tk:(j + 1) * tk, :]
        v1 = jnp.concatenate([v_ref[0, j * tk:(j + 1) * tk, :], ones], axis=1)
        for sub in range(2):
            s = lax.dot_general(q_sub[sub], kc, (((1,), (1,)), ((), ())), preferred_element_type=F32)
            mx = jnp.max(s, axis=-1, keepdims=True)
            if j == 0:
                m[sub] = mx
                p = jnp.exp2(s - mx).astype(BF16)
                acc[sub] = jnp.dot(p, v1, preferred_element_type=F32)
            else:
                m_new = jnp.maximum(m[sub], mx)
                alpha = jnp.exp2(m[sub] - m_new)
                p = jnp.exp2(s - m_new).astype(BF16)
                acc[sub] = alpha * acc[sub] + jnp.dot(p, v1, preferred_element_type=F32)
                m[sub] = m_new

    lv = lam_ref[...]
    s1 = jnp.sum(lv[0:1] * lv[1:2], axis=-1, keepdims=True)
    s2 = jnp.sum(lv[2:3] * lv[3:4], axis=-1, keepdims=True)
    lam = jnp.exp(s1) - jnp.exp(s2) + lam_init
    o = acc[0][:, :LANES] / acc[0][:, LANES:] - lam * (acc[1][:, :LANES] / acc[1][:, LANES:])
    o = o * lax.rsqrt(jnp.mean(o * o, axis=-1, keepdims=True) + EPS)
    o_ref[0] = (o * sg_ref[...] * (1.0 - lam_init)).astype(BF16)


def _attention(out, qkv, lam_vecs, sub_gain, lam_init, n_lat, ctx_queries):
    b, t, _ = qkv.shape
    h = DIFF_HEADS
    n_ctx = t - n_lat
    if ctx_queries:
        tq, n_q_tiles, q_off = n_ctx, 1, n_lat // n_ctx
        keys, k_off = n_ctx, n_lat // n_ctx
        tk = n_ctx
    else:
        tq = _pick_tile(n_lat, (1024, 512, 256))
        n_q_tiles, q_off = n_lat // tq, 0
        keys, k_off = t, 0
        tk = _pick_tile(t, (768, 256))
    return pl.pallas_call(
        functools.partial(_attn_kernel, tk=tk, lam_init=lam_init),
        grid=(b, h, n_q_tiles),
        in_specs=[
            pl.BlockSpec(memory_space=pl.ANY),
            pl.BlockSpec((1, tq, LANES), lambda bi, hi, i: (bi, q_off + i, hi)),
            pl.BlockSpec((1, keys, LANES), lambda bi, hi, i: (bi, k_off, h + hi)),
            pl.BlockSpec((1, keys, LANES), lambda bi, hi, i: (bi, k_off, 2 * h + hi)),
            _const_spec((4, LANES)),
            _const_spec((1, LANES)),
        ],
        out_specs=pl.BlockSpec((1, tq, LANES), lambda bi, hi, i: (bi, q_off + i, hi)),
        out_shape=jax.ShapeDtypeStruct((b, t, h * LANES), BF16),
        input_output_aliases={0: 0},
        compiler_params=_cparams("parallel", "parallel", "arbitrary"),
        name="diff_attn_ctx" if ctx_queries else "diff_attn_lat",
    )(out, qkv, qkv, qkv, lam_vecs, sub_gain)


def _ffn_kernel(*refs, n_parts, n_lat, tm, tf):
    x_ref, modl_ref, modc_ref = refs[:3]
    a_refs, w_refs = refs[3:3 + n_parts], refs[3 + n_parts:3 + 2 * n_parts]
    w1_ref, w3_ref, w2_ref, o_ref = refs[3 + 2 * n_parts:]
    row0 = pl.program_id(1) * tm
    x = _mixer_residual(x_ref[0], a_refs, w_refs, modl_ref, modc_ref, row0, n_lat)
    h = _modulated(x, modl_ref, modc_ref, 3, row0, n_lat).astype(BF16)
    f_dim = w1_ref.shape[1]
    y = jnp.zeros((tm, x.shape[1]), F32)
    for c in range(f_dim // tf):
        sl = slice(c * tf, (c + 1) * tf)
        a = jnp.dot(h, w1_ref[:, sl], preferred_element_type=F32)
        g = jnp.dot(h, w3_ref[:, sl], preferred_element_type=F32)
        u = (a * _sigmoid(a) * g).astype(BF16)
        y = y + jnp.dot(u, w2_ref[sl, :], preferred_element_type=F32)
    gate = _gate_rows(modl_ref, modc_ref, 5, row0, tm, n_lat)
    o_ref[0] = x + gate * y


def _ffn(x, mod, parts, out_weights, w1, w3, w2, n_lat):
    b, t, d = x.shape
    tm = _pick_tile(t, (768, 256))
    tf = _pick_tile(w1.shape[1], (256, 128))
    return pl.pallas_call(
        functools.partial(_ffn_kernel, n_parts=len(parts), n_lat=n_lat, tm=tm, tf=tf),
        grid=(b, t // tm),
        in_specs=[
            pl.BlockSpec((1, tm, d), lambda bi, i: (bi, i, 0)),
            pl.BlockSpec((1, N_MOD, d), lambda bi, i: (bi, 0, 0)),
            pl.BlockSpec((1, N_MOD, d), lambda bi, i: (b, 0, 0)),
            *_mixer_specs(parts, out_weights, tm),
            _resident_spec(w1.shape),
            _resident_spec(w3.shape),
            _resident_spec(w2.shape),
        ],
        out_specs=pl.BlockSpec((1, tm, d), lambda bi, i: (bi, i, 0)),
        out_shape=jax.ShapeDtypeStruct((b, t, d), F32),
        input_output_aliases={0: 0},
        compiler_params=_cparams("parallel", "parallel"),
        name="mixer_out_ffn",
    )(x, mod, mod, *parts, *out_weights, w1, w3, w2)


ROUTE_E1, ROUTE_E2, ROUTE_G1, ROUTE_G2 = 0, 1, 2, 3
SLABS = D_MODEL // LANES
DMA_LOOP_UNROLL = 8


def _top2_route(logits):
    tm = logits.shape[0]
    lane = lax.broadcasted_iota(jnp.int32, (tm, LANES), 1)
    neg = jnp.float32(-jnp.inf)
    lg = jnp.where(lane < N_EXPERTS, logits, neg)
    m1 = jnp.max(lg, axis=-1, keepdims=True)
    i1 = jnp.min(jnp.where(lg == m1, lane, LANES), axis=-1, keepdims=True)
    lg2 = jnp.where(lane == i1, neg, lg)
    m2 = jnp.max(lg2, axis=-1, keepdims=True)
    i2 = jnp.min(jnp.where(lg2 == m2, lane, LANES), axis=-1, keepdims=True)
    e = jnp.exp(m2 - m1)
    g1 = 1.0 / (1.0 + e)
    g2 = e / (1.0 + e)
    rec = jnp.where(lane == ROUTE_E1, i1.astype(F32), 0.0) + jnp.where(lane == ROUTE_E2, i2.astype(F32), 0.0)
    return rec + jnp.where(lane == ROUTE_G1, g1, 0.0) + jnp.where(lane == ROUTE_G2, g2, 0.0)


def _slab_store(ref, val, n_rows):
    for s in range(SLABS):
        ref[pl.ds(s, n_rows, stride=SLABS), :] = val[:, s * LANES:(s + 1) * LANES]


def _slab_load(ref, n_rows):
    return jnp.concatenate([ref[pl.ds(s, n_rows, stride=SLABS), :] for s in range(SLABS)], axis=1)


def _mixer_out_kernel(*refs, n_parts, n_lat, tm):
    x_ref, modl_ref, modc_ref = refs[:3]
    a_refs, w_refs = refs[3:3 + n_parts], refs[3 + n_parts:3 + 2 * n_parts]
    o_ref = refs[3 + 2 * n_parts]
    row0 = pl.program_id(1) * tm
    o_ref[0] = _mixer_residual(x_ref[0], a_refs, w_refs, modl_ref, modc_ref, row0, n_lat)


def _mixer_out(x, mod, parts, out_weights, n_lat, n_rows):
    b, t, d = x.shape
    tm = _pick_tile(math.gcd(n_rows, x.shape[1]), (768, 512, 256))
    return pl.pallas_call(
        functools.partial(_mixer_out_kernel, n_parts=len(parts), n_lat=n_lat, tm=tm),
        grid=(b, n_rows // tm),
        in_specs=[
            pl.BlockSpec((1, tm, d), lambda bi, i: (bi, i, 0)),
            pl.BlockSpec((1, N_MOD, d), lambda bi, i: (bi, 0, 0)),
            pl.BlockSpec((1, N_MOD, d), lambda bi, i: (b, 0, 0)),
            *_mixer_specs(parts, out_weights, tm),
        ],
        out_specs=pl.BlockSpec((1, tm, d), lambda bi, i: (bi, i, 0)),
        out_shape=jax.ShapeDtypeStruct((b, t, d), F32),
        input_output_aliases={0: 0},
        compiler_params=_cparams("parallel", "parallel"),
        name="mixer_out",
    )(x, mod, mod, *parts, *out_weights)


def _route_kernel(x_ref, modl_ref, modc_ref, wr_ref, h_ref, rec_ref, *, n_lat, tm):
    row0 = pl.program_id(1) * tm
    hf = _modulated(x_ref[0], modl_ref, modc_ref, 3, row0, n_lat)
    logits = jnp.dot(hf, wr_ref[...], preferred_element_type=F32, precision=lax.Precision.HIGHEST)
    rec_ref[0] = _top2_route(logits)
    _slab_store(h_ref, hf, tm)


def _moe_route(x, mod, w_router, n_lat, n_rows):
    b, _, d = x.shape
    tm = _pick_tile(math.gcd(n_rows, x.shape[1]), (768, 512, 256))
    tiles = n_rows // tm
    return pl.pallas_call(
        functools.partial(_route_kernel, n_lat=n_lat, tm=tm),
        grid=(b, tiles),
        in_specs=[
            pl.BlockSpec((1, tm, d), lambda bi, i: (bi, i, 0)),
            pl.BlockSpec((1, N_MOD, d), lambda bi, i: (bi, 0, 0)),
            pl.BlockSpec((1, N_MOD, d), lambda bi, i: (b, 0, 0)),
            _const_spec((d, LANES)),
        ],
        out_specs=[
            pl.BlockSpec((tm * SLABS, LANES), lambda bi, i: (bi * tiles + i, 0)),
            pl.BlockSpec((1, tm, LANES), lambda bi, i: (bi, i, 0)),
        ],
        out_shape=[
            jax.ShapeDtypeStruct((b * n_rows * SLABS, LANES), F32),
            jax.ShapeDtypeStruct((b, n_rows, LANES), F32),
        ],
        compiler_params=_cparams("parallel", "parallel"),
        name="moe_route",
    )(x, mod, mod, w_router)


def _row_copy(src_ref, src_row, dst_ref, dst_row, sem):
    return pltpu.make_async_copy(
        src_ref.at[pl.ds(pl.multiple_of(src_row * SLABS, SLABS), SLABS), :],
        dst_ref.at[pl.ds(pl.multiple_of(dst_row * SLABS, SLABS), SLABS), :],
        sem)


def _dispatch_kernel(dest_ref, h_ref, init_ref, xs_ref, sem, *, tm):
    del init_ref

    def start(r, carry):
        for slot in range(2):
            _row_copy(h_ref, r, xs_ref, dest_ref[0, 0, 2 * r + slot], sem).start(priority=slot)
        return carry

    lax.fori_loop(0, tm, start, 0, unroll=DMA_LOOP_UNROLL)
    for _ in range(2):
        pltpu.make_async_copy(h_ref, xs_ref.at[pl.ds(0, tm * SLABS), :], sem).wait()


def _moe_dispatch(h_slabs, dest, n_sorted_rows):
    n_tok = h_slabs.shape[0] // SLABS
    tm = _pick_tile(n_tok, (512, 256))
    steps = n_tok // tm
    return pl.pallas_call(
        functools.partial(_dispatch_kernel, tm=tm),
        grid=(steps,),
        in_specs=[
            pl.BlockSpec((1, 1, 2 * tm), lambda i: (i, 0, 0), memory_space=pltpu.SMEM),
            pl.BlockSpec((tm * SLABS, LANES), lambda i: (i, 0)),
            pl.BlockSpec(memory_space=pl.ANY),
        ],
        out_specs=pl.BlockSpec(memory_space=pl.ANY),
        out_shape=jax.ShapeDtypeStruct((n_sorted_rows * SLABS, LANES), F32),
        scratch_shapes=[pltpu.SemaphoreType.DMA(())],
        input_output_aliases={2: 0},
        compiler_params=_cparams("arbitrary"),
        name="moe_dispatch",
    )(dest.reshape(steps, 1, 2 * tm), h_slabs, jnp.zeros((n_sorted_rows * SLABS, LANES), F32))


def _experts_kernel(te_ref, used_ref, x_ref, w1_ref, w3_ref, w2_ref, o_ref, acc_scr, *, tm, tf, n_steps_f):
    del te_ref
    i = pl.program_id(0)
    f = pl.program_id(1)
    valid = i < used_ref[0]
    n_f = pl.num_programs(1)

    def step(first, last):
        h = _slab_load(x_ref, tm).astype(BF16)
        f_blk = w1_ref.shape[2]
        y = None if first else acc_scr[...]
        for c in range(f_blk // tf):
            sl = slice(c * tf, (c + 1) * tf)
            a = jnp.dot(h, w1_ref[0, :, sl], preferred_element_type=F32)
            g = jnp.dot(h, w3_ref[0, :, sl], preferred_element_type=F32)
            u = (a * _sigmoid(a) * g).astype(BF16)
            part = jnp.dot(u, w2_ref[0, sl, :], preferred_element_type=F32)
            y = part if y is None else y + part
        if last:
            _slab_store(o_ref, y, tm)
        else:
            acc_scr[...] = y

    if n_steps_f == 1:
        pl.when(valid)(lambda: step(True, True))
    else:
        pl.when(valid & (f == 0))(lambda: step(True, False))
        pl.when(valid & (f == n_f - 1))(lambda: step(False, True))
        if n_steps_f > 2:
            pl.when(valid & (f > 0) & (f < n_f - 1))(lambda: step(False, False))

    @pl.when(jnp.logical_not(valid) & (f == n_f - 1))
    def _():
        _slab_store(o_ref, jnp.zeros(acc_scr.shape, F32), tm)


def _moe_experts(xs, tile_expert, n_used, w1, w3, w2, tm):
    n_exp, d, f_dim = w1.shape
    n_tiles = xs.shape[0] // (tm * SLABS)
    f_blk = _pick_tile(f_dim, (1792, 512, 256))
    tf = _pick_tile(f_blk, (256, 128))
    n_f = f_dim // f_blk

    def f_eff(i, f, used):
        return jnp.where(i < used[0], f, n_f - 1)

    grid_spec = pltpu.PrefetchScalarGridSpec(
        num_scalar_prefetch=2,
        grid=(n_tiles, n_f),
        in_specs=[
            pl.BlockSpec((tm * SLABS, LANES), lambda i, f, te, used: (jnp.minimum(i, used[0] - 1), 0)),
            pl.BlockSpec((1, d, f_blk), lambda i, f, te, used: (te[i], 0, f_eff(i, f, used))),
            pl.BlockSpec((1, d, f_blk), lambda i, f, te, used: (te[i], 0, f_eff(i, f, used))),
            pl.BlockSpec((1, f_blk, d), lambda i, f, te, used: (te[i], f_eff(i, f, used), 0)),
        ],
        out_specs=pl.BlockSpec((tm * SLABS, LANES), lambda i, f, te, used: (i, 0)),
        scratch_shapes=[pltpu.VMEM((tm, d), F32)],
    )
    return pl.pallas_call(
        functools.partial(_experts_kernel, tm=tm, tf=tf, n_steps_f=n_f),
        grid_spec=grid_spec,
        out_shape=jax.ShapeDtypeStruct(xs.shape, F32),
        compiler_params=_cparams("arbitrary", "arbitrary"),
        name="moe_experts",
    )(tile_expert, n_used, xs, w1, w3, w2)


def _combine_kernel(dcur_ref, dnext_ref, x_ref, modl_ref, modc_ref, rec_ref, ys_ref, o_ref, buf, sems,
                    *, n_lat, tm, tiles_per_batch):
    i = pl.program_id(0)
    n = pl.num_programs(0)

    def start_gather(dest_ref, slot_buf):
        def body(r, carry):
            for slot in range(2):
                _row_copy(ys_ref, dest_ref[0, 0, 2 * r + slot], buf.at[slot_buf, slot], r,
                          sems.at[slot_buf]).start(priority=slot)
            return carry
        lax.fori_loop(0, tm, body, 0, unroll=DMA_LOOP_UNROLL)

    def wait_gather(slot_buf):
        for slot in range(2):
            pltpu.make_async_copy(ys_ref.at[pl.ds(0, tm * SLABS), :], buf.at[slot_buf, slot], sems.at[slot_buf]).wait()

    @pl.when(i == 0)
    def _():
        start_gather(dcur_ref, 0)

    for parity in range(2):
        @pl.when((i + 1 < n) & ((i + 1) % 2 == parity))
        def _():
            start_gather(dnext_ref, parity)

    for parity in range(2):
        @pl.when(i % 2 == parity)
        def _():
            wait_gather(parity)
            rec = rec_ref[0]
            y = (rec[:, ROUTE_G1:ROUTE_G1 + 1] * _slab_load(buf.at[parity, 0], tm)
                 + rec[:, ROUTE_G2:ROUTE_G2 + 1] * _slab_load(buf.at[parity, 1], tm))
            row0 = (i % tiles_per_batch) * tm
            gate = _gate_rows(modl_ref, modc_ref, 5, row0, tm, n_lat)
            o_ref[0] = x_ref[0] + gate * y


def _moe_combine(x, mod, rec, ys, dest, n_lat, n_rows):
    b, t, d = x.shape
    tm = _pick_tile(n_rows, (256,))
    tpb = n_rows // tm
    n_tiles = b * tpb
    dest3 = dest.reshape(n_tiles, 1, 2 * tm)
    tok = lambda i: (i // tpb, i % tpb, 0)
    return pl.pallas_call(
        functools.partial(_combine_kernel, n_lat=n_lat, tm=tm, tiles_per_batch=tpb),
        grid=(n_tiles,),
        in_specs=[
            pl.BlockSpec((1, 1, 2 * tm), lambda i: (i, 0, 0), memory_space=pltpu.SMEM),
            pl.BlockSpec((1, 1, 2 * tm), lambda i: (jnp.minimum(i + 1, n_tiles - 1), 0, 0),
                         memory_space=pltpu.SMEM),
            pl.BlockSpec((1, tm, d), tok),
            pl.BlockSpec((1, N_MOD, d), lambda i: (i // tpb, 0, 0)),
            pl.BlockSpec((1, N_MOD, d), lambda i: (b, 0, 0)),
            pl.BlockSpec((1, tm, LANES), tok),
            pl.BlockSpec(memory_space=pl.ANY),
        ],
        out_specs=pl.BlockSpec((1, tm, d), tok),
        out_shape=jax.ShapeDtypeStruct((b, n_rows, d), F32),
        scratch_shapes=[pltpu.VMEM((2, 2, tm * SLABS, LANES), F32), pltpu.SemaphoreType.DMA((2,))],
        input_output_aliases={2: 0} if n_rows == t else {},
        compiler_params=_cparams("arbitrary"),
        name="moe_combine",
    )(dest3, dest3, x, mod, mod, rec, ys)


def _dispatch_plan(rec, tm, n_tiles):
    experts = rec[:, ROUTE_E1:ROUTE_E2 + 1].astype(jnp.int32).reshape(-1)
    onehot = (experts[:, None] == jnp.arange(N_EXPERTS, dtype=jnp.int32)[None, :]).astype(jnp.int32)
    csum = jnp.cumsum(onehot, axis=0)
    counts = csum[-1]
    padded = (counts + tm - 1) // tm * tm
    ends = jnp.cumsum(padded)
    dest = jnp.sum(onehot * (csum - 1 + (ends - padded)[None, :]), axis=1)
    n_used = (ends[-1] // tm).astype(jnp.int32)
    tile_start = jnp.arange(n_tiles, dtype=jnp.int32) * tm
    tile_expert = jnp.sum((tile_start[:, None] >= ends[None, :]).astype(jnp.int32), axis=1)
    tile_expert = jnp.minimum(tile_expert, tile_expert[n_used - 1])
    return dest.astype(jnp.int32), tile_expert.astype(jnp.int32), n_used.reshape(1)


def _moe(x, mod, w_router, w1, w3, w2, first_expert, n_lat, n_rows):
    b, _, d = x.shape
    n_tok = b * n_rows
    tm = 512
    n_tiles = (2 * n_tok + N_EXPERTS * (tm - 1)) // tm
    h_slabs, rec = _moe_route(x, mod, w_router, n_lat, n_rows)
    dest, tile_expert, n_used = _dispatch_plan(rec.reshape(n_tok, LANES), tm, n_tiles)
    xs = _moe_dispatch(h_slabs, dest, n_tiles * tm)
    ys = _moe_experts(xs, tile_expert + first_expert, n_used, w1, w3, w2, tm)
    return _moe_combine(x, mod, rec, ys, dest, n_lat, n_rows)


def _rot_cols(w):
    half = HEAD_DIM // 2
    return jnp.concatenate([-w[..., half:], w[..., :half]], axis=-1)


def _with_rot(w_heads):
    d, n, _ = w_heads.shape
    return jnp.concatenate([w_heads, _rot_cols(w_heads)], axis=-1).reshape(d, n * LANES)


def _gain_lanes(gain):
    return jnp.tile(gain.astype(F32), LANES // HEAD_DIM).reshape(1, LANES)


def _head_sum_matrix():
    idx = jnp.arange(LANES, dtype=jnp.int32) // HEAD_DIM
    return (idx[:, None] == idx[None, :]).astype(BF16)


def _rope_table(n_lat, n_ctx):
    rows = n_lat // GRID_W
    row = jnp.repeat(jnp.arange(rows, dtype=F32), GRID_W)
    col = jnp.tile(jnp.arange(GRID_W, dtype=F32), rows)
    quarter = HEAD_DIM // 4
    inv_freq = ROPE_BASE ** (-jnp.arange(quarter, dtype=F32) / quarter)
    ang = jnp.concatenate([row[:, None] * inv_freq, col[:, None] * inv_freq], axis=-1)
    cos, sin = jnp.cos(ang), jnp.sin(ang)
    ones = jnp.ones((n_ctx, HEAD_DIM), F32)
    zeros = jnp.zeros((n_ctx, HEAD_DIM), F32)
    rows_of = lambda lat, ctx: jnp.concatenate([lat, ctx], axis=0)
    dup = rows_of(jnp.concatenate([cos, cos, sin, sin], axis=-1), jnp.concatenate([ones, zeros], axis=-1))
    cos_packed = rows_of(jnp.concatenate([cos] * 4, axis=-1), jnp.concatenate([ones, ones], axis=-1))
    sin_packed = rows_of(jnp.concatenate([-sin, sin, -sin, sin], axis=-1), jnp.concatenate([zeros, zeros], axis=-1))
    return dup, cos_packed, sin_packed


def _chan_tables():
    idx = jnp.arange(FNET_GROUP_DIM, dtype=jnp.int32)
    ang = ((idx[:, None] * idx[None, :]) % FNET_GROUP_DIM).astype(F32) * (2.0 * math.pi / FNET_GROUP_DIM)
    return jnp.cos(ang).astype(BF16), jnp.sin(ang).astype(BF16)


def kernel(x, c, ctx, c_ctx, w_mod, b_mod, w_in_even, w_out_even, ret_decay_fwd, ret_decay_bwd,
           ffn_w1, ffn_w3, ffn_w2, w_in_odd, w_out_odd, q_norm_gain, k_norm_gain,
           lambda_q1, lambda_k1, lambda_q2, lambda_k2, subln_gain, w_router, moe_w1, moe_w3, moe_w2):
    b, n_lat, d = x.shape
    n_ctx = ctx.shape[1]
    depth = w_mod.shape[0]
    assert d == D_MODEL and b < MOD_ROWS and n_lat % n_ctx == 0 and n_ctx % RET_CHUNK == 0

    stream = jnp.concatenate([x, ctx], axis=1)
    cond = jnp.zeros((MOD_ROWS, d), F32).at[:b].set(c).at[b].set(c_ctx)
    mods = _adaln(cond, w_mod, b_mod).reshape(depth, MOD_ROWS, N_MOD, d)
    tab, cos_packed, sin_packed = _rope_table(n_lat, n_ctx)
    chan_tabs = _chan_tables()
    stack_experts = lambda w: w.astype(BF16).reshape((-1,) + w.shape[2:])
    moe_w1_all, moe_w3_all, moe_w2_all = stack_experts(moe_w1), stack_experts(moe_w3), stack_experts(moe_w2)

    for layer in range(depth):
        i = layer // 2
        mod = mods[layer]
        if layer % 2 == 0:
            w = w_in_even[i]
            hq = RET_HEADS * HEAD_DIM
            wq = w[:, FNET_WIDTH:FNET_WIDTH + hq].reshape(d, RET_HEADS, HEAD_DIM)
            wk = w[:, FNET_WIDTH + hq:FNET_WIDTH + 2 * hq].reshape(d, RET_HEADS, HEAD_DIM)
            w_in = jnp.concatenate(
                [w[:, :FNET_WIDTH], _with_rot(wq), _with_rot(wk), w[:, FNET_WIDTH + 2 * hq:]], axis=1).astype(BF16)
            hin = _inproj(stream, mod, w_in, [tab], [], n_lat, w_in.shape[1], True)
            four = jnp.concatenate(
                [_fourier_seq(hin, 0, n_lat, chan_tabs), _fourier_seq(hin, n_lat, n_ctx, chan_tabs)], axis=1)
            dec_f = jnp.broadcast_to(ret_decay_fwd[i].astype(F32)[:, None, None], (RET_HEADS, 1, LANES))
            dec_b = jnp.broadcast_to(ret_decay_bwd[i].astype(F32)[:, None, None], (RET_HEADS, 1, LANES))
            ret = _retention(hin, dec_f, dec_b, n_lat)
            w_out = w_out_even[i].astype(BF16)
            stream = _ffn(stream, mod, [four, ret], [w_out[:FNET_WIDTH], w_out[FNET_WIDTH:]],
                          ffn_w1[i].astype(BF16), ffn_w3[i].astype(BF16), ffn_w2[i].astype(BF16), n_lat)
        else:
            consts = [_gain_lanes(q_norm_gain[i]), _gain_lanes(k_norm_gain[i]), _head_sum_matrix()]
            qkv = _inproj(stream, mod, w_in_odd[i].astype(BF16), [cos_packed, sin_packed], consts, n_lat,
                          3 * DIFF_HEADS * LANES, False)
            lam_init = 0.8 - 0.6 * float(np.exp(-0.3 * layer))
            lam_vecs = jnp.zeros((4, LANES), F32).at[:, :HEAD_DIM].set(
                jnp.stack([lambda_q1[i], lambda_k1[i], lambda_q2[i], lambda_k2[i]]).astype(F32))
            sub_gain = subln_gain[i].astype(F32).reshape(1, LANES)
            att = jnp.zeros((b, n_lat + n_ctx, DIFF_HEADS * LANES), BF16)
            last = layer == depth - 1
            n_rows = n_lat if last else n_lat + n_ctx
            att = _attention(att, qkv, lam_vecs, sub_gain, lam_init, n_lat, False)
            if not last:
                att = _attention(att, qkv, lam_vecs, sub_gain, lam_init, n_lat, True)
            wr = jnp.zeros((d, LANES), F32).at[:, :N_EXPERTS].set(w_router[i].astype(F32))
            stream = _mixer_out(stream, mod, [att], [w_out_odd[i].astype(BF16)], n_lat, n_rows)
            stream = _moe(stream, mod, wr, moe_w1_all, moe_w3_all, moe_w2_all, i * N_EXPERTS, n_lat, n_rows)
    return stream[:, :n_lat]
```

```python
import functools
import math

import jax
import jax.numpy as jnp
import numpy as np
from jax import lax
from jax.experimental import pallas as pl
from jax.experimental.pallas import tpu as pltpu

F32 = jnp.float32
BF16 = jnp.bfloat16

D_MODEL = 1024
GRID_W = 64
HEAD_DIM = 64
LANES = 128
MXU_WIDTH = 256
FNET_GROUPS = 4
FNET_GROUP_DIM = 128
FNET_WIDTH = FNET_GROUPS * FNET_GROUP_DIM
RET_HEADS = 4
RET_CHUNK = 128
DIFF_HEADS = 8
N_EXPERTS = 8
N_MOD = 6
ROPE_BASE = 10000.0
EPS = 1e-6
LOG2_E = math.log2(math.e)
VMEM_LIMIT_BYTES = 56 * 1024 * 1024
MOD_ROWS = 8


def _cparams(*sem):
    return pltpu.CompilerParams(dimension_semantics=sem, vmem_limit_bytes=VMEM_LIMIT_BYTES)


def _pick_tile(n, candidates):
    for c in candidates:
        if n % c == 0:
            return c
    raise ValueError(f"no tile in {candidates} divides {n}")


def _const_spec(shape):
    nd = len(shape)
    return pl.BlockSpec(shape, lambda *_: (0,) * nd)


def _resident_spec(shape):
    nd = len(shape)
    return pl.BlockSpec(shape, lambda *_: (0,) * nd, pipeline_mode=pl.Buffered(1))


def _sigmoid(x):
    return 1.0 / (1.0 + jnp.exp(-x))


def _modulated(x, modl_ref, modc_ref, k_shift, row0, n_lat):
    tm = x.shape[0]
    ms = jnp.mean(x * x, axis=-1, keepdims=True)
    xn = x * lax.rsqrt(ms + EPS)
    rows = row0 + lax.broadcasted_iota(jnp.int32, (tm, 1), 0)
    is_ctx = rows >= n_lat
    shift = jnp.where(is_ctx, modc_ref[0, k_shift:k_shift + 1, :], modl_ref[0, k_shift:k_shift + 1, :])
    scale = jnp.where(is_ctx, modc_ref[0, k_shift + 1:k_shift + 2, :], modl_ref[0, k_shift + 1:k_shift + 2, :])
    return xn * (1.0 + scale) + shift


def _gate_rows(modl_ref, modc_ref, k_gate, row0, tm, n_lat):
    rows = row0 + lax.broadcasted_iota(jnp.int32, (tm, 1), 0)
    return jnp.where(rows >= n_lat, modc_ref[0, k_gate:k_gate + 1, :], modl_ref[0, k_gate:k_gate + 1, :])


def _adaln_kernel(c_ref, w_ref, b_ref, o_ref):
    c = c_ref[...]
    s = c * _sigmoid(c)
    o_ref[0] = jnp.dot(s, w_ref[0], preferred_element_type=F32, precision=lax.Precision.HIGHEST) + b_ref[0]


def _adaln(cond_rows, w_mod, b_mod):
    depth, d, n = w_mod.shape
    tn = _pick_tile(n, (1536, 1024, 512))
    return pl.pallas_call(
        _adaln_kernel,
        grid=(depth, n // tn),
        in_specs=[
            _const_spec((MOD_ROWS, d)),
            pl.BlockSpec((1, d, tn), lambda l, j: (l, 0, j)),
            pl.BlockSpec((1, 1, tn), lambda l, j: (l, 0, j)),
        ],
        out_specs=pl.BlockSpec((1, MOD_ROWS, tn), lambda l, j: (l, 0, j)),
        out_shape=jax.ShapeDtypeStruct((depth, MOD_ROWS, n), F32),
        compiler_params=_cparams("parallel", "parallel"),
        name="adaln",
    )(cond_rows, w_mod, b_mod.reshape(depth, 1, n))


def _rope_dup(blk, tab):
    t = blk * tab
    return t + pltpu.roll(t, HEAD_DIM, axis=1)


def _inproj_even_kernel(x_ref, modl_ref, modc_ref, w_ref, tab_ref, o_ref, *, n_lat, tm):
    row0 = pl.program_id(1) * tm
    h = _modulated(x_ref[0], modl_ref, modc_ref, 0, row0, n_lat).astype(BF16)
    tab = tab_ref[...]
    lane = lax.broadcasted_iota(jnp.int32, (tm, LANES), 1)
    n_out = o_ref.shape[2]
    for c in range(n_out // MXU_WIDTH):
        y = jnp.dot(h, w_ref[:, c * MXU_WIDTH:(c + 1) * MXU_WIDTH], preferred_element_type=F32)
        for half in range(2):
            blk = y[:, half * LANES:(half + 1) * LANES]
            col = c * 2 + half
            if 4 <= col < 8:
                blk = _rope_dup(blk, tab)
            elif 8 <= col < 12:
                blk = jnp.where(lane < HEAD_DIM, _rope_dup(blk, tab) * (HEAD_DIM ** -0.5), 0.0)
            o_ref[0, :, col * LANES:(col + 1) * LANES] = blk.astype(BF16)


def _swap_halves(y, first_half):
    fwd = pltpu.roll(y, HEAD_DIM // 2, axis=1)
    bwd = pltpu.roll(y, LANES - HEAD_DIM // 2, axis=1)
    return jnp.where(first_half, bwd, fwd)


def _inproj_odd_kernel(x_ref, modl_ref, modc_ref, w_ref, cos_ref, sin_ref, gq_ref, gk_ref, seg_ref, o_ref,
                       *, n_lat, tm):
    row0 = pl.program_id(1) * tm
    h = _modulated(x_ref[0], modl_ref, modc_ref, 0, row0, n_lat).astype(BF16)
    cos = cos_ref[...]
    sin = sin_ref[...]
    seg = seg_ref[...]
    lane = lax.broadcasted_iota(jnp.int32, (tm, LANES), 1)
    first_half = lane % HEAD_DIM < HEAD_DIM // 2
    gains = (gq_ref[...] * (HEAD_DIM ** -0.5 * LOG2_E), gk_ref[...])
    for c in range(w_ref.shape[1] // MXU_WIDTH):
        y = jnp.dot(h, w_ref[:, c * MXU_WIDTH:(c + 1) * MXU_WIDTH], preferred_element_type=F32)
        part = c // (DIFF_HEADS // 2)
        if part < 2:
            for half in range(2):
                blk = y[:, half * LANES:(half + 1) * LANES]
                ms = jnp.dot((blk * blk).astype(BF16), seg, preferred_element_type=F32) * (1.0 / HEAD_DIM)
                yn = blk * lax.rsqrt(ms + EPS) * gains[part]
                out = yn * cos + _swap_halves(yn, first_half) * sin
                o_ref[0, :, (2 * c + half) * LANES:(2 * c + half + 1) * LANES] = out.astype(BF16)
        else:
            o_ref[0, :, c * MXU_WIDTH:(c + 1) * MXU_WIDTH] = y.astype(BF16)


def _inproj(x, mod, w, row_tabs, consts, n_lat, n_out, even):
    b, t, d = x.shape
    tm = _pick_tile(t, (768, 256))
    kern = _inproj_even_kernel if even else _inproj_odd_kernel
    in_specs = [
        pl.BlockSpec((1, tm, d), lambda bi, i: (bi, i, 0)),
        pl.BlockSpec((1, N_MOD, d), lambda bi, i: (bi, 0, 0)),
        pl.BlockSpec((1, N_MOD, d), lambda bi, i: (b, 0, 0)),
        _resident_spec(w.shape),
    ]
    in_specs += [pl.BlockSpec((tm, LANES), lambda bi, i: (i, 0)) for _ in row_tabs]
    in_specs += [_const_spec(c.shape) for c in consts]
    return pl.pallas_call(
        functools.partial(kern, n_lat=n_lat, tm=tm),
        grid=(b, t // tm),
        in_specs=in_specs,
        out_specs=pl.BlockSpec((1, tm, n_out), lambda bi, i: (bi, i, 0)),
        out_shape=jax.ShapeDtypeStruct((b, t, n_out), BF16),
        compiler_params=_cparams("parallel", "parallel"),
        name="inproj_even" if even else "inproj_odd",
    )(x, mod, mod, w, *row_tabs, *consts)


def _fold_kernel(ud_ref, ua_ref, ub_ref, p_ref, m_ref, *, nb, tf):
    j = pl.program_id(0)
    row = lax.broadcasted_iota(jnp.int32, (tf, tf), 0)
    col = lax.broadcasted_iota(jnp.int32, (tf, tf), 1)
    flip = ((row + col == tf) & (row >= 1)).astype(BF16)
    corner = ((row == 0) & (col == 0) & (j > 0)).astype(BF16)
    for bi in range(nb):
        mirrored = (jnp.dot(flip, ua_ref[bi], preferred_element_type=F32)
                    + jnp.dot(corner, ub_ref[bi], preferred_element_type=F32))
        ud = ud_ref[bi].astype(F32)
        p_ref[bi] = (ud + mirrored).astype(BF16)
        m_ref[bi] = (ud - mirrored).astype(BF16)


def _dft_kernel(c_ref, s_ref, p_ref, m_ref, mid_ref, cc_ref, sc_ref, o_ref, acc_c, acc_s, *, nb, tk, scale):
    i = pl.program_id(0)
    j = pl.program_id(1)

    @pl.when(j == 0)
    def _():
        acc_c[...] = jnp.zeros_like(acc_c)
        acc_s[...] = jnp.zeros_like(acc_s)

    cm = c_ref[...]
    sm = s_ref[...]
    for bi in range(nb):
        acc_c[bi] += jnp.dot(cm, p_ref[bi], preferred_element_type=F32)
        acc_s[bi] += jnp.dot(sm, m_ref[bi], preferred_element_type=F32)

    @pl.when(j == pl.num_programs(1) - 1)
    def _():
        cc = cc_ref[...]
        sc = sc_ref[...]
        k = i * tk + lax.broadcasted_iota(jnp.int32, (tk, 1), 0)
        sign = (1 - 2 * (k & 1)).astype(F32)
        for bi in range(nb):
            mid = mid_ref[bi, 0:1, :].astype(F32)
            for g in range(FNET_GROUPS):
                sl = slice(g * FNET_GROUP_DIM, (g + 1) * FNET_GROUP_DIM)
                a = (acc_c[bi, :, sl] + sign * mid[:, sl]).astype(BF16)
                bm = acc_s[bi, :, sl].astype(BF16)
                y = jnp.dot(a, cc, preferred_element_type=F32) - jnp.dot(bm, sc, preferred_element_type=F32)
                o_ref[bi, :, sl] = (y * scale).astype(BF16)


def _angle_tables(n_rows, n_cols, stride, period):
    k = jnp.arange(n_rows, dtype=jnp.int32)[:, None]
    n = jnp.arange(n_cols, dtype=jnp.int32)[None, :]
    ang = ((k * n * stride) % period).astype(F32) * (2.0 * math.pi / period)
    return jnp.cos(ang), jnp.sin(ang)


def _dft_tables(n):
    ca, sa = _angle_tables(n // LANES, n // 2, LANES, n)
    cb, sb = _angle_tables(LANES, n // 2, 1, n)
    ca, sa, cb, sb = ca[:, None, :], sa[:, None, :], cb[None, :, :], sb[None, :, :]
    cm = (ca * cb - sa * sb).astype(BF16).reshape(n, n // 2)
    sm = (sa * cb + ca * sb).astype(BF16).reshape(n, n // 2)
    return cm, sm


def _fourier_seq(hin, row0, n, chan_tabs):
    b = hin.shape[0]
    half = n // 2
    tf = _pick_tile(half, (256, 128))
    off_f, last = row0 // tf, n // tf - 1
    seq = lambda index: pl.BlockSpec((b, tf, FNET_WIDTH), lambda j: (0, off_f + index(j), 0))
    folded = jax.ShapeDtypeStruct((b, half, FNET_WIDTH), BF16)
    u_plus, u_minus = pl.pallas_call(
        functools.partial(_fold_kernel, nb=b, tf=tf),
        grid=(half // tf,),
        in_specs=[seq(lambda j: j), seq(lambda j: last - j), seq(lambda j: jnp.minimum(last + 1 - j, last))],
        out_specs=[pl.BlockSpec((b, tf, FNET_WIDTH), lambda j: (0, j, 0))] * 2,
        out_shape=[folded, folded],
        compiler_params=_cparams("parallel"),
        name=f"fourier_fold_{n}",
    )(hin, hin, hin)

    cm, sm = _dft_tables(n)
    cc, sc = chan_tabs
    tk = _pick_tile(n, (1024, 512, 256))
    tn = _pick_tile(half, (1024, 512, 256, 128))
    mid_rows = 16
    folded_spec = pl.BlockSpec((b, tn, FNET_WIDTH), lambda i, j: (0, j, 0))
    return pl.pallas_call(
        functools.partial(_dft_kernel, nb=b, tk=tk, scale=1.0 / math.sqrt(n * FNET_GROUP_DIM)),
        grid=(n // tk, half // tn),
        in_specs=[
            pl.BlockSpec((tk, tn), lambda i, j: (i, j)),
            pl.BlockSpec((tk, tn), lambda i, j: (i, j)),
            folded_spec,
            folded_spec,
            pl.BlockSpec((b, mid_rows, FNET_WIDTH), lambda i, j: (0, (row0 + half) // mid_rows, 0)),
            _const_spec((FNET_GROUP_DIM, FNET_GROUP_DIM)),
            _const_spec((FNET_GROUP_DIM, FNET_GROUP_DIM)),
        ],
        out_specs=pl.BlockSpec((b, tk, FNET_WIDTH), lambda i, j: (0, i, 0)),
        out_shape=jax.ShapeDtypeStruct((b, n, FNET_WIDTH), BF16),
        scratch_shapes=[pltpu.VMEM((b, tk, FNET_WIDTH), F32), pltpu.VMEM((b, tk, FNET_WIDTH), F32)],
        compiler_params=_cparams("parallel", "arbitrary"),
        name=f"fourier_seq_{n}",
    )(cm, sm, u_plus, u_minus, hin, cc, sc)


def _log_sigmoid(x):
    return jnp.minimum(x, 0.0) - jnp.log(1.0 + jnp.exp(-jnp.abs(x)))


def _retention_kernel(q_ref, k_ref, v_ref, g_ref, decf_ref, decb_ref, o_ref,
                      of_scr, ob_scr, dec_scr, *, n_lat, n_ctx):
    c = RET_CHUNK
    pos_i = lax.broadcasted_iota(jnp.int32, (c, c), 0).astype(F32)
    pos_j = lax.broadcasted_iota(jnp.int32, (c, c), 1).astype(F32)
    lg_f = _log_sigmoid(decf_ref[0])
    lg_b = _log_sigmoid(decb_ref[0])
    rel = pos_i - pos_j
    mask_f = rel >= 0.0
    mask_b = rel < 0.0
    dec_scr[0] = jnp.where(mask_f, jnp.exp(lg_f * jnp.where(mask_f, rel, 0.0)), 0.0)
    dec_scr[1] = jnp.where(mask_b, jnp.exp(lg_b * jnp.where(mask_b, -rel, 0.0)), 0.0)
    dec_scr[2] = jnp.exp(lg_f * (pos_i + 1.0))
    dec_scr[3] = jnp.exp(lg_b * (c - pos_i))
    dec_scr[4] = jnp.exp(lg_f * (c - 1.0 - pos_i))
    dec_scr[5] = jnp.exp(lg_b * pos_i)
    dec_scr[6] = jnp.broadcast_to(jnp.exp(lg_f * c), (c, c))
    dec_scr[7] = jnp.broadcast_to(jnp.exp(lg_b * c), (c, c))

    def one_dir(d, start, out_scr, state):
        qc = q_ref[0, pl.ds(start, c), :]
        kc = k_ref[0, pl.ds(start, c), :]
        vc = v_ref[0, pl.ds(start, c), :]
        scores = lax.dot_general(qc, kc, (((1,), (1,)), ((), ())), preferred_element_type=F32)
        intra = jnp.dot((scores * dec_scr[d]).astype(BF16), vc, preferred_element_type=F32)
        inter = jnp.dot(qc, state.astype(BF16), preferred_element_type=F32)
        out_scr[pl.ds(start, c), :] = intra + dec_scr[2 + d] * inter
        vd = (vc.astype(F32) * dec_scr[4 + d]).astype(BF16)
        kv = lax.dot_general(kc, vd, (((0,), (0,)), ((), ())), preferred_element_type=F32)
        return dec_scr[6 + d] * state + kv

    s_f = jnp.zeros((LANES, LANES), F32)
    s_b = jnp.zeros((LANES, LANES), F32)
    n_ctx_chunks = n_ctx // c
    for ci in range(n_ctx_chunks):
        s_f = one_dir(0, n_lat + ci * c, of_scr, s_f)
        s_b = one_dir(1, n_lat + (n_ctx_chunks - 1 - ci) * c, ob_scr, s_b)

    n_lat_chunks = n_lat // c

    def body(i, states):
        s_f = one_dir(0, pl.multiple_of(i * c, c), of_scr, states[0])
        s_b = one_dir(1, pl.multiple_of((n_lat_chunks - 1 - i) * c, c), ob_scr, states[1])
        return s_f, s_b

    lax.fori_loop(0, n_lat_chunks, body, (s_f, s_b), unroll=_pick_tile(n_lat_chunks, (8, 4, 2, 1)))

    def finish(i, carry):
        r0 = pl.multiple_of(i * c, c)
        o = of_scr[pl.ds(r0, c), :] + ob_scr[pl.ds(r0, c), :]
        o = o * lax.rsqrt(jnp.mean(o * o, axis=-1, keepdims=True) + EPS)
        g = g_ref[0, pl.ds(r0, c), :].astype(F32)
        o_ref[0, pl.ds(r0, c), :] = (o * (g * _sigmoid(g))).astype(BF16)
        return carry

    n_chunks = (n_lat + n_ctx) // c
    lax.fori_loop(0, n_chunks, finish, 0, unroll=_pick_tile(n_chunks, (6, 3, 2, 1)))


def _retention(hin, dec_f, dec_b, n_lat):
    b, t, _ = hin.shape
    h = RET_HEADS
    seq = lambda col0: pl.BlockSpec((1, t, LANES), lambda bi, hi: (bi, 0, col0 + hi))
    dec = pl.BlockSpec((1, 1, LANES), lambda bi, hi: (hi, 0, 0))
    return pl.pallas_call(
        functools.partial(_retention_kernel, n_lat=n_lat, n_ctx=t - n_lat),
        grid=(b, h),
        in_specs=[seq(4), seq(8), seq(12), seq(16), dec, dec],
        out_specs=pl.BlockSpec((1, t, LANES), lambda bi, hi: (bi, 0, hi)),
        out_shape=jax.ShapeDtypeStruct((b, t, h * LANES), BF16),
        scratch_shapes=[
            pltpu.VMEM((t, LANES), F32),
            pltpu.VMEM((t, LANES), F32),
            pltpu.VMEM((8, RET_CHUNK, LANES), F32),
        ],
        compiler_params=_cparams("parallel", "parallel"),
        name="retention",
    )(hin, hin, hin, hin, dec_f, dec_b)


def _mixer_residual(x, a_refs, w_refs, modl_ref, modc_ref, row0, n_lat):
    y = jnp.dot(a_refs[0][0], w_refs[0][...], preferred_element_type=F32)
    for a_ref, w_ref in zip(a_refs[1:], w_refs[1:]):
        y = y + jnp.dot(a_ref[0], w_ref[...], preferred_element_type=F32)
    return x + _gate_rows(modl_ref, modc_ref, 2, row0, x.shape[0], n_lat) * y


def _mixer_specs(parts, weights, tm):
    specs = [pl.BlockSpec((1, tm, p.shape[2]), lambda bi, i: (bi, i, 0)) for p in parts]
    return specs + [_resident_spec(w.shape) for w in weights]


def _attn_kernel(prev_ref, q_ref, k_ref, v_ref, lam_ref, sg_ref, o_ref, *, tk, lam_init):
    del prev_ref
    tq = q_ref.shape[1]
    n_keys = k_ref.shape[1]
    q = q_ref[0]
    lane = lax.broadcasted_iota(jnp.int32, (tq, LANES), 1)
    zero = jnp.zeros_like(q)
    q_both = jnp.concatenate([jnp.where(lane < HEAD_DIM, q, zero), jnp.where(lane >= HEAD_DIM, q, zero)], axis=0)
    ones = jnp.ones((tk, LANES), BF16)
    m = [None, None]
    acc = [None, None]
    for j in range(n_keys // tk):
        kc = k_ref[0, j * tk:(j + 1) * tk, :]
        v1 = jnp.concatenate([v_ref[0, j * tk:(j + 1) * tk, :], ones], axis=1)
        s_both = lax.dot_general(q_both, kc, (((1,), (1,)), ((), ())), preferred_element_type=F32)
        for sub in range(2):
            s = s_both[sub * tq:(sub + 1) * tq]
            mx = jnp.max(s, axis=-1, keepdims=True)
            if j == 0:
                m[sub] = mx
                p = jnp.exp2(s - mx).astype(BF16)
                acc[sub] = jnp.dot(p, v1, preferred_element_type=F32)
            else:
                m_new = jnp.maximum(m[sub], mx)
                alpha = jnp.exp2(m[sub] - m_new)
                p = jnp.exp2(s - m_new).astype(BF16)
                acc[sub] = alpha * acc[sub] + jnp.dot(p, v1, preferred_element_type=F32)
                m[sub] = m_new

    lv = lam_ref[...]
    s1 = jnp.sum(lv[0:1] * lv[1:2], axis=-1, keepdims=True)
    s2 = jnp.sum(lv[2:3] * lv[3:4], axis=-1, keepdims=True)
    lam = jnp.exp(s1) - jnp.exp(s2) + lam_init
    o = acc[0][:, :LANES] / acc[0][:, LANES:] - lam * (acc[1][:, :LANES] / acc[1][:, LANES:])
    o = o * lax.rsqrt(jnp.mean(o * o, axis=-1, keepdims=True) + EPS)
    o_ref[0] = (o * sg_ref[...] * (1.0 - lam_init)).astype(BF16)


def _attention(out, qkv, lam_vecs, sub_gain, lam_init, n_lat, ctx_queries):
    b, t, _ = qkv.shape
    h = DIFF_HEADS
    n_ctx = t - n_lat
    if ctx_queries:
        tq, n_q_tiles, q_off = n_ctx, 1, n_lat // n_ctx
        keys, k_off = n_ctx, n_lat // n_ctx
        tk = n_ctx
    else:
        tq = _pick_tile(n_lat, (1024, 512, 256))
        n_q_tiles, q_off = n_lat // tq, 0
        keys, k_off = t, 0
        tk = _pick_tile(t, (768, 256))
    return pl.pallas_call(
        functools.partial(_attn_kernel, tk=tk, lam_init=lam_init),
        grid=(b, h, n_q_tiles),
        in_specs=[
            pl.BlockSpec(memory_space=pl.ANY),
            pl.BlockSpec((1, tq, LANES), lambda bi, hi, i: (bi, q_off + i, hi)),
            pl.BlockSpec((1, keys, LANES), lambda bi, hi, i: (bi, k_off, h + hi)),
            pl.BlockSpec((1, keys, LANES), lambda bi, hi, i: (bi, k_off, 2 * h + hi)),
            _const_spec((4, LANES)),
            _const_spec((1, LANES)),
        ],
        out_specs=pl.BlockSpec((1, tq, LANES), lambda bi, hi, i: (bi, q_off + i, hi)),
        out_shape=jax.ShapeDtypeStruct((b, t, h * LANES), BF16),
        input_output_aliases={0: 0},
        compiler_params=_cparams("parallel", "parallel", "arbitrary"),
        name="diff_attn_ctx" if ctx_queries else "diff_attn_lat",
    )(out, qkv, qkv, qkv, lam_vecs, sub_gain)


def _ffn_kernel(*refs, n_parts, n_lat, tm, tf):
    x_ref, modl_ref, modc_ref = refs[:3]
    a_refs, w_refs = refs[3:3 + n_parts], refs[3 + n_parts:3 + 2 * n_parts]
    w1_ref, w3_ref, w2_ref, o_ref = refs[3 + 2 * n_parts:]
    row0 = pl.program_id(1) * tm
    x = _mixer_residual(x_ref[0], a_refs, w_refs, modl_ref, modc_ref, row0, n_lat)
    h = _modulated(x, modl_ref, modc_ref, 3, row0, n_lat).astype(BF16)
    f_dim = w1_ref.shape[1]
    y = jnp.zeros((tm, x.shape[1]), F32)
    for c in range(f_dim // tf):
        sl = slice(c * tf, (c + 1) * tf)
        a = jnp.dot(h, w1_ref[:, sl], preferred_element_type=F32)
        g = jnp.dot(h, w3_ref[:, sl], preferred_element_type=F32)
        u = (a * _sigmoid(a) * g).astype(BF16)
        y = y + jnp.dot(u, w2_ref[sl, :], preferred_element_type=F32)
    gate = _gate_rows(modl_ref, modc_ref, 5, row0, tm, n_lat)
    o_ref[0] = x + gate * y


def _ffn(x, mod, parts, out_weights, w1, w3, w2, n_lat):
    b, t, d = x.shape
    tm = _pick_tile(t, (768, 256))
    tf = _pick_tile(w1.shape[1], (256, 128))
    return pl.pallas_call(
        functools.partial(_ffn_kernel, n_parts=len(parts), n_lat=n_lat, tm=tm, tf=tf),
        grid=(b, t // tm),
        in_specs=[
            pl.BlockSpec((1, tm, d), lambda bi, i: (bi, i, 0)),
            pl.BlockSpec((1, N_MOD, d), lambda bi, i: (bi, 0, 0)),
            pl.BlockSpec((1, N_MOD, d), lambda bi, i: (b, 0, 0)),
            *_mixer_specs(parts, out_weights, tm),
            _resident_spec(w1.shape),
            _resident_spec(w3.shape),
            _resident_spec(w2.shape),
        ],
        out_specs=pl.BlockSpec((1, tm, d), lambda bi, i: (bi, i, 0)),
        out_shape=jax.ShapeDtypeStruct((b, t, d), F32),
        input_output_aliases={0: 0},
        compiler_params=_cparams("parallel", "parallel"),
        name="mixer_out_ffn",
    )(x, mod, mod, *parts, *out_weights, w1, w3, w2)


ROUTE_E1, ROUTE_E2, ROUTE_G1, ROUTE_G2 = 0, 1, 2, 3
SLABS = D_MODEL // LANES
DMA_LOOP_UNROLL = 8


def _top2_route(logits):
    tm = logits.shape[0]
    lane = lax.broadcasted_iota(jnp.int32, (tm, LANES), 1)
    neg = jnp.float32(-jnp.inf)
    lg = jnp.where(lane < N_EXPERTS, logits, neg)
    m1 = jnp.max(lg, axis=-1, keepdims=True)
    i1 = jnp.min(jnp.where(lg == m1, lane, LANES), axis=-1, keepdims=True)
    lg2 = jnp.where(lane == i1, neg, lg)
    m2 = jnp.max(lg2, axis=-1, keepdims=True)
    i2 = jnp.min(jnp.where(lg2 == m2, lane, LANES), axis=-1, keepdims=True)
    e = jnp.exp(m2 - m1)
    g1 = 1.0 / (1.0 + e)
    g2 = e / (1.0 + e)
    rec = jnp.where(lane == ROUTE_E1, i1.astype(F32), 0.0) + jnp.where(lane == ROUTE_E2, i2.astype(F32), 0.0)
    return rec + jnp.where(lane == ROUTE_G1, g1, 0.0) + jnp.where(lane == ROUTE_G2, g2, 0.0)


def _slab_store(ref, val, n_rows):
    for s in range(SLABS):
        ref[pl.ds(s, n_rows, stride=SLABS), :] = val[:, s * LANES:(s + 1) * LANES]


def _slab_load(ref, n_rows):
    return jnp.concatenate([ref[pl.ds(s, n_rows, stride=SLABS), :] for s in range(SLABS)], axis=1)


def _mixer_out_kernel(*refs, n_parts, n_lat, tm):
    x_ref, modl_ref, modc_ref = refs[:3]
    a_refs, w_refs = refs[3:3 + n_parts], refs[3 + n_parts:3 + 2 * n_parts]
    o_ref = refs[3 + 2 * n_parts]
    row0 = pl.program_id(1) * tm
    o_ref[0] = _mixer_residual(x_ref[0], a_refs, w_refs, modl_ref, modc_ref, row0, n_lat)


def _mixer_out(x, mod, parts, out_weights, n_lat, n_rows):
    b, t, d = x.shape
    tm = _pick_tile(math.gcd(n_rows, x.shape[1]), (768, 512, 256))
    return pl.pallas_call(
        functools.partial(_mixer_out_kernel, n_parts=len(parts), n_lat=n_lat, tm=tm),
        grid=(b, n_rows // tm),
        in_specs=[
            pl.BlockSpec((1, tm, d), lambda bi, i: (bi, i, 0)),
            pl.BlockSpec((1, N_MOD, d), lambda bi, i: (bi, 0, 0)),
            pl.BlockSpec((1, N_MOD, d), lambda bi, i: (b, 0, 0)),
            *_mixer_specs(parts, out_weights, tm),
        ],
        out_specs=pl.BlockSpec((1, tm, d), lambda bi, i: (bi, i, 0)),
        out_shape=jax.ShapeDtypeStruct((b, t, d), F32),
        input_output_aliases={0: 0},
        compiler_params=_cparams("parallel", "parallel"),
        name="mixer_out",
    )(x, mod, mod, *parts, *out_weights)


def _route_kernel(x_ref, modl_ref, modc_ref, wr_ref, h_ref, rec_ref, *, n_lat, tm):
    row0 = pl.program_id(1) * tm
    hf = _modulated(x_ref[0], modl_ref, modc_ref, 3, row0, n_lat)
    logits = jnp.dot(hf, wr_ref[...], preferred_element_type=F32, precision=lax.Precision.HIGHEST)
    rec_ref[0] = _top2_route(logits)
    _slab_store(h_ref, hf, tm)


def _moe_route(x, mod, w_router, n_lat, n_rows):
    b, _, d = x.shape
    tm = _pick_tile(math.gcd(n_rows, x.shape[1]), (768, 512, 256))
    tiles = n_rows // tm
    return pl.pallas_call(
        functools.partial(_route_kernel, n_lat=n_lat, tm=tm),
        grid=(b, tiles),
        in_specs=[
            pl.BlockSpec((1, tm, d), lambda bi, i: (bi, i, 0)),
            pl.BlockSpec((1, N_MOD, d), lambda bi, i: (bi, 0, 0)),
            pl.BlockSpec((1, N_MOD, d), lambda bi, i: (b, 0, 0)),
            _const_spec((d, LANES)),
        ],
        out_specs=[
            pl.BlockSpec((tm * SLABS, LANES), lambda bi, i: (bi * tiles + i, 0)),
            pl.BlockSpec((1, tm, LANES), lambda bi, i: (bi, i, 0)),
        ],
        out_shape=[
            jax.ShapeDtypeStruct((b * n_rows * SLABS, LANES), F32),
            jax.ShapeDtypeStruct((b, n_rows, LANES), F32),
        ],
        compiler_params=_cparams("parallel", "parallel"),
        name="moe_route",
    )(x, mod, mod, w_router)


def _row_copy(src_ref, src_row, dst_ref, dst_row, sem):
    return pltpu.make_async_copy(
        src_ref.at[pl.ds(pl.multiple_of(src_row * SLABS, SLABS), SLABS), :],
        dst_ref.at[pl.ds(pl.multiple_of(dst_row * SLABS, SLABS), SLABS), :],
        sem)


def _dispatch_kernel(dest_ref, h_ref, init_ref, xs_ref, sem, *, tm):
    del init_ref

    def start(r, carry):
        for slot in range(2):
            _row_copy(h_ref, r, xs_ref, dest_ref[0, 0, 2 * r + slot], sem).start(priority=slot)
        return carry

    lax.fori_loop(0, tm, start, 0, unroll=DMA_LOOP_UNROLL)
    for _ in range(2):
        pltpu.make_async_copy(h_ref, xs_ref.at[pl.ds(0, tm * SLABS), :], sem).wait()


def _moe_dispatch(h_slabs, dest, n_sorted_rows):
    n_tok = h_slabs.shape[0] // SLABS
    tm = _pick_tile(n_tok, (512, 256))
    steps = n_tok // tm
    return pl.pallas_call(
        functools.partial(_dispatch_kernel, tm=tm),
        grid=(steps,),
        in_specs=[
            pl.BlockSpec((1, 1, 2 * tm), lambda i: (i, 0, 0), memory_space=pltpu.SMEM),
            pl.BlockSpec((tm * SLABS, LANES), lambda i: (i, 0)),
            pl.BlockSpec(memory_space=pl.ANY),
        ],
        out_specs=pl.BlockSpec(memory_space=pl.ANY),
        out_shape=jax.ShapeDtypeStruct((n_sorted_rows * SLABS, LANES), F32),
        scratch_shapes=[pltpu.SemaphoreType.DMA(())],
        input_output_aliases={2: 0},
        compiler_params=_cparams("arbitrary"),
        name="moe_dispatch",
    )(dest.reshape(steps, 1, 2 * tm), h_slabs, jnp.zeros((n_sorted_rows * SLABS, LANES), F32))


def _experts_kernel(te_ref, used_ref, x_ref, w1_ref, w3_ref, w2_ref, o_ref, acc_scr, *, tm, tf, n_steps_f):
    del te_ref
    i = pl.program_id(0)
    f = pl.program_id(1)
    valid = i < used_ref[0]
    n_f = pl.num_programs(1)

    def step(first, last):
        h = _slab_load(x_ref, tm).astype(BF16)
        f_blk = w1_ref.shape[2]
        y = None if first else acc_scr[...]
        for c in range(f_blk // tf):
            sl = slice(c * tf, (c + 1) * tf)
            a = jnp.dot(h, w1_ref[0, :, sl], preferred_element_type=F32)
            g = jnp.dot(h, w3_ref[0, :, sl], preferred_element_type=F32)
            u = (a * _sigmoid(a) * g).astype(BF16)
            part = jnp.dot(u, w2_ref[0, sl, :], preferred_element_type=F32)
            y = part if y is None else y + part
        if last:
            _slab_store(o_ref, y, tm)
        else:
            acc_scr[...] = y

    if n_steps_f == 1:
        pl.when(valid)(lambda: step(True, True))
    else:
        pl.when(valid & (f == 0))(lambda: step(True, False))
        pl.when(valid & (f == n_f - 1))(lambda: step(False, True))
        if n_steps_f > 2:
            pl.when(valid & (f > 0) & (f < n_f - 1))(lambda: step(False, False))

    @pl.when(jnp.logical_not(valid) & (f == n_f - 1))
    def _():
        _slab_store(o_ref, jnp.zeros(acc_scr.shape, F32), tm)


def _moe_experts(xs, tile_expert, n_used, w1, w3, w2, tm):
    n_exp, d, f_dim = w1.shape
    n_tiles = xs.shape[0] // (tm * SLABS)
    f_blk = _pick_tile(f_dim, (1792, 512, 256))
    tf = _pick_tile(f_blk, (256, 128))
    n_f = f_dim // f_blk

    def f_eff(i, f, used):
        return jnp.where(i < used[0], f, n_f - 1)

    grid_spec = pltpu.PrefetchScalarGridSpec(
        num_scalar_prefetch=2,
        grid=(n_tiles, n_f),
        in_specs=[
            pl.BlockSpec((tm * SLABS, LANES), lambda i, f, te, used: (jnp.minimum(i, used[0] - 1), 0)),
            pl.BlockSpec((1, d, f_blk), lambda i, f, te, used: (te[i], 0, f_eff(i, f, used))),
            pl.BlockSpec((1, d, f_blk), lambda i, f, te, used: (te[i], 0, f_eff(i, f, used))),
            pl.BlockSpec((1, f_blk, d), lambda i, f, te, used: (te[i], f_eff(i, f, used), 0)),
        ],
        out_specs=pl.BlockSpec((tm * SLABS, LANES), lambda i, f, te, used: (i, 0)),
        scratch_shapes=[pltpu.VMEM((tm, d), F32)],
    )
    return pl.pallas_call(
        functools.partial(_experts_kernel, tm=tm, tf=tf, n_steps_f=n_f),
        grid_spec=grid_spec,
        out_shape=jax.ShapeDtypeStruct(xs.shape, F32),
        compiler_params=_cparams("arbitrary", "arbitrary"),
        name="moe_experts",
    )(tile_expert, n_used, xs, w1, w3, w2)


def _combine_kernel(dcur_ref, dnext_ref, x_ref, modl_ref, modc_ref, rec_ref, ys_ref, o_ref, buf, sems,
                    *, n_lat, tm, tiles_per_batch):
    i = pl.program_id(0)
    n = pl.num_programs(0)

    def start_gather(dest_ref, slot_buf):
        def body(r, carry):
            for slot in range(2):
                _row_copy(ys_ref, dest_ref[0, 0, 2 * r + slot], buf.at[slot_buf, slot], r,
                          sems.at[slot_buf]).start(priority=slot)
            return carry
        lax.fori_loop(0, tm, body, 0, unroll=DMA_LOOP_UNROLL)

    def wait_gather(slot_buf):
        for slot in range(2):
            pltpu.make_async_copy(ys_ref.at[pl.ds(0, tm * SLABS), :], buf.at[slot_buf, slot], sems.at[slot_buf]).wait()

    @pl.when(i == 0)
    def _():
        start_gather(dcur_ref, 0)

    for parity in range(2):
        @pl.when((i + 1 < n) & ((i + 1) % 2 == parity))
        def _():
            start_gather(dnext_ref, parity)

    for parity in range(2):
        @pl.when(i % 2 == parity)
        def _():
            wait_gather(parity)
            rec = rec_ref[0]
            y = (rec[:, ROUTE_G1:ROUTE_G1 + 1] * _slab_load(buf.at[parity, 0], tm)
                 + rec[:, ROUTE_G2:ROUTE_G2 + 1] * _slab_load(buf.at[parity, 1], tm))
            row0 = (i % tiles_per_batch) * tm
            gate = _gate_rows(modl_ref, modc_ref, 5, row0, tm, n_lat)
            o_ref[0] = x_ref[0] + gate * y


def _moe_combine(x, mod, rec, ys, dest, n_lat, n_rows):
    b, t, d = x.shape
    tm = _pick_tile(n_rows, (256,))
    tpb = n_rows // tm
    n_tiles = b * tpb
    dest3 = dest.reshape(n_tiles, 1, 2 * tm)
    tok = lambda i: (i // tpb, i % tpb, 0)
    return pl.pallas_call(
        functools.partial(_combine_kernel, n_lat=n_lat, tm=tm, tiles_per_batch=tpb),
        grid=(n_tiles,),
        in_specs=[
            pl.BlockSpec((1, 1, 2 * tm), lambda i: (i, 0, 0), memory_space=pltpu.SMEM),
            pl.BlockSpec((1, 1, 2 * tm), lambda i: (jnp.minimum(i + 1, n_tiles - 1), 0, 0),
                         memory_space=pltpu.SMEM),
            pl.BlockSpec((1, tm, d), tok),
            pl.BlockSpec((1, N_MOD, d), lambda i: (i // tpb, 0, 0)),
            pl.BlockSpec((1, N_MOD, d), lambda i: (b, 0, 0)),
            pl.BlockSpec((1, tm, LANES), tok),
            pl.BlockSpec(memory_space=pl.ANY),
        ],
        out_specs=pl.BlockSpec((1, tm, d), tok),
        out_shape=jax.ShapeDtypeStruct((b, n_rows, d), F32),
        scratch_shapes=[pltpu.VMEM((2, 2, tm * SLABS, LANES), F32), pltpu.SemaphoreType.DMA((2,))],
        input_output_aliases={2: 0} if n_rows == t else {},
        compiler_params=_cparams("arbitrary"),
        name="moe_combine",
    )(dest3, dest3, x, mod, mod, rec, ys)


def _dispatch_plan(rec, tm, n_tiles):
    experts = rec[:, ROUTE_E1:ROUTE_E2 + 1].astype(jnp.int32).reshape(-1)
    onehot = (experts[:, None] == jnp.arange(N_EXPERTS, dtype=jnp.int32)[None, :]).astype(jnp.int32)
    csum = jnp.cumsum(onehot, axis=0)
    counts = csum[-1]
    padded = (counts + tm - 1) // tm * tm
    ends = jnp.cumsum(padded)
    dest = jnp.sum(onehot * (csum - 1 + (ends - padded)[None, :]), axis=1)
    n_used = (ends[-1] // tm).astype(jnp.int32)
    tile_start = jnp.arange(n_tiles, dtype=jnp.int32) * tm
    tile_expert = jnp.sum((tile_start[:, None] >= ends[None, :]).astype(jnp.int32), axis=1)
    tile_expert = jnp.minimum(tile_expert, tile_expert[n_used - 1])
    return dest.astype(jnp.int32), tile_expert.astype(jnp.int32), n_used.reshape(1)


def _moe(x, mod, w_router, w1, w3, w2, first_expert, n_lat, n_rows):
    b, _, d = x.shape
    n_tok = b * n_rows
    tm = 512
    n_tiles = (2 * n_tok + N_EXPERTS * (tm - 1)) // tm
    h_slabs, rec = _moe_route(x, mod, w_router, n_lat, n_rows)
    dest, tile_expert, n_used = _dispatch_plan(rec.reshape(n_tok, LANES), tm, n_tiles)
    xs = _moe_dispatch(h_slabs, dest, n_tiles * tm)
    ys = _moe_experts(xs, tile_expert + first_expert, n_used, w1, w3, w2, tm)
    return _moe_combine(x, mod, rec, ys, dest, n_lat, n_rows)


def _rot_cols(w):
    half = HEAD_DIM // 2
    return jnp.concatenate([-w[..., half:], w[..., :half]], axis=-1)


def _with_rot(w_heads):
    d, n, _ = w_heads.shape
    return jnp.concatenate([w_heads, _rot_cols(w_heads)], axis=-1).reshape(d, n * LANES)


def _gain_lanes(gain):
    return jnp.tile(gain.astype(F32), LANES // HEAD_DIM).reshape(1, LANES)


def _head_sum_matrix():
    idx = jnp.arange(LANES, dtype=jnp.int32) // HEAD_DIM
    return (idx[:, None] == idx[None, :]).astype(BF16)


def _rope_table(n_lat, n_ctx):
    rows = n_lat // GRID_W
    row = jnp.repeat(jnp.arange(rows, dtype=F32), GRID_W)
    col = jnp.tile(jnp.arange(GRID_W, dtype=F32), rows)
    quarter = HEAD_DIM // 4
    inv_freq = ROPE_BASE ** (-jnp.arange(quarter, dtype=F32) / quarter)
    ang = jnp.concatenate([row[:, None] * inv_freq, col[:, None] * inv_freq], axis=-1)
    cos, sin = jnp.cos(ang), jnp.sin(ang)
    ones = jnp.ones((n_ctx, HEAD_DIM), F32)
    zeros = jnp.zeros((n_ctx, HEAD_DIM), F32)
    rows_of = lambda lat, ctx: jnp.concatenate([lat, ctx], axis=0)
    dup = rows_of(jnp.concatenate([cos, cos, sin, sin], axis=-1), jnp.concatenate([ones, zeros], axis=-1))
    cos_packed = rows_of(jnp.concatenate([cos] * 4, axis=-1), jnp.concatenate([ones, ones], axis=-1))
    sin_packed = rows_of(jnp.concatenate([-sin, sin, -sin, sin], axis=-1), jnp.concatenate([zeros, zeros], axis=-1))
    return dup, cos_packed, sin_packed


def _chan_tables():
    idx = jnp.arange(FNET_GROUP_DIM, dtype=jnp.int32)
    ang = ((idx[:, None] * idx[None, :]) % FNET_GROUP_DIM).astype(F32) * (2.0 * math.pi / FNET_GROUP_DIM)
    return jnp.cos(ang).astype(BF16), jnp.sin(ang).astype(BF16)


def kernel(x, c, ctx, c_ctx, w_mod, b_mod, w_in_even, w_out_even, ret_decay_fwd, ret_decay_bwd,
           ffn_w1, ffn_w3, ffn_w2, w_in_odd, w_out_odd, q_norm_gain, k_norm_gain,
           lambda_q1, lambda_k1, lambda_q2, lambda_k2, subln_gain, w_router, moe_w1, moe_w3, moe_w2):
    b, n_lat, d = x.shape
    n_ctx = ctx.shape[1]
    depth = w_mod.shape[0]
    assert d == D_MODEL and b < MOD_ROWS and n_lat % n_ctx == 0 and n_ctx % RET_CHUNK == 0

    stream = jnp.concatenate([x, ctx], axis=1)
    cond = jnp.zeros((MOD_ROWS, d), F32).at[:b].set(c).at[b].set(c_ctx)
    mods = _adaln(cond, w_mod, b_mod).reshape(depth, MOD_ROWS, N_MOD, d)
    tab, cos_packed, sin_packed = _rope_table(n_lat, n_ctx)
    chan_tabs = _chan_tables()
    stack_experts = lambda w: w.astype(BF16).reshape((-1,) + w.shape[2:])
    moe_w1_all, moe_w3_all, moe_w2_all = stack_experts(moe_w1), stack_experts(moe_w3), stack_experts(moe_w2)

    for layer in range(depth):
        i = layer // 2
        mod = mods[layer]
        if layer % 2 == 0:
            w = w_in_even[i]
            hq = RET_HEADS * HEAD_DIM
            wq = w[:, FNET_WIDTH:FNET_WIDTH + hq].reshape(d, RET_HEADS, HEAD_DIM)
            wk = w[:, FNET_WIDTH + hq:FNET_WIDTH + 2 * hq].reshape(d, RET_HEADS, HEAD_DIM)
            w_in = jnp.concatenate(
                [w[:, :FNET_WIDTH], _with_rot(wq), _with_rot(wk), w[:, FNET_WIDTH + 2 * hq:]], axis=1).astype(BF16)
            hin = _inproj(stream, mod, w_in, [tab], [], n_lat, w_in.shape[1], True)
            four = jnp.concatenate(
                [_fourier_seq(hin, 0, n_lat, chan_tabs), _fourier_seq(hin, n_lat, n_ctx, chan_tabs)], axis=1)
            dec_f = jnp.broadcast_to(ret_decay_fwd[i].astype(F32)[:, None, None], (RET_HEADS, 1, LANES))
            dec_b = jnp.broadcast_to(ret_decay_bwd[i].astype(F32)[:, None, None], (RET_HEADS, 1, LANES))
            ret = _retention(hin, dec_f, dec_b, n_lat)
            w_out = w_out_even[i].astype(BF16)
            stream = _ffn(stream, mod, [four, ret], [w_out[:FNET_WIDTH], w_out[FNET_WIDTH:]],
                          ffn_w1[i].astype(BF16), ffn_w3[i].astype(BF16), ffn_w2[i].astype(BF16), n_lat)
        else:
            consts = [_gain_lanes(q_norm_gain[i]), _gain_lanes(k_norm_gain[i]), _head_sum_matrix()]
            qkv = _inproj(stream, mod, w_in_odd[i].astype(BF16), [cos_packed, sin_packed], consts, n_lat,
                          3 * DIFF_HEADS * LANES, False)
            lam_init = 0.8 - 0.6 * float(np.exp(-0.3 * layer))
            lam_vecs = jnp.zeros((4, LANES), F32).at[:, :HEAD_DIM].set(
                jnp.stack([lambda_q1[i], lambda_k1[i], lambda_q2[i], lambda_k2[i]]).astype(F32))
            sub_gain = subln_gain[i].astype(F32).reshape(1, LANES)
            att = jnp.zeros((b, n_lat + n_ctx, DIFF_HEADS * LANES), BF16)
            last = layer == depth - 1
            n_rows = n_lat if last else n_lat + n_ctx
            att = _attention(att, qkv, lam_vecs, sub_gain, lam_init, n_lat, False)
            if not last:
                att = _attention(att, qkv, lam_vecs, sub_gain, lam_init, n_lat, True)
            wr = jnp.zeros((d, LANES), F32).at[:, :N_EXPERTS].set(w_router[i].astype(F32))
            stream = _mixer_out(stream, mod, [att], [w_out_odd[i].astype(BF16)], n_lat, n_rows)
            stream = _moe(stream, mod, wr, moe_w1_all, moe_w3_all, moe_w2_all, i * N_EXPERTS, n_lat, n_rows)
    return stream[:, :n_lat]
```
